```python
import jax, jax.numpy as jnp
from jax import lax
import numpy as np

D_MODEL = 1024
BATCH = 8
SEQ = 4096
DEPTH = 4

N_MIXERS = 2
N_POOL_LAYERS = (DEPTH + N_MIXERS - 1) // N_MIXERS
N_LRU_LAYERS = DEPTH // N_MIXERS
N_META = 16
POOL_WINDOWS = (2, 4, 8, 16)
N_POOL_GROUPS = len(POOL_WINDOWS)
POOL_GROUP = D_MODEL // N_POOL_GROUPS
LRU_WIDTH = D_MODEL
LRU_HEADS = 4
LRU_HEAD_DIM = LRU_WIDTH // LRU_HEADS
LRU_C = 8.0
CONV_WIDTH = 4
CONV_LEFT = 1
N_DIRS = 2
N_EXPERTS = 16
CAPACITY_FACTOR = 2
D_EXPERT = 2048
RMS_EPS = 1e-6

kernel_name = "hybrid_pool_rglru_ec_moe_encoder"


def _rmsnorm(x, g):
    xf = x.astype(jnp.float32)
    y = xf * lax.rsqrt(jnp.mean(xf * xf, axis=-1, keepdims=True) + RMS_EPS)
    return (y * g.astype(jnp.float32)).astype(x.dtype)


def _pool_mixer(u, w, scale):
    B, T, _ = u.shape
    uf = u.astype(jnp.float32)
    cs = jnp.pad(jnp.cumsum(uf, axis=1), ((0, 0), (1, 0), (0, 0)))
    t = jnp.arange(T)
    outs = []
    for g, win in enumerate(POOL_WINDOWS):
        sl = slice(g * POOL_GROUP, (g + 1) * POOL_GROUP)
        lo = jnp.clip(t - win // 2, 0, T)
        hi = jnp.clip(t + win - win // 2, 0, T)
        cnt = (hi - lo).astype(jnp.float32)[:, None]
        csg = cs[..., sl]
        mean = (jnp.take(csg, hi, axis=1) - jnp.take(csg, lo, axis=1)) / cnt
        outs.append(mean - uf[..., sl])
    p = jnp.stack(outs, axis=2).astype(u.dtype)
    y = jnp.einsum('btgc,gcd->btgd', p, w).reshape(B, T, D_MODEL)
    return y * scale


def _lru_combine(left, right):
    a1, b1 = left
    a2, b2 = right
    return a1 * a2, a2 * b1 + b2


def _rglru_mixer(u, w_in, conv_w, conv_b, w_gates, b_gates, lam, w_out):
    B, T, _ = u.shape
    xy = jnp.einsum('btd,de->bte', u, w_in)
    xb, yb = jnp.split(xy, 2, axis=-1)
    xp = jnp.pad(xb, ((0, 0), (CONV_LEFT, CONV_WIDTH - 1 - CONV_LEFT), (0, 0)))
    xc = conv_b + sum(xp[:, k:k + T] * conv_w[k] for k in range(CONV_WIDTH))
    xh = xc.reshape(B, T, LRU_HEADS, LRU_HEAD_DIM)
    xf = xc.astype(jnp.float32)
    hs = []
    for d in range(N_DIRS):
        gl = jnp.einsum('bthi,ghij->gbthj', xh, w_gates[d]) + b_gates[d][:, None, None]
        gates = jax.nn.sigmoid(gl.astype(jnp.float32)).reshape(2, B, T, LRU_WIDTH)
        r, i = gates[0], gates[1]
        log_a = -LRU_C * r * jax.nn.softplus(-lam[d].astype(jnp.float32))
        a = jnp.exp(log_a)
        b = jnp.sqrt(-jnp.expm1(2.0 * log_a)) * (i * xf)
        _, h = lax.associative_scan(_lru_combine, (a, b), axis=1, reverse=(d == 1))
        hs.append(h)
    hsum = (hs[0] + hs[1]).astype(u.dtype)
    return jnp.einsum('bte,ed->btd', hsum * jax.nn.gelu(yb), w_out)


def _expert_choice_ffn(u, router_w, w_gate, w_up, w_down):
    B, T, D = u.shape
    cap = CAPACITY_FACTOR * T // N_EXPERTS
    logits = jnp.einsum('btd,de->bte', u, router_w).astype(jnp.float32)
    aff = jax.nn.softmax(logits, axis=-1)
    g, idx = lax.top_k(jnp.swapaxes(aff, 1, 2), cap)
    flat_idx = idx.reshape(B, N_EXPERTS * cap)
    xs = jax.vmap(lambda ub, ib: ub[ib])(u, flat_idx).reshape(B, N_EXPERTS, cap, D)
    hid = jax.nn.silu(jnp.einsum('becd,edf->becf', xs, w_gate)) * \
        jnp.einsum('becd,edf->becf', xs, w_up)
    ys = jnp.einsum('becf,efd->becd', hid, w_down) * g[..., None].astype(u.dtype)
    out = jax.vmap(lambda yb, ib: jnp.zeros((T, D), u.dtype).at[ib].add(yb))(
        ys.reshape(B, N_EXPERTS * cap, D), flat_idx)
    return out


def setup_inputs(seed: int = 0) -> dict:
    key = jax.random.key(seed)
    ks = jax.random.split(key, 20)
    f32 = jnp.float32
    nrm = lambda k, s: jax.random.normal(k, s, f32)
    R, H, hd = LRU_WIDTH, LRU_HEADS, LRU_HEAD_DIM
    a8 = jax.random.uniform(ks[12], (N_LRU_LAYERS, N_DIRS, R), f32, 0.9, 0.999)
    s = a8 ** (1.0 / LRU_C)
    lam = jnp.log(s) - jnp.log1p(-s)
    return {
        "x": nrm(ks[0], (BATCH, SEQ, D_MODEL)),
        "meta_tokens": nrm(ks[1], (N_META, D_MODEL)),
        "norm_mix": 1.0 + 0.02 * nrm(ks[2], (DEPTH, D_MODEL)),
        "norm_ffn": 1.0 + 0.02 * nrm(ks[3], (DEPTH, D_MODEL)),
        "norm_final": 1.0 + 0.02 * nrm(ks[4], (D_MODEL,)),
        "pool_w": nrm(ks[5], (N_POOL_LAYERS, N_POOL_GROUPS, POOL_GROUP, POOL_GROUP)) * POOL_GROUP ** -0.5,
        "pool_scale": 1.0 + 0.02 * nrm(ks[6], (N_POOL_LAYERS, D_MODEL)),
        "lru_w_in": nrm(ks[7], (N_LRU_LAYERS, D_MODEL, 2 * R)) * D_MODEL ** -0.5,
        "lru_conv_w": nrm(ks[8], (N_LRU_LAYERS, CONV_WIDTH, R)) * CONV_WIDTH ** -0.5,
        "lru_conv_b": 0.01 * nrm(ks[9], (N_LRU_LAYERS, R)),
        "lru_w_gates": nrm(ks[10], (N_LRU_LAYERS, N_DIRS, 2, H, hd, hd)) * hd ** -0.5,
        "lru_b_gates": 0.01 * nrm(ks[11], (N_LRU_LAYERS, N_DIRS, 2, H, hd)),
        "lru_lambda": lam,
        "lru_w_out": nrm(ks[13], (N_LRU_LAYERS, R, D_MODEL)) * R ** -0.5,
        "router_w": nrm(ks[14], (DEPTH, D_MODEL, N_EXPERTS)) * D_MODEL ** -0.5,
        "moe_w_gate": nrm(ks[15], (DEPTH, N_EXPERTS, D_MODEL, D_EXPERT)) * D_MODEL ** -0.5,
        "moe_w_up": nrm(ks[16], (DEPTH, N_EXPERTS, D_MODEL, D_EXPERT)) * D_MODEL ** -0.5,
        "moe_w_down": nrm(ks[17], (DEPTH, N_EXPERTS, D_EXPERT, D_MODEL)) * D_EXPERT ** -0.5,
    }


def reference(x, meta_tokens, norm_mix, norm_ffn, norm_final, pool_w, pool_scale,
              lru_w_in, lru_conv_w, lru_conv_b, lru_w_gates, lru_b_gates, lru_lambda,
              lru_w_out, router_w, moe_w_gate, moe_w_up, moe_w_down):
    B = x.shape[0]
    meta = jnp.broadcast_to(meta_tokens.astype(x.dtype)[None], (B, N_META, D_MODEL))
    h = jnp.concatenate([meta, x], axis=1)
    for i in range(DEPTH):
        u = _rmsnorm(h, norm_mix[i])
        j = i // N_MIXERS
        if i % N_MIXERS == 0:
            h = h + _pool_mixer(u, pool_w[j], pool_scale[j])
        else:
            h = h + _rglru_mixer(u, lru_w_in[j], lru_conv_w[j], lru_conv_b[j],
                                 lru_w_gates[j], lru_b_gates[j], lru_lambda[j], lru_w_out[j])
        h = h + _expert_choice_ffn(_rmsnorm(h, norm_ffn[i]), router_w[i],
                                   moe_w_gate[i], moe_w_up[i], moe_w_down[i])
    h = _rmsnorm(h, norm_final)
    return h[:, N_META:]
```

```python
import functools

import jax
import jax.numpy as jnp
from jax import lax
from jax.experimental import pallas as pl
from jax.experimental.pallas import tpu as pltpu

F32 = jnp.float32
BF16 = jnp.bfloat16

LANES = 128
SUBLANES = 8
BF16_ROWS = 16
D_MODEL = 1024
N_CHUNKS = D_MODEL // LANES
assert N_CHUNKS == SUBLANES
N_META = 16
POOL_WINDOWS = (2, 4, 8, 16)
POOL_GROUP = D_MODEL // len(POOL_WINDOWS)
CHUNKS_PER_GROUP = POOL_GROUP // LANES
POOL_HALO = 8
LRU_HEADS = 4
LRU_HEAD_DIM = D_MODEL // LRU_HEADS
CHUNKS_PER_HEAD = LRU_HEAD_DIM // LANES
LRU_C = 8.0
CONV_WIDTH = 4
CONV_LEFT = 1
CONV_RIGHT = CONV_WIDTH - 1 - CONV_LEFT
N_EXPERTS = 16
CAPACITY_FACTOR = 2
RMS_EPS = 1e-6
TIME_TILE = 512
FFN_ROW_TILES = 4
FFN_F_TILE = 512
SLOT_CHUNK = 88
MIB = 1024 * 1024


def _cdiv(a, b):
    return -(-a // b)


def _round_up(a, b):
    return _cdiv(a, b) * b


def _chunks(ref2d, tok0, n):
    return [ref2d[pl.ds(tok0 * SUBLANES + j, n, stride=SUBLANES), :]
            for j in range(N_CHUNKS)]


def _store_chunks(ref2d, tok0, n, chunks):
    for j in range(N_CHUNKS):
        ref2d[pl.ds(tok0 * SUBLANES + j, n, stride=SUBLANES), :] = chunks[j]


def _lane_chunk(ref, j):
    return ref[:, j * LANES:(j + 1) * LANES]


def _rms_chunks(chunks, g_ref):
    ss = chunks[0] * chunks[0]
    for c in chunks[1:]:
        ss = ss + c * c
    ms = jnp.sum(ss, axis=-1, keepdims=True) * (1.0 / D_MODEL)
    inv = lax.rsqrt(ms + RMS_EPS)
    return [c * inv * _lane_chunk(g_ref, j) for j, c in enumerate(chunks)]


def _token_ids(tok0, n):
    return tok0 + lax.broadcasted_iota(jnp.int32, (n, 1), 0)


def _params(vmem_mib, *sem):
    return pltpu.CompilerParams(dimension_semantics=sem,
                                vmem_limit_bytes=vmem_mib * MIB)


def _pool_kernel(hp_ref, hm_ref, hn_ref, g_ref, w_ref, sc_ref, o_ref, ext_ref,
                 *, seq, tt):
    t0 = pl.program_id(1) * tt
    hm = hm_ref.at[0]
    out = o_ref.at[0]

    def norm_into_ext(src, n, ext_tok0, tok0):
        tok = _token_ids(tok0, n)
        valid = (tok >= 0) & (tok < seq)
        ch = [jnp.where(valid, c, 0.0) for c in _chunks(src, 0, n)]
        _store_chunks(ext_ref, ext_tok0, n, _rms_chunks(ch, g_ref))

    norm_into_ext(hp_ref.at[0], POOL_HALO, 0, t0 - POOL_HALO)
    norm_into_ext(hm, tt, POOL_HALO, t0)
    norm_into_ext(hn_ref.at[0], POOL_HALO, POOL_HALO + tt, t0 + tt)

    tok = _token_ids(t0, tt)
    for g, win in enumerate(POOL_WINDOWS):
        left = win // 2
        lo = jnp.maximum(tok - left, 0)
        hi = jnp.minimum(tok + win - left, seq)
        cnt = jnp.maximum(hi - lo, 1).astype(F32)
        group_chunks = range(g * CHUNKS_PER_GROUP, (g + 1) * CHUNKS_PER_GROUP)
        ps = []
        for c in group_chunks:
            acc = None
            for k in range(-left, win - left):
                v = ext_ref[pl.ds((POOL_HALO + k) * SUBLANES + c, tt, stride=SUBLANES), :]
                acc = v if acc is None else acc + v
            cur = ext_ref[pl.ds(POOL_HALO * SUBLANES + c, tt, stride=SUBLANES), :]
            ps.append(acc / cnt - cur)
        p = jnp.concatenate(ps, axis=1).astype(BF16)
        y = jnp.dot(p, w_ref[g], preferred_element_type=F32)
        for q, c in enumerate(group_chunks):
            res = hm[pl.ds(c, tt, stride=SUBLANES), :]
            out[pl.ds(c, tt, stride=SUBLANES), :] = (
                res + y[:, q * LANES:(q + 1) * LANES] * _lane_chunk(sc_ref, c))


def _pool_layer(h, g, w, scale, *, seq, tt):
    b = h.shape[0]
    nt = _cdiv(seq, tt)
    halo_rows = POOL_HALO * SUBLANES
    blocks_per_tile = tt // POOL_HALO
    last_halo_block = seq // POOL_HALO - 1
    return pl.pallas_call(
        functools.partial(_pool_kernel, seq=seq, tt=tt),
        grid=(b, nt),
        in_specs=[
            pl.BlockSpec((1, halo_rows, LANES),
                         lambda bi, i: (bi, jnp.maximum(i * blocks_per_tile - 1, 0), 0)),
            pl.BlockSpec((1, tt * SUBLANES, LANES), lambda bi, i: (bi, i, 0)),
            pl.BlockSpec((1, halo_rows, LANES),
                         lambda bi, i: (bi, jnp.minimum((i + 1) * blocks_per_tile,
                                                        last_halo_block), 0)),
            pl.BlockSpec((1, D_MODEL), lambda bi, i: (0, 0)),
            pl.BlockSpec((len(POOL_WINDOWS), POOL_GROUP, POOL_GROUP),
                         lambda bi, i: (0, 0, 0)),
            pl.BlockSpec((1, D_MODEL), lambda bi, i: (0, 0)),
        ],
        out_specs=pl.BlockSpec((1, tt * SUBLANES, LANES), lambda bi, i: (bi, i, 0)),
        out_shape=jax.ShapeDtypeStruct(h.shape, F32),
        scratch_shapes=[pltpu.VMEM(((tt + 2 * POOL_HALO) * SUBLANES, LANES), F32)],
        compiler_params=_params(40, "parallel", "arbitrary"),
        name="pool_mixer",
    )(h, h, h, g.reshape(1, D_MODEL), w.astype(BF16), scale.reshape(1, D_MODEL))


def _gelu_tanh(x):
    c = 0.7978845608028654
    return 0.5 * x * (1.0 + jnp.tanh(c * (x + 0.044715 * (x * x * x))))


def _lru_in_kernel(h_ref, g_ref, w_ref, xb_ref, gy_ref, *, tf):
    u = jnp.concatenate(_rms_chunks(_chunks(h_ref, 0, tf), g_ref), axis=1).astype(BF16)
    xb = jnp.dot(u, w_ref[:, :D_MODEL], preferred_element_type=F32)
    _store_chunks(xb_ref, 0, tf,
                  [xb[:, j * LANES:(j + 1) * LANES] for j in range(N_CHUNKS)])
    yb = jnp.dot(u, w_ref[:, D_MODEL:], preferred_element_type=F32)
    gy_ref[...] = _gelu_tanh(yb)


def _lru_in(h_flat, g, w_in, *, tf):
    n_tok = h_flat.shape[0] // SUBLANES
    return pl.pallas_call(
        functools.partial(_lru_in_kernel, tf=tf),
        grid=(_cdiv(n_tok, tf),),
        in_specs=[
            pl.BlockSpec((tf * SUBLANES, LANES), lambda i: (i, 0)),
            pl.BlockSpec((1, D_MODEL), lambda i: (0, 0)),
            pl.BlockSpec((D_MODEL, 2 * D_MODEL), lambda i: (0, 0)),
        ],
        out_specs=[
            pl.BlockSpec((tf * SUBLANES, LANES), lambda i: (i, 0)),
            pl.BlockSpec((tf, D_MODEL), lambda i: (i, 0)),
        ],
        out_shape=[
            jax.ShapeDtypeStruct(h_flat.shape, F32),
            jax.ShapeDtypeStruct((n_tok, D_MODEL), F32),
        ],
        compiler_params=_params(48, "parallel"),
        name="lru_in_proj",
    )(h_flat, g.reshape(1, D_MODEL), w_in.astype(BF16))


def _softplus(x):
    return jnp.maximum(x, 0.0) + jnp.log1p(jnp.exp(-jnp.abs(x)))


def _lru_scan_kernel(*refs, seq, tt, reverse):
    if reverse:
        (xp_ref, xm_ref, xn_ref, cw_ref, cb_ref, wg_ref, bg_ref, lam_ref,
         o_ref, ext_ref, a_ref, b_ref, carry_ref) = refs
        hs = o_ref.at[0]
    else:
        (xp_ref, xm_ref, xn_ref, cw_ref, cb_ref, wg_ref, bg_ref, lam_ref,
         hb_ref, gy_ref, res_ref, wo_ref,
         o_ref, ext_ref, a_ref, b_ref, carry_ref, hs) = refs
    step = pl.program_id(1)
    nt = pl.num_programs(1)
    ti = nt - 1 - step if reverse else step
    t0 = ti * tt

    @pl.when(step == 0)
    def _():
        carry_ref[...] = jnp.zeros_like(carry_ref)

    def masked_rows(src, n_tok, tok0):
        rows = n_tok * SUBLANES
        tok = tok0 + (lax.broadcasted_iota(jnp.int32, (rows, 1), 0) >> 3)
        return jnp.where((tok >= 0) & (tok < seq), src[...], 0.0)

    ext_ref[pl.ds(0, CONV_LEFT * SUBLANES), :] = masked_rows(
        xp_ref.at[0], CONV_LEFT, t0 - CONV_LEFT)
    ext_ref[pl.ds(CONV_LEFT * SUBLANES, tt * SUBLANES), :] = masked_rows(
        xm_ref.at[0], tt, t0)
    ext_ref[pl.ds((CONV_LEFT + tt) * SUBLANES, CONV_RIGHT * SUBLANES), :] = masked_rows(
        xn_ref.at[0], CONV_RIGHT, t0 + tt)

    valid = _token_ids(t0, tt) < seq
    for hh in range(LRU_HEADS):
        head_chunks = range(hh * CHUNKS_PER_HEAD, (hh + 1) * CHUNKS_PER_HEAD)
        xcs = []
        for c in head_chunks:
            xc = _lane_chunk(cb_ref, c)
            for k in range(CONV_WIDTH):
                xc = xc + (ext_ref[pl.ds(k * SUBLANES + c, tt, stride=SUBLANES), :]
                           * cw_ref[k:k + 1, c * LANES:(c + 1) * LANES])
            xcs.append(xc)
        xh = jnp.concatenate(xcs, axis=1)
        gl = jnp.dot(xh.astype(BF16), wg_ref[hh], preferred_element_type=F32) + bg_ref[hh]
        r = jax.nn.sigmoid(gl[:, :LRU_HEAD_DIM])
        ig = jax.nn.sigmoid(gl[:, LRU_HEAD_DIM:])
        lam = lam_ref[:, hh * LRU_HEAD_DIM:(hh + 1) * LRU_HEAD_DIM]
        log_a = -LRU_C * r * _softplus(-lam)
        a = jnp.exp(log_a)
        bb = jnp.sqrt(jnp.tanh(-log_a) * (a * a + 1.0)) * (ig * xh)
        bb = jnp.where(valid, bb, 0.0)
        for q, c in enumerate(head_chunks):
            a_ref[pl.ds(c, tt, stride=SUBLANES), :] = a[:, q * LANES:(q + 1) * LANES]
            b_ref[pl.ds(c, tt, stride=SUBLANES), :] = bb[:, q * LANES:(q + 1) * LANES]

    def scan_body(i, h):
        for k in range(SUBLANES):
            t = (tt - 1 - (i * SUBLANES + k)) if reverse else (i * SUBLANES + k)
            r0 = pl.multiple_of(t * SUBLANES, SUBLANES)
            h = a_ref[pl.ds(r0, SUBLANES), :] * h + b_ref[pl.ds(r0, SUBLANES), :]
            hs[pl.ds(r0, SUBLANES), :] = h
        return h

    carry_ref[...] = lax.fori_loop(0, tt // SUBLANES, scan_body, carry_ref[...])

    if not reverse:
        hb = hb_ref.at[0]
        hsum = jnp.concatenate(
            [hs[pl.ds(j, tt, stride=SUBLANES), :] + hb[pl.ds(j, tt, stride=SUBLANES), :]
             for j in range(N_CHUNKS)], axis=1)
        z = (hsum * gy_ref[0]).astype(BF16)
        y = jnp.dot(z, wo_ref[...], preferred_element_type=F32)
        res = res_ref.at[0]
        out = o_ref.at[0]
        for j in range(N_CHUNKS):
            out[pl.ds(j, tt, stride=SUBLANES), :] = (
                res[pl.ds(j, tt, stride=SUBLANES), :] + y[:, j * LANES:(j + 1) * LANES])


def _lru_scan(xb, conv_w, conv_b, wg, bg, lam, *, seq, tt, reverse,
              hb=None, gy=None, res=None, w_out=None):
    b = xb.shape[0]
    nt = _cdiv(seq, tt)
    right_rows = CONV_RIGHT * SUBLANES
    last_right_block = seq // CONV_RIGHT - 1

    def tix(i):
        return nt - 1 - i if reverse else i

    tile_spec = pl.BlockSpec((1, tt * SUBLANES, LANES), lambda bi, i: (bi, tix(i), 0))

    def const_spec(shape):
        return pl.BlockSpec(shape, lambda bi, i: (0,) * len(shape))

    in_specs = [
        pl.BlockSpec((1, CONV_LEFT * SUBLANES, LANES),
                     lambda bi, i: (bi, jnp.maximum(tix(i) * (tt // CONV_LEFT) - 1, 0), 0)),
        tile_spec,
        pl.BlockSpec((1, right_rows, LANES),
                     lambda bi, i: (bi, jnp.minimum((tix(i) + 1) * (tt // CONV_RIGHT),
                                                    last_right_block), 0)),
        const_spec((CONV_WIDTH, D_MODEL)),
        const_spec((1, D_MODEL)),
        const_spec((LRU_HEADS, LRU_HEAD_DIM, 2 * LRU_HEAD_DIM)),
        const_spec((LRU_HEADS, 1, 2 * LRU_HEAD_DIM)),
        const_spec((1, D_MODEL)),
    ]
    args = [xb, xb, xb, conv_w, conv_b.reshape(1, D_MODEL), wg, bg, lam.reshape(1, D_MODEL)]
    scratch = [
        pltpu.VMEM(((tt + CONV_WIDTH - 1) * SUBLANES, LANES), F32),
        pltpu.VMEM((tt * SUBLANES, LANES), F32),
        pltpu.VMEM((tt * SUBLANES, LANES), F32),
        pltpu.VMEM((SUBLANES, LANES), F32),
    ]
    if not reverse:
        in_specs += [
            tile_spec,
            pl.BlockSpec((1, tt, D_MODEL), lambda bi, i: (bi, i, 0)),
            tile_spec,
            const_spec((D_MODEL, D_MODEL)),
        ]
        args += [hb, gy, res, w_out]
        scratch.append(pltpu.VMEM((tt * SUBLANES, LANES), F32))
    return pl.pallas_call(
        functools.partial(_lru_scan_kernel, seq=seq, tt=tt, reverse=reverse),
        grid=(b, nt),
        in_specs=in_specs,
        out_specs=tile_spec,
        out_shape=jax.ShapeDtypeStruct(xb.shape, F32),
        scratch_shapes=scratch,
        compiler_params=_params(48, "parallel", "arbitrary"),
        name="lru_scan_rev" if reverse else "lru_scan_fwd_out",
    )(*args)


def _lru_layer(h, g, w_in, conv_w, conv_b, w_gates, b_gates, lam, w_out, *, seq, tt):
    b = h.shape[0]
    xb, gy = _lru_in(h.reshape(b * seq * SUBLANES, LANES), g, w_in, tf=tt)
    xb = xb.reshape(h.shape)
    gy = gy.reshape(b, seq, D_MODEL)
    wg = jnp.concatenate([w_gates[:, 0], w_gates[:, 1]], axis=-1).astype(BF16)
    bg = jnp.concatenate([b_gates[:, 0], b_gates[:, 1]], axis=-1)[:, :, None, :]
    scan = functools.partial(_lru_scan, xb, conv_w, conv_b, seq=seq, tt=tt)
    hb = scan(wg[1], bg[1], lam[1], reverse=True)
    return scan(wg[0], bg[0], lam[0], reverse=False,
                hb=hb, gy=gy, res=h, w_out=w_out.astype(BF16))


def _route_kernel(h_ref, g_ref, rwt_ref, aff_ref, *, tt):
    u = jnp.concatenate(_rms_chunks(_chunks(h_ref.at[0], 0, tt), g_ref), axis=1)
    logits = lax.dot_general(rwt_ref[...], u, (((1,), (1,)), ((), ())),
                             precision=lax.Precision.HIGHEST,
                             preferred_element_type=F32)
    m = jnp.max(logits, axis=0, keepdims=True)
    ex = jnp.exp(logits - m)
    aff_ref[0] = ex / jnp.sum(ex, axis=0, keepdims=True)


def _route(h, g, router_w, *, seq, tt, seq_pad):
    b = h.shape[0]
    return pl.pallas_call(
        functools.partial(_route_kernel, tt=tt),
        grid=(b, _cdiv(seq, tt)),
        in_specs=[
            pl.BlockSpec((1, tt * SUBLANES, LANES), lambda bi, i: (bi, i, 0)),
            pl.BlockSpec((1, D_MODEL), lambda bi, i: (0, 0)),
            pl.BlockSpec((N_EXPERTS, D_MODEL), lambda bi, i: (0, 0)),
        ],
        out_specs=pl.BlockSpec((1, N_EXPERTS, tt), lambda bi, i: (bi, 0, i)),
        out_shape=jax.ShapeDtypeStruct((b, N_EXPERTS, seq_pad), F32),
        compiler_params=_params(40, "parallel", "parallel"),
        name="moe_route",
    )(h, g.reshape(1, D_MODEL), router_w.T)


def _cumsum_lanes(x01):
    r = lax.broadcasted_iota(jnp.int32, (LANES, LANES), 0)
    c = lax.broadcasted_iota(jnp.int32, (LANES, LANES), 1)
    tri = (r <= c).astype(BF16)
    off = jnp.zeros((x01.shape[0], 1), F32)
    outs = []
    for j in range(x01.shape[1] // LANES):
        blk = x01[:, j * LANES:(j + 1) * LANES].astype(BF16)
        cs = jnp.dot(blk, tri, preferred_element_type=F32) + off
        outs.append(cs)
        off = cs[:, LANES - 1:LANES]
    return jnp.concatenate(outs, axis=1)


def _topk_kernel(aff_ref, idx_ref, *, seq, cap, cap_pad, slots_pad):
    aff = aff_ref[0]
    lane = lax.broadcasted_iota(jnp.int32, aff.shape, 1)
    aff = jnp.where(lane < seq, aff, -1.0)
    capf = float(cap)

    def bit_step(i, cur):
        cand = cur | lax.shift_left(jnp.int32(1), 30 - i)
        candf = lax.bitcast_convert_type(cand, F32)
        cnt = jnp.sum((aff >= candf).astype(F32), axis=1, keepdims=True)
        return jnp.where(cnt >= capf, cand, cur)

    cur = lax.fori_loop(0, 31, bit_step, jnp.zeros((N_EXPERTS, 1), jnp.int32))
    thr = lax.bitcast_convert_type(cur, F32)
    gt = aff > thr
    eq = aff == thr
    need = capf - jnp.sum(gt.astype(F32), axis=1, keepdims=True)
    sel = gt | (eq & (_cumsum_lanes(eq.astype(F32)) <= need))
    cnt_incl = _cumsum_lanes(sel.astype(F32))

    lane_e = lax.broadcasted_iota(jnp.int32, (slots_pad, LANES), 1)
    acc = jnp.zeros((slots_pad, LANES), F32)
    for e in range(N_EXPERTS):
        crow = cnt_incl[e:e + 1, :]
        cols = []
        for k in range(cap_pad // SLOT_CHUNK):
            s = (k * SLOT_CHUNK
                 + lax.broadcasted_iota(jnp.int32, (SLOT_CHUNK, 1), 0)).astype(F32)
            cols.append(jnp.sum((crow <= s).astype(F32), axis=1, keepdims=True))
        cols.append(jnp.zeros((slots_pad - cap_pad, 1), F32))
        acc = jnp.where(lane_e == e, jnp.concatenate(cols, axis=0), acc)
    idx = acc.T[:N_EXPERTS, :].astype(jnp.int32)
    slot = lax.broadcasted_iota(jnp.int32, idx.shape, 1)
    idx_ref[0] = jnp.where(slot < cap, idx, 0)


def _topk(aff_t, *, seq, cap, cap_pad, slots_pad):
    b, _, seq_pad = aff_t.shape
    return pl.pallas_call(
        functools.partial(_topk_kernel, seq=seq, cap=cap, cap_pad=cap_pad,
                          slots_pad=slots_pad),
        grid=(b,),
        in_specs=[pl.BlockSpec((1, N_EXPERTS, seq_pad), lambda bi: (bi, 0, 0))],
        out_specs=pl.BlockSpec((1, N_EXPERTS, slots_pad), lambda bi: (bi, 0, 0)),
        out_shape=jax.ShapeDtypeStruct((b, N_EXPERTS, slots_pad), jnp.int32),
        compiler_params=_params(40, "parallel"),
        name="moe_topk",
    )(aff_t)


def _gather_kernel(idx_ref, h_ref, g_ref, xs_ref, zx_ref, *, cap_pad):
    h = h_ref.at[0]

    def body(i, carry):
        for k in range(SUBLANES):
            s = i * SUBLANES + k
            t = idx_ref[0, 0, s]
            zx_ref[pl.ds(pl.multiple_of(s * SUBLANES, SUBLANES), SUBLANES), :] = (
                h[pl.ds(pl.multiple_of(t * SUBLANES, SUBLANES), SUBLANES), :])
        return carry

    lax.fori_loop(0, cap_pad // SUBLANES, body, 0)
    u = _rms_chunks(_chunks(zx_ref, 0, cap_pad), g_ref)
    xs_ref[0, 0] = jnp.concatenate(u, axis=1).astype(BF16)


def _gather(idx_smem, h, g, *, cap_pad):
    b = h.shape[0]
    slots_pad = idx_smem.shape[-1]
    return pl.pallas_call(
        functools.partial(_gather_kernel, cap_pad=cap_pad),
        grid=(b, N_EXPERTS),
        in_specs=[
            pl.BlockSpec((1, 1, slots_pad), lambda bi, e: (bi * N_EXPERTS + e, 0, 0),
                         memory_space=pltpu.SMEM),
            pl.BlockSpec((1,) + h.shape[1:], lambda bi, e: (bi, 0, 0)),
            pl.BlockSpec((1, D_MODEL), lambda bi, e: (0, 0)),
        ],
        out_specs=pl.BlockSpec((1, 1, cap_pad, D_MODEL), lambda bi, e: (e, bi, 0, 0)),
        out_shape=jax.ShapeDtypeStruct((N_EXPERTS, b, cap_pad, D_MODEL), BF16),
        scratch_shapes=[pltpu.VMEM((cap_pad * SUBLANES, LANES), F32)],
        compiler_params=_params(56, "parallel", "arbitrary"),
        name="moe_gather",
    )(idx_smem, h, g.reshape(1, D_MODEL))


def _ffn_kernel(x_ref, wg_ref, wu_ref, wd_ref, o_ref):
    f = pl.program_id(2)
    x = x_ref[0]
    hg = jnp.dot(x, wg_ref[0, 0].astype(BF16), preferred_element_type=F32)
    hu = jnp.dot(x, wu_ref[0, 0].astype(BF16), preferred_element_type=F32)
    hid = (hg * jax.nn.sigmoid(hg) * hu).astype(BF16)
    y = jnp.dot(hid, wd_ref[0, 0].astype(BF16), preferred_element_type=F32)

    @pl.when(f == 0)
    def _():
        o_ref[0] = y

    @pl.when(f > 0)
    def _():
        o_ref[0] += y


def _ffn(xs, w_gate, w_up, w_down, layer):
    n_exp, rows, _ = xs.shape
    d_expert = w_gate.shape[-1]
    rt = rows // FFN_ROW_TILES
    ft = min(FFN_F_TILE, d_expert)
    return pl.pallas_call(
        _ffn_kernel,
        grid=(n_exp, FFN_ROW_TILES, d_expert // ft),
        in_specs=[
            pl.BlockSpec((1, rt, D_MODEL), lambda e, r, f: (e, r, 0)),
            pl.BlockSpec((1, 1, D_MODEL, ft), lambda e, r, f: (layer, e, 0, f)),
            pl.BlockSpec((1, 1, D_MODEL, ft), lambda e, r, f: (layer, e, 0, f)),
            pl.BlockSpec((1, 1, ft, D_MODEL), lambda e, r, f: (layer, e, f, 0)),
        ],
        out_specs=pl.BlockSpec((1, rt, D_MODEL), lambda e, r, f: (e, r, 0)),
        out_shape=jax.ShapeDtypeStruct((n_exp, rows, D_MODEL), F32),
        compiler_params=_params(48, "parallel", "parallel", "arbitrary"),
        name="moe_ffn",
    )(xs, w_gate, w_up, w_down)


def _scatter_kernel(idx_ref, aff_ref, ys_ref, h_ref, o_ref, zy_ref, *, cap, cap_pad):
    @pl.when(pl.program_id(1) == 0)
    def _():
        o_ref[...] = h_ref[...]

    y = ys_ref[0, 0]
    _store_chunks(zy_ref, 0, cap_pad,
                  [y[:, j * LANES:(j + 1) * LANES] for j in range(N_CHUNKS)])
    out = o_ref.at[0]
    pair = 2

    def body(i, carry):
        rows, vals = [], []
        for k in range(pair):
            s = i * pair + k
            t = idx_ref[0, 0, s]
            gate = aff_ref[0, 0, t]
            r0 = pl.multiple_of(t * SUBLANES, SUBLANES)
            s0 = pl.multiple_of(s * SUBLANES, SUBLANES)
            vals.append(out[pl.ds(r0, SUBLANES), :] + gate * zy_ref[pl.ds(s0, SUBLANES), :])
            rows.append(r0)
        for k in range(pair):
            out[pl.ds(rows[k], SUBLANES), :] = vals[k]
        return carry

    lax.fori_loop(0, cap // pair, body, 0)


def _scatter(idx_smem, aff_smem, ys, h, *, cap, cap_pad):
    b = h.shape[0]
    slots_pad = idx_smem.shape[-1]
    seq_pad = aff_smem.shape[-1]
    assert cap % 2 == 0
    resident = pl.BlockSpec((1,) + h.shape[1:], lambda bi, e: (bi, 0, 0),
                            pipeline_mode=pl.Buffered(1))
    return pl.pallas_call(
        functools.partial(_scatter_kernel, cap=cap, cap_pad=cap_pad),
        grid=(b, N_EXPERTS),
        in_specs=[
            pl.BlockSpec((1, 1, slots_pad), lambda bi, e: (bi * N_EXPERTS + e, 0, 0),
                         memory_space=pltpu.SMEM),
            pl.BlockSpec((1, 1, seq_pad), lambda bi, e: (bi * N_EXPERTS + e, 0, 0),
                         memory_space=pltpu.SMEM),
            pl.BlockSpec((1, 1, cap_pad, D_MODEL), lambda bi, e: (e, bi, 0, 0)),
            resident,
        ],
        out_specs=resident,
        out_shape=jax.ShapeDtypeStruct(h.shape, F32),
        scratch_shapes=[pltpu.VMEM((cap_pad * SUBLANES, LANES), F32)],
        compiler_params=_params(56, "parallel", "arbitrary"),
        name="moe_scatter",
    )(idx_smem, aff_smem, ys, h)


def _moe_layer(h, g, router_w, w_gate, w_up, w_down, layer, *, seq, tt):
    b = h.shape[0]
    cap = CAPACITY_FACTOR * seq // N_EXPERTS
    cap_pad = _round_up(cap, SLOT_CHUNK)
    assert cap_pad % BF16_ROWS == 0 and (b * cap_pad) % (FFN_ROW_TILES * BF16_ROWS) == 0
    slots_pad = _round_up(cap_pad, LANES)
    seq_pad = _round_up(seq, LANES)
    aff_t = _route(h, g, router_w, seq=seq, tt=tt, seq_pad=seq_pad)
    idx = _topk(aff_t, seq=seq, cap=cap, cap_pad=cap_pad, slots_pad=slots_pad)
    idx_smem = idx.reshape(b * N_EXPERTS, 1, slots_pad)
    aff_smem = aff_t.reshape(b * N_EXPERTS, 1, seq_pad)
    xs = _gather(idx_smem, h, g, cap_pad=cap_pad)
    ys = _ffn(xs.reshape(N_EXPERTS, b * cap_pad, D_MODEL), w_gate, w_up, w_down, layer)
    ys = ys.reshape(N_EXPERTS, b, cap_pad, D_MODEL)
    return _scatter(idx_smem, aff_smem, ys, h, cap=cap, cap_pad=cap_pad)


def _final_kernel(ha_ref, hb_ref, g_ref, o_ref, *, tt):
    ha = ha_ref.at[0]
    hb = hb_ref.at[0]
    ch = [jnp.concatenate(
        [ha[pl.ds(N_META * SUBLANES + j, tt - N_META, stride=SUBLANES), :],
         hb[pl.ds(j, N_META, stride=SUBLANES), :]], axis=0) for j in range(N_CHUNKS)]
    o_ref[0] = jnp.concatenate(_rms_chunks(ch, g_ref), axis=1)


def _final(h, g, *, seq, tt):
    b = h.shape[0]
    s_out = seq - N_META
    assert s_out % tt == 0
    tile = (1, tt * SUBLANES, LANES)
    return pl.pallas_call(
        functools.partial(_final_kernel, tt=tt),
        grid=(b, s_out // tt),
        in_specs=[
            pl.BlockSpec(tile, lambda bi, i: (bi, i, 0)),
            pl.BlockSpec(tile, lambda bi, i: (bi, i + 1, 0)),
            pl.BlockSpec((1, D_MODEL), lambda bi, i: (0, 0)),
        ],
        out_specs=pl.BlockSpec((1, tt, D_MODEL), lambda bi, i: (bi, i, 0)),
        out_shape=jax.ShapeDtypeStruct((b, s_out, D_MODEL), F32),
        compiler_params=_params(40, "parallel", "parallel"),
        name="final_norm",
    )(h, h, g.reshape(1, D_MODEL))


def kernel(x, meta_tokens, norm_mix, norm_ffn, norm_final, pool_w, pool_scale, lru_w_in, lru_conv_w, lru_conv_b, lru_w_gates, lru_b_gates, lru_lambda, lru_w_out, router_w, moe_w_gate, moe_w_up, moe_w_down):
    b, s, d = x.shape
    assert d == D_MODEL
    seq = s + N_META
    tt = TIME_TILE
    depth = norm_mix.shape[0]
    meta = jnp.broadcast_to(meta_tokens.astype(x.dtype)[None], (b, N_META, d))
    h = jnp.concatenate([meta, x], axis=1).reshape(b, seq * SUBLANES, LANES)
    for i in range(depth):
        j = i // 2
        if i % 2 == 0:
            h = _pool_layer(h, norm_mix[i], pool_w[j], pool_scale[j], seq=seq, tt=tt)
        else:
            h = _lru_layer(h, norm_mix[i], lru_w_in[j], lru_conv_w[j], lru_conv_b[j],
                           lru_w_gates[j], lru_b_gates[j], lru_lambda[j], lru_w_out[j],
                           seq=seq, tt=tt)
        h = _moe_layer(h, norm_ffn[i], router_w[i], moe_w_gate, moe_w_up, moe_w_down, i,
                       seq=seq, tt=tt)
    return _final(h, norm_final, seq=seq, tt=tt)
```

```python
import functools

import jax
import jax.numpy as jnp
from jax import lax
from jax.experimental import pallas as pl
from jax.experimental.pallas import tpu as pltpu

F32 = jnp.float32
BF16 = jnp.bfloat16

LANES = 128
SUBLANES = 8
BF16_ROWS = 16
D_MODEL = 1024
N_CHUNKS = D_MODEL // LANES
assert N_CHUNKS == SUBLANES
N_META = 16
POOL_WINDOWS = (2, 4, 8, 16)
POOL_GROUP = D_MODEL // len(POOL_WINDOWS)
CHUNKS_PER_GROUP = POOL_GROUP // LANES
POOL_HALO = 8
LRU_HEADS = 4
LRU_HEAD_DIM = D_MODEL // LRU_HEADS
CHUNKS_PER_HEAD = LRU_HEAD_DIM // LANES
LRU_C = 8.0
CONV_WIDTH = 4
CONV_LEFT = 1
CONV_RIGHT = CONV_WIDTH - 1 - CONV_LEFT
N_EXPERTS = 16
CAPACITY_FACTOR = 2
RMS_EPS = 1e-6
TIME_TILE = 512
FFN_ROW_TILES = 4
FFN_F_TILE = 512
SCATTER_GROUP = 8
COUNT_RADIX = 32.0
MIB = 1024 * 1024


def _cdiv(a, b):
    return -(-a // b)


def _round_up(a, b):
    return _cdiv(a, b) * b


def _chunks(ref2d, tok0, n):
    return [ref2d[pl.ds(tok0 * SUBLANES + j, n, stride=SUBLANES), :]
            for j in range(N_CHUNKS)]


def _store_chunks(ref2d, tok0, n, chunks):
    for j in range(N_CHUNKS):
        ref2d[pl.ds(tok0 * SUBLANES + j, n, stride=SUBLANES), :] = chunks[j]


def _lane_chunk(ref, j):
    return ref[:, j * LANES:(j + 1) * LANES]


def _rms_chunks(chunks, g_ref):
    ss = chunks[0] * chunks[0]
    for c in chunks[1:]:
        ss = ss + c * c
    ms = jnp.sum(ss, axis=-1, keepdims=True) * (1.0 / D_MODEL)
    inv = lax.rsqrt(ms + RMS_EPS)
    return [c * inv * _lane_chunk(g_ref, j) for j, c in enumerate(chunks)]


def _token_ids(tok0, n):
    return tok0 + lax.broadcasted_iota(jnp.int32, (n, 1), 0)


def _params(vmem_mib, *sem):
    return pltpu.CompilerParams(dimension_semantics=sem,
                                vmem_limit_bytes=vmem_mib * MIB)


def _embed_kernel(meta_ref, xa_ref, xb_ref, o_ref, *, tt):
    i = pl.program_id(1)
    out = o_ref.at[0]
    body = xb_ref[0, :tt - N_META, :]
    _store_chunks(out, N_META, tt - N_META,
                  [body[:, j * LANES:(j + 1) * LANES] for j in range(N_CHUNKS)])

    def head(src):
        _store_chunks(out, 0, N_META,
                      [src[:, j * LANES:(j + 1) * LANES] for j in range(N_CHUNKS)])

    @pl.when(i == 0)
    def _():
        head(meta_ref[...])

    @pl.when(i > 0)
    def _():
        head(xa_ref[0, tt - N_META:, :])


def _embed(x, meta, *, tt):
    b, s, _ = x.shape
    seq = s + N_META
    assert s % tt == 0 and tt % SUBLANES == 0
    last = s // tt - 1
    return pl.pallas_call(
        functools.partial(_embed_kernel, tt=tt),
        grid=(b, _cdiv(seq, tt)),
        in_specs=[
            pl.BlockSpec((N_META, D_MODEL), lambda bi, i: (0, 0)),
            pl.BlockSpec((1, tt, D_MODEL), lambda bi, i: (bi, jnp.maximum(i - 1, 0), 0)),
            pl.BlockSpec((1, tt, D_MODEL), lambda bi, i: (bi, jnp.minimum(i, last), 0)),
        ],
        out_specs=pl.BlockSpec((1, tt * SUBLANES, LANES), lambda bi, i: (bi, i, 0)),
        out_shape=jax.ShapeDtypeStruct((b, seq * SUBLANES, LANES), F32),
        compiler_params=_params(40, "parallel", "parallel"),
        name="embed",
    )(meta, x, x)


def _pool_kernel(hp_ref, hm_ref, hn_ref, g_ref, w_ref, sc_ref, o_ref, ext_ref,
                 *, seq, tt):
    t0 = pl.program_id(1) * tt
    hm = hm_ref.at[0]
    out = o_ref.at[0]

    def norm_into_ext(src, n, ext_tok0, tok0):
        tok = _token_ids(tok0, n)
        valid = (tok >= 0) & (tok < seq)
        ch = [jnp.where(valid, c, 0.0) for c in _chunks(src, 0, n)]
        _store_chunks(ext_ref, ext_tok0, n, _rms_chunks(ch, g_ref))

    norm_into_ext(hp_ref.at[0], POOL_HALO, 0, t0 - POOL_HALO)
    norm_into_ext(hm, tt, POOL_HALO, t0)
    norm_into_ext(hn_ref.at[0], POOL_HALO, POOL_HALO + tt, t0 + tt)

    tok = _token_ids(t0, tt)
    for g, win in enumerate(POOL_WINDOWS):
        left = win // 2
        lo = jnp.maximum(tok - left, 0)
        hi = jnp.minimum(tok + win - left, seq)
        cnt = jnp.maximum(hi - lo, 1).astype(F32)
        group_chunks = range(g * CHUNKS_PER_GROUP, (g + 1) * CHUNKS_PER_GROUP)
        ps = []
        for c in group_chunks:
            acc = None
            for k in range(-left, win - left):
                v = ext_ref[pl.ds((POOL_HALO + k) * SUBLANES + c, tt, stride=SUBLANES), :]
                acc = v if acc is None else acc + v
            cur = ext_ref[pl.ds(POOL_HALO * SUBLANES + c, tt, stride=SUBLANES), :]
            ps.append(acc / cnt - cur)
        p = jnp.concatenate(ps, axis=1).astype(BF16)
        y = jnp.dot(p, w_ref[g], preferred_element_type=F32)
        for q, c in enumerate(group_chunks):
            res = hm[pl.ds(c, tt, stride=SUBLANES), :]
            out[pl.ds(c, tt, stride=SUBLANES), :] = (
                res + y[:, q * LANES:(q + 1) * LANES] * _lane_chunk(sc_ref, c))


def _pool_layer(h, g, w, scale, *, seq, tt):
    b = h.shape[0]
    nt = _cdiv(seq, tt)
    halo_rows = POOL_HALO * SUBLANES
    blocks_per_tile = tt // POOL_HALO
    last_halo_block = seq // POOL_HALO - 1
    return pl.pallas_call(
        functools.partial(_pool_kernel, seq=seq, tt=tt),
        grid=(b, nt),
        in_specs=[
            pl.BlockSpec((1, halo_rows, LANES),
                         lambda bi, i: (bi, jnp.maximum(i * blocks_per_tile - 1, 0), 0)),
            pl.BlockSpec((1, tt * SUBLANES, LANES), lambda bi, i: (bi, i, 0)),
            pl.BlockSpec((1, halo_rows, LANES),
                         lambda bi, i: (bi, jnp.minimum((i + 1) * blocks_per_tile,
                                                        last_halo_block), 0)),
            pl.BlockSpec((1, D_MODEL), lambda bi, i: (0, 0)),
            pl.BlockSpec((len(POOL_WINDOWS), POOL_GROUP, POOL_GROUP),
                         lambda bi, i: (0, 0, 0)),
            pl.BlockSpec((1, D_MODEL), lambda bi, i: (0, 0)),
        ],
        out_specs=pl.BlockSpec((1, tt * SUBLANES, LANES), lambda bi, i: (bi, i, 0)),
        out_shape=jax.ShapeDtypeStruct(h.shape, F32),
        scratch_shapes=[pltpu.VMEM(((tt + 2 * POOL_HALO) * SUBLANES, LANES), F32)],
        compiler_params=_params(40, "parallel", "arbitrary"),
        name="pool_mixer",
    )(h, h, h, g.reshape(1, D_MODEL), w.astype(BF16), scale.reshape(1, D_MODEL))


def _gelu_tanh(x):
    c = 0.7978845608028654
    return 0.5 * x * (1.0 + jnp.tanh(c * (x + 0.044715 * (x * x * x))))


def _lru_in_kernel(h_ref, g_ref, w_ref, xb_ref, gy_ref, *, tf):
    u = jnp.concatenate(_rms_chunks(_chunks(h_ref, 0, tf), g_ref), axis=1).astype(BF16)
    xb = jnp.dot(u, w_ref[:, :D_MODEL], preferred_element_type=F32)
    _store_chunks(xb_ref, 0, tf,
                  [xb[:, j * LANES:(j + 1) * LANES] for j in range(N_CHUNKS)])
    yb = jnp.dot(u, w_ref[:, D_MODEL:], preferred_element_type=F32)
    gy_ref[...] = _gelu_tanh(yb)


def _lru_in(h_flat, g, w_in, *, tf):
    n_tok = h_flat.shape[0] // SUBLANES
    return pl.pallas_call(
        functools.partial(_lru_in_kernel, tf=tf),
        grid=(_cdiv(n_tok, tf),),
        in_specs=[
            pl.BlockSpec((tf * SUBLANES, LANES), lambda i: (i, 0)),
            pl.BlockSpec((1, D_MODEL), lambda i: (0, 0)),
            pl.BlockSpec((D_MODEL, 2 * D_MODEL), lambda i: (0, 0)),
        ],
        out_specs=[
            pl.BlockSpec((tf * SUBLANES, LANES), lambda i: (i, 0)),
            pl.BlockSpec((tf, D_MODEL), lambda i: (i, 0)),
        ],
        out_shape=[
            jax.ShapeDtypeStruct(h_flat.shape, F32),
            jax.ShapeDtypeStruct((n_tok, D_MODEL), F32),
        ],
        compiler_params=_params(48, "parallel"),
        name="lru_in_proj",
    )(h_flat, g.reshape(1, D_MODEL), w_in.astype(BF16))


def _softplus(x):
    return jnp.maximum(x, 0.0) + jnp.log1p(jnp.exp(-jnp.abs(x)))


def _sigmoid(x):
    return 0.5 * jnp.tanh(0.5 * x) + 0.5


def _lru_scan_kernel(*refs, seq, tt, reverse):
    if reverse:
        (xp_ref, xm_ref, xn_ref, cw_ref, cb_ref, wg_ref, bg_ref, lam_ref,
         o_ref, ext_ref, a_ref, b_ref, carry_ref) = refs
        hs = o_ref.at[0]
    else:
        (xp_ref, xm_ref, xn_ref, cw_ref, cb_ref, wg_ref, bg_ref, lam_ref,
         hb_ref, gy_ref, res_ref, wo_ref,
         o_ref, ext_ref, a_ref, b_ref, carry_ref, hs) = refs
    step = pl.program_id(1)
    nt = pl.num_programs(1)
    ti = nt - 1 - step if reverse else step
    t0 = ti * tt

    @pl.when(step == 0)
    def _():
        carry_ref[...] = jnp.zeros_like(carry_ref)

    def masked_rows(src, n_tok, tok0):
        rows = n_tok * SUBLANES
        tok = tok0 + (lax.broadcasted_iota(jnp.int32, (rows, 1), 0) >> 3)
        return jnp.where((tok >= 0) & (tok < seq), src[...], 0.0)

    main_rows = pl.ds(CONV_LEFT * SUBLANES, tt * SUBLANES)
    ext_ref[pl.ds(0, CONV_LEFT * SUBLANES), :] = masked_rows(
        xp_ref.at[0], CONV_LEFT, t0 - CONV_LEFT)
    ext_ref[main_rows, :] = xm_ref[0]
    ext_ref[pl.ds((CONV_LEFT + tt) * SUBLANES, CONV_RIGHT * SUBLANES), :] = masked_rows(
        xn_ref.at[0], CONV_RIGHT, t0 + tt)
    ragged = t0 + tt > seq

    @pl.when(ragged)
    def _():
        ext_ref[main_rows, :] = masked_rows(xm_ref.at[0], tt, t0)

    for hh in range(LRU_HEADS):
        head_chunks = range(hh * CHUNKS_PER_HEAD, (hh + 1) * CHUNKS_PER_HEAD)
        xcs = []
        for c in head_chunks:
            xc = _lane_chunk(cb_ref, c)
            for k in range(CONV_WIDTH):
                xc = xc + (ext_ref[pl.ds(k * SUBLANES + c, tt, stride=SUBLANES), :]
                           * cw_ref[k:k + 1, c * LANES:(c + 1) * LANES])
            xcs.append(xc)
        xh = jnp.concatenate(xcs, axis=1)
        gl = jnp.dot(xh.astype(BF16), wg_ref[hh], preferred_element_type=F32) + bg_ref[hh]
        r = _sigmoid(gl[:, :LRU_HEAD_DIM])
        ig = _sigmoid(gl[:, LRU_HEAD_DIM:])
        lam = lam_ref[:, hh * LRU_HEAD_DIM:(hh + 1) * LRU_HEAD_DIM]
        log_a = r * (-LRU_C * _softplus(-lam))
        a = jnp.exp(log_a)
        bb = jnp.sqrt(jnp.tanh(-log_a) * (a * a + 1.0)) * (ig * xh)
        for q, c in enumerate(head_chunks):
            a_ref[pl.ds(c, tt, stride=SUBLANES), :] = a[:, q * LANES:(q + 1) * LANES]
            b_ref[pl.ds(c, tt, stride=SUBLANES), :] = bb[:, q * LANES:(q + 1) * LANES]

    @pl.when(ragged)
    def _():
        tok = t0 + (lax.broadcasted_iota(jnp.int32, (tt * SUBLANES, 1), 0) >> 3)
        b_ref[...] = jnp.where(tok < seq, b_ref[...], 0.0)

    def scan_body(i, h):
        for k in range(SUBLANES):
            t = (tt - 1 - (i * SUBLANES + k)) if reverse else (i * SUBLANES + k)
            r0 = pl.multiple_of(t * SUBLANES, SUBLANES)
            h = a_ref[pl.ds(r0, SUBLANES), :] * h + b_ref[pl.ds(r0, SUBLANES), :]
            hs[pl.ds(r0, SUBLANES), :] = h
        return h

    carry_ref[...] = lax.fori_loop(0, tt // SUBLANES, scan_body, carry_ref[...])

    if not reverse:
        hb = hb_ref.at[0]
        hsum = jnp.concatenate(
            [hs[pl.ds(j, tt, stride=SUBLANES), :] + hb[pl.ds(j, tt, stride=SUBLANES), :]
             for j in range(N_CHUNKS)], axis=1)
        z = (hsum * gy_ref[0]).astype(BF16)
        y = jnp.dot(z, wo_ref[...], preferred_element_type=F32)
        res = res_ref.at[0]
        out = o_ref.at[0]
        for j in range(N_CHUNKS):
            out[pl.ds(j, tt, stride=SUBLANES), :] = (
                res[pl.ds(j, tt, stride=SUBLANES), :] + y[:, j * LANES:(j + 1) * LANES])


def _lru_scan(xb, conv_w, conv_b, wg, bg, lam, *, seq, tt, reverse,
              hb=None, gy=None, res=None, w_out=None):
    b = xb.shape[0]
    nt = _cdiv(seq, tt)
    right_rows = CONV_RIGHT * SUBLANES
    last_right_block = seq // CONV_RIGHT - 1

    def tix(i):
        return nt - 1 - i if reverse else i

    tile_spec = pl.BlockSpec((1, tt * SUBLANES, LANES), lambda bi, i: (bi, tix(i), 0))

    def const_spec(shape):
        return pl.BlockSpec(shape, lambda bi, i: (0,) * len(shape))

    in_specs = [
        pl.BlockSpec((1, CONV_LEFT * SUBLANES, LANES),
                     lambda bi, i: (bi, jnp.maximum(tix(i) * (tt // CONV_LEFT) - 1, 0), 0)),
        tile_spec,
        pl.BlockSpec((1, right_rows, LANES),
                     lambda bi, i: (bi, jnp.minimum((tix(i) + 1) * (tt // CONV_RIGHT),
                                                    last_right_block), 0)),
        const_spec((CONV_WIDTH, D_MODEL)),
        const_spec((1, D_MODEL)),
        const_spec((LRU_HEADS, LRU_HEAD_DIM, 2 * LRU_HEAD_DIM)),
        const_spec((LRU_HEADS, 1, 2 * LRU_HEAD_DIM)),
        const_spec((1, D_MODEL)),
    ]
    args = [xb, xb, xb, conv_w, conv_b.reshape(1, D_MODEL), wg, bg, lam.reshape(1, D_MODEL)]
    scratch = [
        pltpu.VMEM(((tt + CONV_WIDTH - 1) * SUBLANES, LANES), F32),
        pltpu.VMEM((tt * SUBLANES, LANES), F32),
        pltpu.VMEM((tt * SUBLANES, LANES), F32),
        pltpu.VMEM((SUBLANES, LANES), F32),
    ]
    if not reverse:
        in_specs += [
            tile_spec,
            pl.BlockSpec((1, tt, D_MODEL), lambda bi, i: (bi, i, 0)),
            tile_spec,
            const_spec((D_MODEL, D_MODEL)),
        ]
        args += [hb, gy, res, w_out]
        scratch.append(pltpu.VMEM((tt * SUBLANES, LANES), F32))
    return pl.pallas_call(
        functools.partial(_lru_scan_kernel, seq=seq, tt=tt, reverse=reverse),
        grid=(b, nt),
        in_specs=in_specs,
        out_specs=tile_spec,
        out_shape=jax.ShapeDtypeStruct(xb.shape, F32),
        scratch_shapes=scratch,
        compiler_params=_params(48, "parallel", "arbitrary"),
        name="lru_scan_rev" if reverse else "lru_scan_fwd_out",
    )(*args)


def _lru_layer(h, g, w_in, conv_w, conv_b, w_gates, b_gates, lam, w_out, *, seq, tt):
    b = h.shape[0]
    xb, gy = _lru_in(h.reshape(b * seq * SUBLANES, LANES), g, w_in, tf=tt)
    xb = xb.reshape(h.shape)
    gy = gy.reshape(b, seq, D_MODEL)
    wg = jnp.concatenate([w_gates[:, 0], w_gates[:, 1]], axis=-1).astype(BF16)
    bg = jnp.concatenate([b_gates[:, 0], b_gates[:, 1]], axis=-1)[:, :, None, :]
    scan = functools.partial(_lru_scan, xb, conv_w, conv_b, seq=seq, tt=tt)
    hb = scan(wg[1], bg[1], lam[1], reverse=True)
    return scan(wg[0], bg[0], lam[0], reverse=False,
                hb=hb, gy=gy, res=h, w_out=w_out.astype(BF16))


def _split_bf16(x):
    hi = x.astype(BF16)
    return hi, (x - hi.astype(F32)).astype(BF16)


def _route_kernel(h_ref, g_ref, rw_ref, aff_ref, *, tt):
    u = jnp.concatenate(_rms_chunks(_chunks(h_ref.at[0], 0, tt), g_ref), axis=1)
    u_hi, u_lo = _split_bf16(u)
    p = jnp.dot(u_hi, rw_ref[...], preferred_element_type=F32)
    q = jnp.dot(u_lo, rw_ref[:, :LANES], preferred_element_type=F32)
    logits = ((p[:, :LANES] + p[:, LANES:]) + q).T[:N_EXPERTS, :]
    m = jnp.max(logits, axis=0, keepdims=True)
    ex = jnp.exp(logits - m)
    aff_ref[0] = ex / jnp.sum(ex, axis=0, keepdims=True)


def _route(h, g, router_w, *, seq, tt, seq_pad):
    b = h.shape[0]
    w_hi, w_lo = _split_bf16(jnp.pad(router_w, ((0, 0), (0, LANES - N_EXPERTS))))
    return pl.pallas_call(
        functools.partial(_route_kernel, tt=tt),
        grid=(b, _cdiv(seq, tt)),
        in_specs=[
            pl.BlockSpec((1, tt * SUBLANES, LANES), lambda bi, i: (bi, i, 0)),
            pl.BlockSpec((1, D_MODEL), lambda bi, i: (0, 0)),
            pl.BlockSpec((D_MODEL, 2 * LANES), lambda bi, i: (0, 0)),
        ],
        out_specs=pl.BlockSpec((1, N_EXPERTS, tt), lambda bi, i: (bi, 0, i)),
        out_shape=jax.ShapeDtypeStruct((b, N_EXPERTS, seq_pad), F32),
        compiler_params=_params(40, "parallel", "parallel"),
        name="moe_route",
    )(h, g.reshape(1, D_MODEL), jnp.concatenate([w_hi, w_lo], axis=1))


def _cumsum_blocks(x01):
    r = lax.broadcasted_iota(jnp.int32, (LANES, LANES), 0)
    c = lax.broadcasted_iota(jnp.int32, (LANES, LANES), 1)
    tri = (r <= c).astype(BF16)
    local = [jnp.dot(x01[:, j * LANES:(j + 1) * LANES].astype(BF16), tri,
                     preferred_element_type=F32) for j in range(x01.shape[1] // LANES)]
    off = jnp.zeros((x01.shape[0], 1), F32)
    outs = []
    for cs in local:
        outs.append(cs + off)
        off = off + cs[:, LANES - 1:LANES]
    return outs


def _topk_kernel(aff_ref, idx_ref, cs_ref, *, seq, cap, cap_pad, slots_pad):
    aff = aff_ref[0]
    lane = lax.broadcasted_iota(jnp.int32, aff.shape, 1)
    aff = jnp.where(lane < seq, aff, -1.0)
    capf = float(cap)
    n_blk = aff.shape[1] // LANES

    def bit_step(i, cur):
        cand = cur | lax.shift_left(jnp.int32(1), 30 - i)
        candf = lax.bitcast_convert_type(cand, F32)
        cnt = jnp.sum((aff >= candf).astype(F32), axis=1, keepdims=True)
        return jnp.where(cnt >= capf, cand, cur)

    cur = lax.fori_loop(0, 31, bit_step, jnp.zeros((N_EXPERTS, 1), jnp.int32))
    thr = lax.bitcast_convert_type(cur, F32)
    gt = aff > thr
    eq = aff == thr
    need = capf - jnp.sum(gt.astype(F32), axis=1, keepdims=True)
    eq_cnt = jnp.concatenate(_cumsum_blocks(eq.astype(F32)), axis=1)
    sel = gt | (eq & (eq_cnt <= need))

    blocks = _cumsum_blocks(sel.astype(F32))
    for j, cs in enumerate(blocks):
        cs_ref[pl.ds(j * N_EXPERTS, N_EXPERTS), :] = cs
    cs_ref[pl.ds(n_blk * N_EXPERTS, (LANES - n_blk) * N_EXPERTS), :] = jnp.zeros(
        ((LANES - n_blk) * N_EXPERTS, LANES), F32)
    never = jnp.full((N_EXPERTS, LANES - n_blk), 2.0 * COUNT_RADIX * 256, F32)
    blk_end = jnp.concatenate([cs[:, LANES - 1:LANES] for cs in blocks] + [never], axis=1)

    slot = lax.broadcasted_iota(jnp.int32, (cap_pad, 1), 0).astype(F32)
    lane_s = lax.broadcasted_iota(jnp.int32, (cap_pad, LANES), 1).astype(F32)
    lane_e = lax.broadcasted_iota(jnp.int32, (slots_pad, LANES), 1)
    experts = range(N_EXPERTS)
    digits = []
    for e in experts:
        cmat = cs_ref[pl.ds(e, LANES, stride=N_EXPERTS), :]
        hi = jnp.floor(cmat * (1.0 / COUNT_RADIX))
        digits.append(jnp.concatenate([hi, cmat - COUNT_RADIX * hi], axis=1).astype(BF16))
    ones = jnp.ones((LANES, LANES), BF16)

    def count_le(x):
        return jnp.dot((x <= slot).astype(BF16), ones, preferred_element_type=F32)

    full = [count_le(blk_end[e:e + 1, :]) for e in experts]
    pick = [(lane_s == full[e]).astype(BF16) for e in experts]
    rows = [jnp.dot(pick[e], digits[e], preferred_element_type=F32) for e in experts]
    rows = [COUNT_RADIX * r_[:, :LANES] + r_[:, LANES:] for r_ in rows]
    inside = [count_le(rows[e]) for e in experts]
    acc = jnp.zeros((slots_pad, LANES), F32)
    pad_rows = jnp.zeros((slots_pad - cap_pad, LANES), F32)
    for e in experts:
        col = jnp.concatenate([LANES * full[e] + inside[e], pad_rows], axis=0)
        acc = jnp.where(lane_e == e, col, acc)
    idx = acc.T[:N_EXPERTS, :].astype(jnp.int32)
    slot_l = lax.broadcasted_iota(jnp.int32, idx.shape, 1)
    idx_ref[0] = jnp.where(slot_l < cap, idx, 0)


def _topk(aff_t, *, seq, cap, cap_pad, slots_pad):
    b, _, seq_pad = aff_t.shape
    assert seq_pad // LANES <= LANES and cap < COUNT_RADIX * 256
    return pl.pallas_call(
        functools.partial(_topk_kernel, seq=seq, cap=cap, cap_pad=cap_pad,
                          slots_pad=slots_pad),
        grid=(b,),
        in_specs=[pl.BlockSpec((1, N_EXPERTS, seq_pad), lambda bi: (bi, 0, 0))],
        out_specs=pl.BlockSpec((1, N_EXPERTS, slots_pad), lambda bi: (bi, 0, 0)),
        out_shape=jax.ShapeDtypeStruct((b, N_EXPERTS, slots_pad), jnp.int32),
        scratch_shapes=[pltpu.VMEM((LANES * N_EXPERTS, LANES), F32)],
        compiler_params=_params(40, "parallel"),
        name="moe_topk",
    )(aff_t)


def _gather_kernel(idx_ref, h_ref, g_ref, xs_ref, zx_ref, *, cap_pad):
    h = h_ref.at[0]

    def body(i, carry):
        for k in range(SUBLANES):
            s = i * SUBLANES + k
            t = idx_ref[0, 0, s]
            zx_ref[pl.ds(pl.multiple_of(s * SUBLANES, SUBLANES), SUBLANES), :] = (
                h[pl.ds(pl.multiple_of(t * SUBLANES, SUBLANES), SUBLANES), :])
        return carry

    lax.fori_loop(0, cap_pad // SUBLANES, body, 0)
    u = _rms_chunks(_chunks(zx_ref, 0, cap_pad), g_ref)
    xs_ref[0, 0] = jnp.concatenate(u, axis=1).astype(BF16)


def _gather(idx_smem, h, g, *, cap_pad):
    b = h.shape[0]
    slots_pad = idx_smem.shape[-1]
    return pl.pallas_call(
        functools.partial(_gather_kernel, cap_pad=cap_pad),
        grid=(b, N_EXPERTS),
        in_specs=[
            pl.BlockSpec((1, 1, slots_pad), lambda bi, e: (bi * N_EXPERTS + e, 0, 0),
                         memory_space=pltpu.SMEM),
            pl.BlockSpec((1,) + h.shape[1:], lambda bi, e: (bi, 0, 0)),
            pl.BlockSpec((1, D_MODEL), lambda bi, e: (0, 0)),
        ],
        out_specs=pl.BlockSpec((1, 1, cap_pad, D_MODEL), lambda bi, e: (e, bi, 0, 0)),
        out_shape=jax.ShapeDtypeStruct((N_EXPERTS, b, cap_pad, D_MODEL), BF16),
        scratch_shapes=[pltpu.VMEM((cap_pad * SUBLANES, LANES), F32)],
        compiler_params=_params(56, "parallel", "arbitrary"),
        name="moe_gather",
    )(idx_smem, h, g.reshape(1, D_MODEL))


def _ffn_kernel(x_ref, wg_ref, wu_ref, wd_ref, o_ref):
    f = pl.program_id(2)
    x = x_ref[0]
    hg = jnp.dot(x, wg_ref[0, 0].astype(BF16), preferred_element_type=F32)
    hu = jnp.dot(x, wu_ref[0, 0].astype(BF16), preferred_element_type=F32)
    hid = (hg * jax.nn.sigmoid(hg) * hu).astype(BF16)
    y = jnp.dot(hid, wd_ref[0, 0].astype(BF16), preferred_element_type=F32)

    @pl.when(f == 0)
    def _():
        o_ref[0] = y

    @pl.when(f > 0)
    def _():
        o_ref[0] += y


def _ffn(xs, w_gate, w_up, w_down, layer):
    n_exp, rows, _ = xs.shape
    d_expert = w_gate.shape[-1]
    rt = rows // FFN_ROW_TILES
    ft = min(FFN_F_TILE, d_expert)
    return pl.pallas_call(
        _ffn_kernel,
        grid=(n_exp, FFN_ROW_TILES, d_expert // ft),
        in_specs=[
            pl.BlockSpec((1, rt, D_MODEL), lambda e, r, f: (e, r, 0)),
            pl.BlockSpec((1, 1, D_MODEL, ft), lambda e, r, f: (layer, e, 0, f)),
            pl.BlockSpec((1, 1, D_MODEL, ft), lambda e, r, f: (layer, e, 0, f)),
            pl.BlockSpec((1, 1, ft, D_MODEL), lambda e, r, f: (layer, e, f, 0)),
        ],
        out_specs=pl.BlockSpec((1, rt, D_MODEL), lambda e, r, f: (e, r, 0)),
        out_shape=jax.ShapeDtypeStruct((n_exp, rows, D_MODEL), F32),
        compiler_params=_params(48, "parallel", "parallel", "arbitrary"),
        name="moe_ffn",
    )(xs, w_gate, w_up, w_down)


def _scatter_kernel(idx_ref, aff_ref, ys_ref, h_ref, o_ref, zy_ref, *, cap, cap_pad):
    @pl.when(pl.program_id(1) == 0)
    def _():
        o_ref[...] = h_ref[...]

    y = ys_ref[0, 0]
    _store_chunks(zy_ref, 0, cap_pad,
                  [y[:, j * LANES:(j + 1) * LANES] for j in range(N_CHUNKS)])
    out = o_ref.at[0]

    def add_rows(slots):
        rows, vals = [], []
        for s in slots:
            t = idx_ref[0, 0, s]
            gate = aff_ref[0, 0, t]
            r0 = pl.multiple_of(t * SUBLANES, SUBLANES)
            s0 = s * SUBLANES if isinstance(s, int) else pl.multiple_of(s * SUBLANES, SUBLANES)
            vals.append(out[pl.ds(r0, SUBLANES), :] + gate * zy_ref[pl.ds(s0, SUBLANES), :])
            rows.append(r0)
        for r0, v in zip(rows, vals):
            out[pl.ds(r0, SUBLANES), :] = v

    def body(i, carry):
        add_rows([i * SCATTER_GROUP + k for k in range(SCATTER_GROUP)])
        return carry

    n_groups = cap // SCATTER_GROUP
    lax.fori_loop(0, n_groups, body, 0)
    if cap % SCATTER_GROUP:
        add_rows(list(range(n_groups * SCATTER_GROUP, cap)))


def _scatter(idx_smem, aff_smem, ys, h, *, cap, cap_pad):
    b = h.shape[0]
    slots_pad = idx_smem.shape[-1]
    seq_pad = aff_smem.shape[-1]
    resident = pl.BlockSpec((1,) + h.shape[1:], lambda bi, e: (bi, 0, 0),
                            pipeline_mode=pl.Buffered(1))
    return pl.pallas_call(
        functools.partial(_scatter_kernel, cap=cap, cap_pad=cap_pad),
        grid=(b, N_EXPERTS),
        in_specs=[
            pl.BlockSpec((1, 1, slots_pad), lambda bi, e: (bi * N_EXPERTS + e, 0, 0),
                         memory_space=pltpu.SMEM),
            pl.BlockSpec((1, 1, seq_pad), lambda bi, e: (bi * N_EXPERTS + e, 0, 0),
                         memory_space=pltpu.SMEM),
            pl.BlockSpec((1, 1, cap_pad, D_MODEL), lambda bi, e: (e, bi, 0, 0)),
            resident,
        ],
        out_specs=resident,
        out_shape=jax.ShapeDtypeStruct(h.shape, F32),
        scratch_shapes=[pltpu.VMEM((cap_pad * SUBLANES, LANES), F32)],
        compiler_params=_params(56, "parallel", "arbitrary"),
        name="moe_scatter",
    )(idx_smem, aff_smem, ys, h)


def _moe_layer(h, g, router_w, w_gate, w_up, w_down, layer, *, seq, tt):
    b = h.shape[0]
    cap = CAPACITY_FACTOR * seq // N_EXPERTS
    cap_pad = _round_up(cap, BF16_ROWS)
    assert (b * cap_pad) % (FFN_ROW_TILES * BF16_ROWS) == 0
    slots_pad = _round_up(cap_pad, LANES)
    seq_pad = _round_up(seq, LANES)
    aff_t = _route(h, g, router_w, seq=seq, tt=tt, seq_pad=seq_pad)
    idx = _topk(aff_t, seq=seq, cap=cap, cap_pad=cap_pad, slots_pad=slots_pad)
    idx_smem = idx.reshape(b * N_EXPERTS, 1, slots_pad)
    aff_smem = aff_t.reshape(b * N_EXPERTS, 1, seq_pad)
    xs = _gather(idx_smem, h, g, cap_pad=cap_pad)
    ys = _ffn(xs.reshape(N_EXPERTS, b * cap_pad, D_MODEL), w_gate, w_up, w_down, layer)
    ys = ys.reshape(N_EXPERTS, b, cap_pad, D_MODEL)
    return _scatter(idx_smem, aff_smem, ys, h, cap=cap, cap_pad=cap_pad)


def _final_kernel(ha_ref, hb_ref, g_ref, o_ref, *, tt):
    ha = ha_ref.at[0]
    hb = hb_ref.at[0]
    ch = [jnp.concatenate(
        [ha[pl.ds(N_META * SUBLANES + j, tt - N_META, stride=SUBLANES), :],
         hb[pl.ds(j, N_META, stride=SUBLANES), :]], axis=0) for j in range(N_CHUNKS)]
    o_ref[0] = jnp.concatenate(_rms_chunks(ch, g_ref), axis=1)


def _final(h, g, *, seq, tt):
    b = h.shape[0]
    s_out = seq - N_META
    assert s_out % tt == 0
    assert tt % N_META == 0
    return pl.pallas_call(
        functools.partial(_final_kernel, tt=tt),
        grid=(b, s_out // tt),
        in_specs=[
            pl.BlockSpec((1, tt * SUBLANES, LANES), lambda bi, i: (bi, i, 0)),
            pl.BlockSpec((1, N_META * SUBLANES, LANES),
                         lambda bi, i: (bi, (i + 1) * (tt // N_META), 0)),
            pl.BlockSpec((1, D_MODEL), lambda bi, i: (0, 0)),
        ],
        out_specs=pl.BlockSpec((1, tt, D_MODEL), lambda bi, i: (bi, i, 0)),
        out_shape=jax.ShapeDtypeStruct((b, s_out, D_MODEL), F32),
        compiler_params=_params(40, "parallel", "parallel"),
        name="final_norm",
    )(h, h, g.reshape(1, D_MODEL))


def kernel(x, meta_tokens, norm_mix, norm_ffn, norm_final, pool_w, pool_scale, lru_w_in, lru_conv_w, lru_conv_b, lru_w_gates, lru_b_gates, lru_lambda, lru_w_out, router_w, moe_w_gate, moe_w_up, moe_w_down):
    b, s, d = x.shape
    assert d == D_MODEL
    seq = s + N_META
    tt = TIME_TILE
    depth = norm_mix.shape[0]
    h = _embed(x, meta_tokens.astype(x.dtype), tt=tt)
    for i in range(depth):
        j = i // 2
        if i % 2 == 0:
            h = _pool_layer(h, norm_mix[i], pool_w[j], pool_scale[j], seq=seq, tt=tt)
        else:
            h = _lru_layer(h, norm_mix[i], lru_w_in[j], lru_conv_w[j], lru_conv_b[j],
                           lru_w_gates[j], lru_b_gates[j], lru_lambda[j], lru_w_out[j],
                           seq=seq, tt=tt)
        h = _moe_layer(h, norm_ffn[i], router_w[i], moe_w_gate, moe_w_up, moe_w_down, i,
                       seq=seq, tt=tt)
    return _final(h, norm_final, seq=seq, tt=tt)
```

```python
import functools

import jax
import jax.numpy as jnp
from jax import lax
from jax.experimental import pallas as pl
from jax.experimental.pallas import tpu as pltpu

F32 = jnp.float32
BF16 = jnp.bfloat16

LANES = 128
SUBLANES = 8
BF16_ROWS = 16
D_MODEL = 1024
N_CHUNKS = D_MODEL // LANES
assert N_CHUNKS == SUBLANES
N_META = 16
POOL_WINDOWS = (2, 4, 8, 16)
POOL_GROUP = D_MODEL // len(POOL_WINDOWS)
CHUNKS_PER_GROUP = POOL_GROUP // LANES
POOL_HALO = 8
LRU_HEADS = 4
LRU_HEAD_DIM = D_MODEL // LRU_HEADS
CHUNKS_PER_HEAD = LRU_HEAD_DIM // LANES
LRU_C = 8.0
CONV_WIDTH = 4
CONV_LEFT = 1
CONV_RIGHT = CONV_WIDTH - 1 - CONV_LEFT
N_EXPERTS = 16
CAPACITY_FACTOR = 2
RMS_EPS = 1e-6
TIME_TILE = 512
FFN_ROW_TILES = 4
FFN_F_TILE = 1024
SCATTER_GROUP = 8
COUNT_RADIX = 32.0
MIB = 1024 * 1024


def _cdiv(a, b):
    return -(-a // b)


def _round_up(a, b):
    return _cdiv(a, b) * b


def _chunks(ref2d, tok0, n):
    return [ref2d[pl.ds(tok0 * SUBLANES + j, n, stride=SUBLANES), :]
            for j in range(N_CHUNKS)]


def _store_chunks(ref2d, tok0, n, chunks):
    for j in range(N_CHUNKS):
        ref2d[pl.ds(tok0 * SUBLANES + j, n, stride=SUBLANES), :] = chunks[j]


def _lane_chunk(ref, j):
    return ref[:, j * LANES:(j + 1) * LANES]


def _rms_chunks(chunks, g_ref):
    ss = chunks[0] * chunks[0]
    for c in chunks[1:]:
        ss = ss + c * c
    ms = jnp.sum(ss, axis=-1, keepdims=True) * (1.0 / D_MODEL)
    inv = lax.rsqrt(ms + RMS_EPS)
    return [c * inv * _lane_chunk(g_ref, j) for j, c in enumerate(chunks)]


def _token_ids(tok0, n):
    return tok0 + lax.broadcasted_iota(jnp.int32, (n, 1), 0)


def _params(vmem_mib, *sem):
    return pltpu.CompilerParams(dimension_semantics=sem,
                                vmem_limit_bytes=vmem_mib * MIB)


def _embed_kernel(meta_ref, xa_ref, xb_ref, o_ref, *, tt):
    i = pl.program_id(1)
    out = o_ref.at[0]
    body = xb_ref[0, :tt - N_META, :]
    _store_chunks(out, N_META, tt - N_META,
                  [body[:, j * LANES:(j + 1) * LANES] for j in range(N_CHUNKS)])

    def head(src):
        _store_chunks(out, 0, N_META,
                      [src[:, j * LANES:(j + 1) * LANES] for j in range(N_CHUNKS)])

    @pl.when(i == 0)
    def _():
        head(meta_ref[...])

    @pl.when(i > 0)
    def _():
        head(xa_ref[0])


def _embed(x, meta, *, tt):
    b, s, _ = x.shape
    seq = s + N_META
    assert s % tt == 0 and tt % N_META == 0
    last = s // tt - 1
    metas_per_tile = tt // N_META
    return pl.pallas_call(
        functools.partial(_embed_kernel, tt=tt),
        grid=(b, _cdiv(seq, tt)),
        in_specs=[
            pl.BlockSpec((N_META, D_MODEL), lambda bi, i: (0, 0)),
            pl.BlockSpec((1, N_META, D_MODEL),
                         lambda bi, i: (bi, jnp.maximum(i * metas_per_tile - 1, 0), 0)),
            pl.BlockSpec((1, tt, D_MODEL), lambda bi, i: (bi, jnp.minimum(i, last), 0)),
        ],
        out_specs=pl.BlockSpec((1, tt * SUBLANES, LANES), lambda bi, i: (bi, i, 0)),
        out_shape=jax.ShapeDtypeStruct((b, seq * SUBLANES, LANES), F32),
        compiler_params=_params(40, "parallel", "parallel"),
        name="embed",
    )(meta, x, x)


def _pool_kernel(hp_ref, hm_ref, hn_ref, g_ref, w_ref, sc_ref, o_ref, ext_ref,
                 s2_ref, s4_ref, s8_ref, p_ref, *, seq, tt):
    ti = pl.program_id(1)
    t0 = ti * tt
    hm = hm_ref.at[0]
    out = o_ref.at[0]

    def norm_into_ext(src, n, ext_tok0, tok0):
        tok = _token_ids(tok0, n)
        valid = (tok >= 0) & (tok < seq)
        ch = [jnp.where(valid, c, 0.0) for c in _chunks(src, 0, n)]
        _store_chunks(ext_ref, ext_tok0, n, _rms_chunks(ch, g_ref))

    norm_into_ext(hp_ref.at[0], POOL_HALO, 0, t0 - POOL_HALO)
    norm_into_ext(hm, tt, POOL_HALO, t0)
    norm_into_ext(hn_ref.at[0], POOL_HALO, POOL_HALO + tt, t0 + tt)

    def rows(a, b):
        return pl.ds(a * SUBLANES, (b - a) * SUBLANES)

    lo, hi = 1, tt + 2 * POOL_HALO
    s2_ref[rows(lo, hi), :] = ext_ref[rows(lo - 1, hi - 1), :] + ext_ref[rows(lo, hi), :]
    prev, half = s2_ref, 1
    for nxt in (s4_ref, s8_ref):
        lo, hi = lo + half, hi - half
        nxt[rows(lo, hi), :] = (prev[rows(lo - half, hi - half), :]
                                + prev[rows(lo + half, hi + half), :])
        prev, half = nxt, 2 * half
    m0, m1 = POOL_HALO, POOL_HALO + tt
    assert lo + half <= m0 and m1 <= hi - half and 4 * half == POOL_WINDOWS[-1]

    grp = lax.broadcasted_iota(jnp.int32, (SUBLANES, LANES), 0) // CHUNKS_PER_GROUP

    def window_sums():
        def tiles(v):
            return v.reshape(tt, SUBLANES, LANES)

        s16 = prev[rows(m0 - half, m1 - half), :] + prev[rows(m0 + half, m1 + half), :]
        return jnp.where(grp == 0, tiles(s2_ref[rows(m0, m1), :]),
                         jnp.where(grp == 1, tiles(s4_ref[rows(m0, m1), :]),
                                   jnp.where(grp == 2, tiles(s8_ref[rows(m0, m1), :]),
                                             tiles(s16))))

    near_end = (ti == 0) | (t0 + tt > seq - POOL_HALO)

    @pl.when(jnp.logical_not(near_end))
    def _():
        inv_win = jnp.where(grp == 0, 0.5, jnp.where(grp == 1, 0.25,
                                                     jnp.where(grp == 2, 0.125, 0.0625)))
        p_ref[...] = ((window_sums() * inv_win).reshape(tt * SUBLANES, LANES)
                      - ext_ref[rows(m0, m1), :])

    @pl.when(near_end)
    def _():
        r = lax.broadcasted_iota(jnp.int32, (tt * SUBLANES, 1), 0)
        tok = t0 + (r >> 3)
        left = lax.shift_left(jnp.int32(1), (r & (SUBLANES - 1)) // CHUNKS_PER_GROUP)
        cnt = jnp.minimum(tok + left, seq) - jnp.maximum(tok - left, 0)
        cnt = jnp.maximum(cnt, 1).astype(F32)
        p_ref[...] = (window_sums().reshape(tt * SUBLANES, LANES) / cnt
                      - ext_ref[rows(m0, m1), :])

    for g in range(len(POOL_WINDOWS)):
        group_chunks = range(g * CHUNKS_PER_GROUP, (g + 1) * CHUNKS_PER_GROUP)
        p = jnp.concatenate([p_ref[pl.ds(c, tt, stride=SUBLANES), :] for c in group_chunks],
                            axis=1).astype(BF16)
        y = jnp.dot(p, w_ref[g], preferred_element_type=F32)
        for q, c in enumerate(group_chunks):
            res = hm[pl.ds(c, tt, stride=SUBLANES), :]
            out[pl.ds(c, tt, stride=SUBLANES), :] = (
                res + y[:, q * LANES:(q + 1) * LANES] * _lane_chunk(sc_ref, c))


def _pool_layer(h, g, w, scale, *, seq, tt):
    b = h.shape[0]
    nt = _cdiv(seq, tt)
    halo_rows = POOL_HALO * SUBLANES
    blocks_per_tile = tt // POOL_HALO
    last_halo_block = seq // POOL_HALO - 1
    return pl.pallas_call(
        functools.partial(_pool_kernel, seq=seq, tt=tt),
        grid=(b, nt),
        in_specs=[
            pl.BlockSpec((1, halo_rows, LANES),
                         lambda bi, i: (bi, jnp.maximum(i * blocks_per_tile - 1, 0), 0)),
            pl.BlockSpec((1, tt * SUBLANES, LANES), lambda bi, i: (bi, i, 0)),
            pl.BlockSpec((1, halo_rows, LANES),
                         lambda bi, i: (bi, jnp.minimum((i + 1) * blocks_per_tile,
                                                        last_halo_block), 0)),
            pl.BlockSpec((1, D_MODEL), lambda bi, i: (0, 0)),
            pl.BlockSpec((len(POOL_WINDOWS), POOL_GROUP, POOL_GROUP),
                         lambda bi, i: (0, 0, 0)),
            pl.BlockSpec((1, D_MODEL), lambda bi, i: (0, 0)),
        ],
        out_specs=pl.BlockSpec((1, tt * SUBLANES, LANES), lambda bi, i: (bi, i, 0)),
        out_shape=jax.ShapeDtypeStruct(h.shape, F32),
        scratch_shapes=[pltpu.VMEM(((tt + 2 * POOL_HALO) * SUBLANES, LANES), F32)] * 4
        + [pltpu.VMEM((tt * SUBLANES, LANES), F32)],
        compiler_params=_params(40, "parallel", "arbitrary"),
        name="pool_mixer",
    )(h, h, h, g.reshape(1, D_MODEL), w.astype(BF16), scale.reshape(1, D_MODEL))


def _gelu_tanh(x):
    c = 0.7978845608028654
    return 0.5 * x * (1.0 + jnp.tanh(c * (x + 0.044715 * (x * x * x))))


def _lru_in_kernel(h_ref, g_ref, w_ref, xb_ref, gy_ref, *, tf):
    u = jnp.concatenate(_rms_chunks(_chunks(h_ref, 0, tf), g_ref), axis=1).astype(BF16)
    xb = jnp.dot(u, w_ref[:, :D_MODEL], preferred_element_type=F32)
    _store_chunks(xb_ref, 0, tf,
                  [xb[:, j * LANES:(j + 1) * LANES] for j in range(N_CHUNKS)])
    yb = jnp.dot(u, w_ref[:, D_MODEL:], preferred_element_type=F32)
    gy_ref[...] = _gelu_tanh(yb)


def _lru_in(h_flat, g, w_in, *, tf):
    n_tok = h_flat.shape[0] // SUBLANES
    return pl.pallas_call(
        functools.partial(_lru_in_kernel, tf=tf),
        grid=(_cdiv(n_tok, tf),),
        in_specs=[
            pl.BlockSpec((tf * SUBLANES, LANES), lambda i: (i, 0)),
            pl.BlockSpec((1, D_MODEL), lambda i: (0, 0)),
            pl.BlockSpec((D_MODEL, 2 * D_MODEL), lambda i: (0, 0)),
        ],
        out_specs=[
            pl.BlockSpec((tf * SUBLANES, LANES), lambda i: (i, 0)),
            pl.BlockSpec((tf, D_MODEL), lambda i: (i, 0)),
        ],
        out_shape=[
            jax.ShapeDtypeStruct(h_flat.shape, F32),
            jax.ShapeDtypeStruct((n_tok, D_MODEL), F32),
        ],
        compiler_params=_params(48, "parallel"),
        name="lru_in_proj",
    )(h_flat, g.reshape(1, D_MODEL), w_in.astype(BF16))


def _softplus(x):
    return jnp.maximum(x, 0.0) + jnp.log1p(jnp.exp(-jnp.abs(x)))


def _sigmoid(x):
    return 0.5 * jnp.tanh(0.5 * x) + 0.5


def _lru_scan_kernel(*refs, seq, tt, reverse):
    if reverse:
        (xp_ref, xm_ref, xn_ref, cw_ref, cb_ref, wg_ref, bg_ref, lam_ref,
         o_ref, ext_ref, a_ref, b_ref, carry_ref) = refs
        hs = o_ref.at[0]
    else:
        (xp_ref, xm_ref, xn_ref, cw_ref, cb_ref, wg_ref, bg_ref, lam_ref,
         hb_ref, gy_ref, res_ref, wo_ref,
         o_ref, ext_ref, a_ref, b_ref, carry_ref, hs) = refs
    step = pl.program_id(1)
    nt = pl.num_programs(1)
    ti = nt - 1 - step if reverse else step
    t0 = ti * tt

    @pl.when(step == 0)
    def _():
        carry_ref[...] = jnp.zeros_like(carry_ref)

    def masked_rows(src, n_tok, tok0):
        rows = n_tok * SUBLANES
        tok = tok0 + (lax.broadcasted_iota(jnp.int32, (rows, 1), 0) >> 3)
        return jnp.where((tok >= 0) & (tok < seq), src[...], 0.0)

    main_rows = pl.ds(CONV_LEFT * SUBLANES, tt * SUBLANES)
    ext_ref[pl.ds(0, CONV_LEFT * SUBLANES), :] = masked_rows(
        xp_ref.at[0], CONV_LEFT, t0 - CONV_LEFT)
    ext_ref[main_rows, :] = xm_ref[0]
    ext_ref[pl.ds((CONV_LEFT + tt) * SUBLANES, CONV_RIGHT * SUBLANES), :] = masked_rows(
        xn_ref.at[0], CONV_RIGHT, t0 + tt)
    ragged = t0 + tt > seq

    @pl.when(ragged)
    def _():
        ext_ref[main_rows, :] = masked_rows(xm_ref.at[0], tt, t0)

    for hh in range(LRU_HEADS):
        head_chunks = range(hh * CHUNKS_PER_HEAD, (hh + 1) * CHUNKS_PER_HEAD)
        xcs = []
        for c in head_chunks:
            xc = _lane_chunk(cb_ref, c)
            for k in range(CONV_WIDTH):
                xc = xc + (ext_ref[pl.ds(k * SUBLANES + c, tt, stride=SUBLANES), :]
                           * cw_ref[k:k + 1, c * LANES:(c + 1) * LANES])
            xcs.append(xc)
        xh = jnp.concatenate(xcs, axis=1)
        gl = jnp.dot(xh.astype(BF16), wg_ref[hh], preferred_element_type=F32) + bg_ref[hh]
        r = _sigmoid(gl[:, :LRU_HEAD_DIM])
        ig = _sigmoid(gl[:, LRU_HEAD_DIM:])
        lam = lam_ref[:, hh * LRU_HEAD_DIM:(hh + 1) * LRU_HEAD_DIM]
        log_a = r * (-LRU_C * _softplus(-lam))
        a = jnp.exp(log_a)
        bb = jnp.sqrt(jnp.tanh(-log_a) * (a * a + 1.0)) * (ig * xh)
        for q, c in enumerate(head_chunks):
            a_ref[pl.ds(c, tt, stride=SUBLANES), :] = a[:, q * LANES:(q + 1) * LANES]
            b_ref[pl.ds(c, tt, stride=SUBLANES), :] = bb[:, q * LANES:(q + 1) * LANES]

    @pl.when(ragged)
    def _():
        tok = t0 + (lax.broadcasted_iota(jnp.int32, (tt * SUBLANES, 1), 0) >> 3)
        b_ref[...] = jnp.where(tok < seq, b_ref[...], 0.0)

    def scan_body(i, h):
        for k in range(SUBLANES):
            t = (tt - 1 - (i * SUBLANES + k)) if reverse else (i * SUBLANES + k)
            r0 = pl.multiple_of(t * SUBLANES, SUBLANES)
            h = a_ref[pl.ds(r0, SUBLANES), :] * h + b_ref[pl.ds(r0, SUBLANES), :]
            hs[pl.ds(r0, SUBLANES), :] = h
        return h

    carry_ref[...] = lax.fori_loop(0, tt // SUBLANES, scan_body, carry_ref[...])

    if not reverse:
        hb = hb_ref.at[0]
        hsum = jnp.concatenate(
            [hs[pl.ds(j, tt, stride=SUBLANES), :] + hb[pl.ds(j, tt, stride=SUBLANES), :]
             for j in range(N_CHUNKS)], axis=1)
        z = (hsum * gy_ref[0]).astype(BF16)
        y = jnp.dot(z, wo_ref[...], preferred_element_type=F32)
        res = res_ref.at[0]
        out = o_ref.at[0]
        for j in range(N_CHUNKS):
            out[pl.ds(j, tt, stride=SUBLANES), :] = (
                res[pl.ds(j, tt, stride=SUBLANES), :] + y[:, j * LANES:(j + 1) * LANES])


def _lru_scan(xb, conv_w, conv_b, wg, bg, lam, *, seq, tt, reverse,
              hb=None, gy=None, res=None, w_out=None):
    b = xb.shape[0]
    nt = _cdiv(seq, tt)
    right_rows = CONV_RIGHT * SUBLANES
    last_right_block = seq // CONV_RIGHT - 1

    def tix(i):
        return nt - 1 - i if reverse else i

    tile_spec = pl.BlockSpec((1, tt * SUBLANES, LANES), lambda bi, i: (bi, tix(i), 0))

    def const_spec(shape):
        return pl.BlockSpec(shape, lambda bi, i: (0,) * len(shape))

    in_specs = [
        pl.BlockSpec((1, CONV_LEFT * SUBLANES, LANES),
                     lambda bi, i: (bi, jnp.maximum(tix(i) * (tt // CONV_LEFT) - 1, 0), 0)),
        tile_spec,
        pl.BlockSpec((1, right_rows, LANES),
                     lambda bi, i: (bi, jnp.minimum((tix(i) + 1) * (tt // CONV_RIGHT),
                                                    last_right_block), 0)),
        const_spec((CONV_WIDTH, D_MODEL)),
        const_spec((1, D_MODEL)),
        const_spec((LRU_HEADS, LRU_HEAD_DIM, 2 * LRU_HEAD_DIM)),
        const_spec((LRU_HEADS, 1, 2 * LRU_HEAD_DIM)),
        const_spec((1, D_MODEL)),
    ]
    args = [xb, xb, xb, conv_w, conv_b.reshape(1, D_MODEL), wg, bg, lam.reshape(1, D_MODEL)]
    scratch = [
        pltpu.VMEM(((tt + CONV_WIDTH - 1) * SUBLANES, LANES), F32),
        pltpu.VMEM((tt * SUBLANES, LANES), F32),
        pltpu.VMEM((tt * SUBLANES, LANES), F32),
        pltpu.VMEM((SUBLANES, LANES), F32),
    ]
    if not reverse:
        in_specs += [
            tile_spec,
            pl.BlockSpec((1, tt, D_MODEL), lambda bi, i: (bi, i, 0)),
            tile_spec,
            const_spec((D_MODEL, D_MODEL)),
        ]
        args += [hb, gy, res, w_out]
        scratch.append(pltpu.VMEM((tt * SUBLANES, LANES), F32))
    return pl.pallas_call(
        functools.partial(_lru_scan_kernel, seq=seq, tt=tt, reverse=reverse),
        grid=(b, nt),
        in_specs=in_specs,
        out_specs=tile_spec,
        out_shape=jax.ShapeDtypeStruct(xb.shape, F32),
        scratch_shapes=scratch,
        compiler_params=_params(48, "parallel", "arbitrary"),
        name="lru_scan_rev" if reverse else "lru_scan_fwd_out",
    )(*args)


def _lru_layer(h, g, w_in, conv_w, conv_b, w_gates, b_gates, lam, w_out, *, seq, tt):
    b = h.shape[0]
    xb, gy = _lru_in(h.reshape(b * seq * SUBLANES, LANES), g, w_in, tf=tt)
    xb = xb.reshape(h.shape)
    gy = gy.reshape(b, seq, D_MODEL)
    wg = jnp.concatenate([w_gates[:, 0], w_gates[:, 1]], axis=-1).astype(BF16)
    bg = jnp.concatenate([b_gates[:, 0], b_gates[:, 1]], axis=-1)[:, :, None, :]
    scan = functools.partial(_lru_scan, xb, conv_w, conv_b, seq=seq, tt=tt)
    hb = scan(wg[1], bg[1], lam[1], reverse=True)
    return scan(wg[0], bg[0], lam[0], reverse=False,
                hb=hb, gy=gy, res=h, w_out=w_out.astype(BF16))


def _split_bf16(x):
    hi = x.astype(BF16)
    return hi, (x - hi.astype(F32)).astype(BF16)


def _route_kernel(h_ref, g_ref, rw_ref, aff_ref, *, tt):
    u = jnp.concatenate(_rms_chunks(_chunks(h_ref.at[0], 0, tt), g_ref), axis=1)
    u_hi, u_lo = _split_bf16(u)
    p = jnp.dot(u_hi, rw_ref[...], preferred_element_type=F32)
    q = jnp.dot(u_lo, rw_ref[:, :LANES], preferred_element_type=F32)
    logits = ((p[:, :LANES] + p[:, LANES:]) + q).T[:N_EXPERTS, :]
    m = jnp.max(logits, axis=0, keepdims=True)
    ex = jnp.exp(logits - m)
    aff_ref[0] = ex / jnp.sum(ex, axis=0, keepdims=True)


def _route(h, g, router_w, *, seq, tt, seq_pad):
    b = h.shape[0]
    w_hi, w_lo = _split_bf16(jnp.pad(router_w, ((0, 0), (0, LANES - N_EXPERTS))))
    return pl.pallas_call(
        functools.partial(_route_kernel, tt=tt),
        grid=(b, _cdiv(seq, tt)),
        in_specs=[
            pl.BlockSpec((1, tt * SUBLANES, LANES), lambda bi, i: (bi, i, 0)),
            pl.BlockSpec((1, D_MODEL), lambda bi, i: (0, 0)),
            pl.BlockSpec((D_MODEL, 2 * LANES), lambda bi, i: (0, 0)),
        ],
        out_specs=pl.BlockSpec((1, N_EXPERTS, tt), lambda bi, i: (bi, 0, i)),
        out_shape=jax.ShapeDtypeStruct((b, N_EXPERTS, seq_pad), F32),
        compiler_params=_params(40, "parallel", "parallel"),
        name="moe_route",
    )(h, g.reshape(1, D_MODEL), jnp.concatenate([w_hi, w_lo], axis=1))


def _cumsum_blocks(x01):
    r = lax.broadcasted_iota(jnp.int32, (LANES, LANES), 0)
    c = lax.broadcasted_iota(jnp.int32, (LANES, LANES), 1)
    tri = (r <= c).astype(BF16)
    local = [jnp.dot(x01[:, j * LANES:(j + 1) * LANES].astype(BF16), tri,
                     preferred_element_type=F32) for j in range(x01.shape[1] // LANES)]
    off = jnp.zeros((x01.shape[0], 1), F32)
    outs = []
    for cs in local:
        outs.append(cs + off)
        off = off + cs[:, LANES - 1:LANES]
    return outs


def _topk_kernel(aff_ref, idx_ref, cs_ref, *, seq, cap, cap_pad, slots_pad):
    aff = aff_ref[0]
    lane = lax.broadcasted_iota(jnp.int32, aff.shape, 1)
    aff = jnp.where(lane < seq, aff, -1.0)
    capf = float(cap)
    n_blk = aff.shape[1] // LANES

    def bit_step(i, cur):
        cand = cur | lax.shift_left(jnp.int32(1), 30 - i)
        candf = lax.bitcast_convert_type(cand, F32)
        cnt = jnp.sum((aff >= candf).astype(F32), axis=1, keepdims=True)
        return jnp.where(cnt >= capf, cand, cur)

    cur = lax.fori_loop(0, 31, bit_step, jnp.zeros((N_EXPERTS, 1), jnp.int32))
    thr = lax.bitcast_convert_type(cur, F32)
    gt = aff > thr
    eq = aff == thr
    need = capf - jnp.sum(gt.astype(F32), axis=1, keepdims=True)
    eq_cnt = jnp.concatenate(_cumsum_blocks(eq.astype(F32)), axis=1)
    sel = gt | (eq & (eq_cnt <= need))

    blocks = _cumsum_blocks(sel.astype(F32))
    for j, cs in enumerate(blocks):
        cs_ref[pl.ds(j * N_EXPERTS, N_EXPERTS), :] = cs
    cs_ref[pl.ds(n_blk * N_EXPERTS, (LANES - n_blk) * N_EXPERTS), :] = jnp.zeros(
        ((LANES - n_blk) * N_EXPERTS, LANES), F32)
    never = jnp.full((N_EXPERTS, LANES - n_blk), 2.0 * COUNT_RADIX * 256, F32)
    blk_end = jnp.concatenate([cs[:, LANES - 1:LANES] for cs in blocks] + [never], axis=1)

    slot = lax.broadcasted_iota(jnp.int32, (cap_pad, 1), 0).astype(F32)
    lane_s = lax.broadcasted_iota(jnp.int32, (cap_pad, LANES), 1).astype(F32)
    lane_e = lax.broadcasted_iota(jnp.int32, (slots_pad, LANES), 1)
    experts = range(N_EXPERTS)
    digits = []
    for e in experts:
        cmat = cs_ref[pl.ds(e, LANES, stride=N_EXPERTS), :]
        hi = jnp.floor(cmat * (1.0 / COUNT_RADIX))
        digits.append(jnp.concatenate([hi, cmat - COUNT_RADIX * hi], axis=1).astype(BF16))
    ones = jnp.ones((LANES, LANES), BF16)

    def count_le(x):
        return jnp.dot((x <= slot).astype(BF16), ones, preferred_element_type=F32)

    full = [count_le(blk_end[e:e + 1, :]) for e in experts]
    pick = [(lane_s == full[e]).astype(BF16) for e in experts]
    rows = [jnp.dot(pick[e], digits[e], preferred_element_type=F32) for e in experts]
    rows = [COUNT_RADIX * r_[:, :LANES] + r_[:, LANES:] for r_ in rows]
    inside = [count_le(rows[e]) for e in experts]
    acc = jnp.zeros((slots_pad, LANES), F32)
    pad_rows = jnp.zeros((slots_pad - cap_pad, LANES), F32)
    for e in experts:
        col = jnp.concatenate([LANES * full[e] + inside[e], pad_rows], axis=0)
        acc = jnp.where(lane_e == e, col, acc)
    idx = acc.T[:N_EXPERTS, :].astype(jnp.int32)
    slot_l = lax.broadcasted_iota(jnp.int32, idx.shape, 1)
    idx_ref[0] = jnp.where(slot_l < cap, idx, 0)


def _topk(aff_t, *, seq, cap, cap_pad, slots_pad):
    b, _, seq_pad = aff_t.shape
    assert seq_pad // LANES <= LANES and cap < COUNT_RADIX * 256
    return pl.pallas_call(
        functools.partial(_topk_kernel, seq=seq, cap=cap, cap_pad=cap_pad,
                          slots_pad=slots_pad),
        grid=(b,),
        in_specs=[pl.BlockSpec((1, N_EXPERTS, seq_pad), lambda bi: (bi, 0, 0))],
        out_specs=pl.BlockSpec((1, N_EXPERTS, slots_pad), lambda bi: (bi, 0, 0)),
        out_shape=jax.ShapeDtypeStruct((b, N_EXPERTS, slots_pad), jnp.int32),
        scratch_shapes=[pltpu.VMEM((LANES * N_EXPERTS, LANES), F32)],
        compiler_params=_params(40, "parallel"),
        name="moe_topk",
    )(aff_t)


def _gather_kernel(idx_ref, h_ref, g_ref, xs_ref, zx_ref, *, cap_pad):
    h = h_ref.at[0]

    def body(i, carry):
        for k in range(SUBLANES):
            s = i * SUBLANES + k
            t = idx_ref[0, 0, s]
            zx_ref[pl.ds(pl.multiple_of(s * SUBLANES, SUBLANES), SUBLANES), :] = (
                h[pl.ds(pl.multiple_of(t * SUBLANES, SUBLANES), SUBLANES), :])
        return carry

    lax.fori_loop(0, cap_pad // SUBLANES, body, 0)
    u = _rms_chunks(_chunks(zx_ref, 0, cap_pad), g_ref)
    xs_ref[0, 0] = jnp.concatenate(u, axis=1).astype(BF16)


def _gather(idx_smem, h, g, *, cap_pad):
    b = h.shape[0]
    slots_pad = idx_smem.shape[-1]
    return pl.pallas_call(
        functools.partial(_gather_kernel, cap_pad=cap_pad),
        grid=(b, N_EXPERTS),
        in_specs=[
            pl.BlockSpec((1, 1, slots_pad), lambda bi, e: (bi * N_EXPERTS + e, 0, 0),
                         memory_space=pltpu.SMEM),
            pl.BlockSpec((1,) + h.shape[1:], lambda bi, e: (bi, 0, 0)),
            pl.BlockSpec((1, D_MODEL), lambda bi, e: (0, 0)),
        ],
        out_specs=pl.BlockSpec((1, 1, cap_pad, D_MODEL), lambda bi, e: (e, bi, 0, 0)),
        out_shape=jax.ShapeDtypeStruct((N_EXPERTS, b, cap_pad, D_MODEL), BF16),
        scratch_shapes=[pltpu.VMEM((cap_pad * SUBLANES, LANES), F32)],
        compiler_params=_params(56, "parallel", "arbitrary"),
        name="moe_gather",
    )(idx_smem, h, g.reshape(1, D_MODEL))


def _ffn_kernel(x_ref, wg_ref, wu_ref, wd_ref, o_ref, hid_ref, wdb_ref, *, ft, n_f):
    r = pl.program_id(1)
    f = pl.program_id(2)

    @pl.when(r == 0)
    def _():
        wdb_ref[pl.ds(pl.multiple_of(f * ft, ft), ft), :] = wd_ref[0, 0].astype(BF16)

    x = x_ref[0]
    hg = jnp.dot(x, wg_ref[0, 0].astype(BF16), preferred_element_type=F32)
    hu = jnp.dot(x, wu_ref[0, 0].astype(BF16), preferred_element_type=F32)
    hid_ref[f] = (hg * jax.nn.sigmoid(hg) * hu).astype(BF16)

    @pl.when(f == n_f - 1)
    def _():
        hid = jnp.concatenate([hid_ref[k] for k in range(n_f)], axis=1)
        o_ref[0] = jnp.dot(hid, wdb_ref[...], preferred_element_type=F32)


def _ffn(xs, w_gate, w_up, w_down, layer):
    n_exp, rows, _ = xs.shape
    d_expert = w_gate.shape[-1]
    rt = rows // FFN_ROW_TILES
    ft = min(FFN_F_TILE, d_expert)
    n_f = d_expert // ft

    def wd_tile(e, r, f):
        return (layer, e, jnp.where(r == 0, f, n_f - 1), 0)

    return pl.pallas_call(
        functools.partial(_ffn_kernel, ft=ft, n_f=n_f),
        grid=(n_exp, FFN_ROW_TILES, n_f),
        in_specs=[
            pl.BlockSpec((1, rt, D_MODEL), lambda e, r, f: (e, r, 0)),
            pl.BlockSpec((1, 1, D_MODEL, ft), lambda e, r, f: (layer, e, 0, f)),
            pl.BlockSpec((1, 1, D_MODEL, ft), lambda e, r, f: (layer, e, 0, f)),
            pl.BlockSpec((1, 1, ft, D_MODEL), wd_tile),
        ],
        out_specs=pl.BlockSpec((1, rt, D_MODEL), lambda e, r, f: (e, r, 0)),
        out_shape=jax.ShapeDtypeStruct((n_exp, rows, D_MODEL), F32),
        scratch_shapes=[pltpu.VMEM((n_f, rt, ft), BF16),
                        pltpu.VMEM((d_expert, D_MODEL), BF16)],
        compiler_params=_params(60, "parallel", "arbitrary", "arbitrary"),
        name="moe_ffn",
    )(xs, w_gate, w_up, w_down)


def _scatter_kernel(idx_ref, aff_ref, ys_ref, h_ref, o_ref, zy_ref, *, cap, cap_pad):
    @pl.when(pl.program_id(1) == 0)
    def _():
        o_ref[...] = h_ref[...]

    y = ys_ref[0, 0]
    _store_chunks(zy_ref, 0, cap_pad,
                  [y[:, j * LANES:(j + 1) * LANES] for j in range(N_CHUNKS)])
    out = o_ref.at[0]

    def add_rows(slots):
        rows, vals = [], []
        for s in slots:
            t = idx_ref[0, 0, s]
            gate = aff_ref[0, 0, t]
            r0 = pl.multiple_of(t * SUBLANES, SUBLANES)
            s0 = s * SUBLANES if isinstance(s, int) else pl.multiple_of(s * SUBLANES, SUBLANES)
            vals.append(out[pl.ds(r0, SUBLANES), :] + gate * zy_ref[pl.ds(s0, SUBLANES), :])
            rows.append(r0)
        for r0, v in zip(rows, vals):
            out[pl.ds(r0, SUBLANES), :] = v

    def body(i, carry):
        add_rows([i * SCATTER_GROUP + k for k in range(SCATTER_GROUP)])
        return carry

    n_groups = cap // SCATTER_GROUP
    lax.fori_loop(0, n_groups, body, 0)
    if cap % SCATTER_GROUP:
        add_rows(list(range(n_groups * SCATTER_GROUP, cap)))


def _scatter(idx_smem, aff_smem, ys, h, *, cap, cap_pad):
    b = h.shape[0]
    slots_pad = idx_smem.shape[-1]
    seq_pad = aff_smem.shape[-1]
    resident = pl.BlockSpec((1,) + h.shape[1:], lambda bi, e: (bi, 0, 0),
                            pipeline_mode=pl.Buffered(1))
    return pl.pallas_call(
        functools.partial(_scatter_kernel, cap=cap, cap_pad=cap_pad),
        grid=(b, N_EXPERTS),
        in_specs=[
            pl.BlockSpec((1, 1, slots_pad), lambda bi, e: (bi * N_EXPERTS + e, 0, 0),
                         memory_space=pltpu.SMEM),
            pl.BlockSpec((1, 1, seq_pad), lambda bi, e: (bi * N_EXPERTS + e, 0, 0),
                         memory_space=pltpu.SMEM),
            pl.BlockSpec((1, 1, cap_pad, D_MODEL), lambda bi, e: (e, bi, 0, 0)),
            resident,
        ],
        out_specs=resident,
        out_shape=jax.ShapeDtypeStruct(h.shape, F32),
        scratch_shapes=[pltpu.VMEM((cap_pad * SUBLANES, LANES), F32)],
        compiler_params=_params(56, "parallel", "arbitrary"),
        name="moe_scatter",
    )(idx_smem, aff_smem, ys, h)


def _moe_layer(h, g, router_w, w_gate, w_up, w_down, layer, *, seq, tt):
    b = h.shape[0]
    cap = CAPACITY_FACTOR * seq // N_EXPERTS
    cap_pad = _round_up(cap, BF16_ROWS)
    assert (b * cap_pad) % (FFN_ROW_TILES * BF16_ROWS) == 0
    slots_pad = _round_up(cap_pad, LANES)
    seq_pad = _round_up(seq, LANES)
    aff_t = _route(h, g, router_w, seq=seq, tt=tt, seq_pad=seq_pad)
    idx = _topk(aff_t, seq=seq, cap=cap, cap_pad=cap_pad, slots_pad=slots_pad)
    idx_smem = idx.reshape(b * N_EXPERTS, 1, slots_pad)
    aff_smem = aff_t.reshape(b * N_EXPERTS, 1, seq_pad)
    xs = _gather(idx_smem, h, g, cap_pad=cap_pad)
    ys = _ffn(xs.reshape(N_EXPERTS, b * cap_pad, D_MODEL), w_gate, w_up, w_down, layer)
    ys = ys.reshape(N_EXPERTS, b, cap_pad, D_MODEL)
    return _scatter(idx_smem, aff_smem, ys, h, cap=cap, cap_pad=cap_pad)


def _final_kernel(ha_ref, hb_ref, g_ref, o_ref, *, tt):
    ha = ha_ref.at[0]
    hb = hb_ref.at[0]
    ch = [jnp.concatenate(
        [ha[pl.ds(N_META * SUBLANES + j, tt - N_META, stride=SUBLANES), :],
         hb[pl.ds(j, N_META, stride=SUBLANES), :]], axis=0) for j in range(N_CHUNKS)]
    o_ref[0] = jnp.concatenate(_rms_chunks(ch, g_ref), axis=1)


def _final(h, g, *, seq, tt):
    b = h.shape[0]
    s_out = seq - N_META
    assert s_out % tt == 0
    assert tt % N_META == 0
    return pl.pallas_call(
        functools.partial(_final_kernel, tt=tt),
        grid=(b, s_out // tt),
        in_specs=[
            pl.BlockSpec((1, tt * SUBLANES, LANES), lambda bi, i: (bi, i, 0)),
            pl.BlockSpec((1, N_META * SUBLANES, LANES),
                         lambda bi, i: (bi, (i + 1) * (tt // N_META), 0)),
            pl.BlockSpec((1, D_MODEL), lambda bi, i: (0, 0)),
        ],
        out_specs=pl.BlockSpec((1, tt, D_MODEL), lambda bi, i: (bi, i, 0)),
        out_shape=jax.ShapeDtypeStruct((b, s_out, D_MODEL), F32),
        compiler_params=_params(40, "parallel", "parallel"),
        name="final_norm",
    )(h, h, g.reshape(1, D_MODEL))


def kernel(x, meta_tokens, norm_mix, norm_ffn, norm_final, pool_w, pool_scale, lru_w_in, lru_conv_w, lru_conv_b, lru_w_gates, lru_b_gates, lru_lambda, lru_w_out, router_w, moe_w_gate, moe_w_up, moe_w_down):
    b, s, d = x.shape
    assert d == D_MODEL
    seq = s + N_META
    tt = TIME_TILE
    depth = norm_mix.shape[0]
    h = _embed(x, meta_tokens.astype(x.dtype), tt=tt)
    for i in range(depth):
        j = i // 2
        if i % 2 == 0:
            h = _pool_layer(h, norm_mix[i], pool_w[j], pool_scale[j], seq=seq, tt=tt)
        else:
            h = _lru_layer(h, norm_mix[i], lru_w_in[j], lru_conv_w[j], lru_conv_b[j],
                           lru_w_gates[j], lru_b_gates[j], lru_lambda[j], lru_w_out[j],
                           seq=seq, tt=tt)
        h = _moe_layer(h, norm_ffn[i], router_w[i], moe_w_gate, moe_w_up, moe_w_down, i,
                       seq=seq, tt=tt)
    return _final(h, norm_final, seq=seq, tt=tt)
```

```python
import functools

import jax
import jax.numpy as jnp
from jax import lax
from jax.experimental import pallas as pl
from jax.experimental.pallas import tpu as pltpu

F32 = jnp.float32
BF16 = jnp.bfloat16

LANES = 128
SUBLANES = 8
BF16_ROWS = 16
D_MODEL = 1024
N_CHUNKS = D_MODEL // LANES
assert N_CHUNKS == SUBLANES
N_META = 16
POOL_WINDOWS = (2, 4, 8, 16)
POOL_GROUP = D_MODEL // len(POOL_WINDOWS)
CHUNKS_PER_GROUP = POOL_GROUP // LANES
POOL_HALO = 8
LRU_HEADS = 4
LRU_HEAD_DIM = D_MODEL // LRU_HEADS
CHUNKS_PER_HEAD = LRU_HEAD_DIM // LANES
LRU_C = 8.0
CONV_WIDTH = 4
CONV_LEFT = 1
CONV_RIGHT = CONV_WIDTH - 1 - CONV_LEFT
N_EXPERTS = 16
CAPACITY_FACTOR = 2
RMS_EPS = 1e-6
TIME_TILE = 512
FFN_ROW_TILES = 4
FFN_F_TILE = 1024
SCATTER_GROUP = 8
COUNT_RADIX = 32.0
MIB = 1024 * 1024


def _cdiv(a, b):
    return -(-a // b)


def _round_up(a, b):
    return _cdiv(a, b) * b


def _chunks(ref2d, tok0, n):
    return [ref2d[pl.ds(tok0 * SUBLANES + j, n, stride=SUBLANES), :]
            for j in range(N_CHUNKS)]


def _store_chunks(ref2d, tok0, n, chunks):
    for j in range(N_CHUNKS):
        ref2d[pl.ds(tok0 * SUBLANES + j, n, stride=SUBLANES), :] = chunks[j]


def _lane_chunk(ref, j):
    return ref[:, j * LANES:(j + 1) * LANES]


def _rms_chunks(chunks, g_ref):
    ss = chunks[0] * chunks[0]
    for c in chunks[1:]:
        ss = ss + c * c
    ms = jnp.sum(ss, axis=-1, keepdims=True) * (1.0 / D_MODEL)
    inv = lax.rsqrt(ms + RMS_EPS)
    return [c * inv * _lane_chunk(g_ref, j) for j, c in enumerate(chunks)]


def _token_ids(tok0, n):
    return tok0 + lax.broadcasted_iota(jnp.int32, (n, 1), 0)


def _params(vmem_mib, *sem):
    return pltpu.CompilerParams(dimension_semantics=sem,
                                vmem_limit_bytes=vmem_mib * MIB)


def _embed_kernel(meta_ref, xa_ref, xb_ref, o_ref, *, tt):
    i = pl.program_id(1)
    out = o_ref.at[0]
    body = xb_ref[0, :tt - N_META, :]
    _store_chunks(out, N_META, tt - N_META,
                  [body[:, j * LANES:(j + 1) * LANES] for j in range(N_CHUNKS)])

    def head(src):
        _store_chunks(out, 0, N_META,
                      [src[:, j * LANES:(j + 1) * LANES] for j in range(N_CHUNKS)])

    @pl.when(i == 0)
    def _():
        head(meta_ref[...])

    @pl.when(i > 0)
    def _():
        head(xa_ref[0])


def _embed(x, meta, *, tt):
    b, s, _ = x.shape
    seq = s + N_META
    assert s % tt == 0 and tt % N_META == 0
    last = s // tt - 1
    metas_per_tile = tt // N_META
    return pl.pallas_call(
        functools.partial(_embed_kernel, tt=tt),
        grid=(b, _cdiv(seq, tt)),
        in_specs=[
            pl.BlockSpec((N_META, D_MODEL), lambda bi, i: (0, 0)),
            pl.BlockSpec((1, N_META, D_MODEL),
                         lambda bi, i: (bi, jnp.maximum(i * metas_per_tile - 1, 0), 0)),
            pl.BlockSpec((1, tt, D_MODEL), lambda bi, i: (bi, jnp.minimum(i, last), 0)),
        ],
        out_specs=pl.BlockSpec((1, tt * SUBLANES, LANES), lambda bi, i: (bi, i, 0)),
        out_shape=jax.ShapeDtypeStruct((b, seq * SUBLANES, LANES), F32),
        compiler_params=_params(40, "parallel", "parallel"),
        name="embed",
    )(meta, x, x)


def _pool_kernel(hp_ref, hm_ref, hn_ref, g_ref, w_ref, sc_ref, o_ref, ext_ref,
                 s2_ref, s4_ref, s8_ref, p_ref, *, seq, tt):
    ti = pl.program_id(1)
    t0 = ti * tt
    hm = hm_ref.at[0]
    out = o_ref.at[0]

    def norm_into_ext(src, n, ext_tok0, tok0):
        tok = _token_ids(tok0, n)
        valid = (tok >= 0) & (tok < seq)
        ch = [jnp.where(valid, c, 0.0) for c in _chunks(src, 0, n)]
        _store_chunks(ext_ref, ext_tok0, n, _rms_chunks(ch, g_ref))

    norm_into_ext(hp_ref.at[0], POOL_HALO, 0, t0 - POOL_HALO)
    norm_into_ext(hm, tt, POOL_HALO, t0)
    norm_into_ext(hn_ref.at[0], POOL_HALO, POOL_HALO + tt, t0 + tt)

    def rows(a, b):
        return pl.ds(a * SUBLANES, (b - a) * SUBLANES)

    lo, hi = 1, tt + 2 * POOL_HALO
    s2_ref[rows(lo, hi), :] = ext_ref[rows(lo - 1, hi - 1), :] + ext_ref[rows(lo, hi), :]
    prev, half = s2_ref, 1
    for nxt in (s4_ref, s8_ref):
        lo, hi = lo + half, hi - half
        nxt[rows(lo, hi), :] = (prev[rows(lo - half, hi - half), :]
                                + prev[rows(lo + half, hi + half), :])
        prev, half = nxt, 2 * half
    m0, m1 = POOL_HALO, POOL_HALO + tt
    assert lo + half <= m0 and m1 <= hi - half and 4 * half == POOL_WINDOWS[-1]

    grp = lax.broadcasted_iota(jnp.int32, (SUBLANES, LANES), 0) // CHUNKS_PER_GROUP

    def window_sums():
        def tiles(v):
            return v.reshape(tt, SUBLANES, LANES)

        s16 = prev[rows(m0 - half, m1 - half), :] + prev[rows(m0 + half, m1 + half), :]
        return jnp.where(grp == 0, tiles(s2_ref[rows(m0, m1), :]),
                         jnp.where(grp == 1, tiles(s4_ref[rows(m0, m1), :]),
                                   jnp.where(grp == 2, tiles(s8_ref[rows(m0, m1), :]),
                                             tiles(s16))))

    near_end = (ti == 0) | (t0 + tt > seq - POOL_HALO)

    @pl.when(jnp.logical_not(near_end))
    def _():
        inv_win = jnp.where(grp == 0, 0.5, jnp.where(grp == 1, 0.25,
                                                     jnp.where(grp == 2, 0.125, 0.0625)))
        p_ref[...] = ((window_sums() * inv_win).reshape(tt * SUBLANES, LANES)
                      - ext_ref[rows(m0, m1), :])

    @pl.when(near_end)
    def _():
        r = lax.broadcasted_iota(jnp.int32, (tt * SUBLANES, 1), 0)
        tok = t0 + (r >> 3)
        left = lax.shift_left(jnp.int32(1), (r & (SUBLANES - 1)) // CHUNKS_PER_GROUP)
        cnt = jnp.minimum(tok + left, seq) - jnp.maximum(tok - left, 0)
        cnt = jnp.maximum(cnt, 1).astype(F32)
        p_ref[...] = (window_sums().reshape(tt * SUBLANES, LANES) / cnt
                      - ext_ref[rows(m0, m1), :])

    for g in range(len(POOL_WINDOWS)):
        group_chunks = range(g * CHUNKS_PER_GROUP, (g + 1) * CHUNKS_PER_GROUP)
        p = jnp.concatenate([p_ref[pl.ds(c, tt, stride=SUBLANES), :] for c in group_chunks],
                            axis=1).astype(BF16)
        y = jnp.dot(p, w_ref[g], preferred_element_type=F32)
        for q, c in enumerate(group_chunks):
            res = hm[pl.ds(c, tt, stride=SUBLANES), :]
            out[pl.ds(c, tt, stride=SUBLANES), :] = (
                res + y[:, q * LANES:(q + 1) * LANES] * _lane_chunk(sc_ref, c))


def _pool_layer(h, g, w, scale, *, seq, tt):
    b = h.shape[0]
    nt = _cdiv(seq, tt)
    halo_rows = POOL_HALO * SUBLANES
    blocks_per_tile = tt // POOL_HALO
    last_halo_block = seq // POOL_HALO - 1
    return pl.pallas_call(
        functools.partial(_pool_kernel, seq=seq, tt=tt),
        grid=(b, nt),
        in_specs=[
            pl.BlockSpec((1, halo_rows, LANES),
                         lambda bi, i: (bi, jnp.maximum(i * blocks_per_tile - 1, 0), 0)),
            pl.BlockSpec((1, tt * SUBLANES, LANES), lambda bi, i: (bi, i, 0)),
            pl.BlockSpec((1, halo_rows, LANES),
                         lambda bi, i: (bi, jnp.minimum((i + 1) * blocks_per_tile,
                                                        last_halo_block), 0)),
            pl.BlockSpec((1, D_MODEL), lambda bi, i: (0, 0)),
            pl.BlockSpec((len(POOL_WINDOWS), POOL_GROUP, POOL_GROUP),
                         lambda bi, i: (0, 0, 0)),
            pl.BlockSpec((1, D_MODEL), lambda bi, i: (0, 0)),
        ],
        out_specs=pl.BlockSpec((1, tt * SUBLANES, LANES), lambda bi, i: (bi, i, 0)),
        out_shape=jax.ShapeDtypeStruct(h.shape, F32),
        scratch_shapes=[pltpu.VMEM(((tt + 2 * POOL_HALO) * SUBLANES, LANES), F32)] * 4
        + [pltpu.VMEM((tt * SUBLANES, LANES), F32)],
        compiler_params=_params(40, "parallel", "arbitrary"),
        name="pool_mixer",
    )(h, h, h, g.reshape(1, D_MODEL), w.astype(BF16), scale.reshape(1, D_MODEL))


def _gelu_tanh(x):
    c = 0.7978845608028654
    return 0.5 * x * (1.0 + jnp.tanh(c * (x + 0.044715 * (x * x * x))))


def _lru_in_kernel(h_ref, g_ref, w_ref, xb_ref, gy_ref, *, tf):
    u = jnp.concatenate(_rms_chunks(_chunks(h_ref, 0, tf), g_ref), axis=1).astype(BF16)
    xb = jnp.dot(u, w_ref[:, :D_MODEL], preferred_element_type=F32)
    _store_chunks(xb_ref, 0, tf,
                  [xb[:, j * LANES:(j + 1) * LANES] for j in range(N_CHUNKS)])
    yb = jnp.dot(u, w_ref[:, D_MODEL:], preferred_element_type=F32)
    gy_ref[...] = _gelu_tanh(yb)


def _lru_in(h_flat, g, w_in, *, tf):
    n_tok = h_flat.shape[0] // SUBLANES
    return pl.pallas_call(
        functools.partial(_lru_in_kernel, tf=tf),
        grid=(_cdiv(n_tok, tf),),
        in_specs=[
            pl.BlockSpec((tf * SUBLANES, LANES), lambda i: (i, 0)),
            pl.BlockSpec((1, D_MODEL), lambda i: (0, 0)),
            pl.BlockSpec((D_MODEL, 2 * D_MODEL), lambda i: (0, 0)),
        ],
        out_specs=[
            pl.BlockSpec((tf * SUBLANES, LANES), lambda i: (i, 0)),
            pl.BlockSpec((tf, D_MODEL), lambda i: (i, 0)),
        ],
        out_shape=[
            jax.ShapeDtypeStruct(h_flat.shape, F32),
            jax.ShapeDtypeStruct((n_tok, D_MODEL), F32),
        ],
        compiler_params=_params(48, "parallel"),
        name="lru_in_proj",
    )(h_flat, g.reshape(1, D_MODEL), w_in.astype(BF16))


def _softplus(x):
    return jnp.maximum(x, 0.0) + jnp.log1p(jnp.exp(-jnp.abs(x)))


def _lru_scan_kernel(*refs, seq, tt, reverse):
    if reverse:
        (xp_ref, xm_ref, xn_ref, cw_ref, cb_ref, wg_ref, bg_ref, lam_ref,
         o_ref, ext_ref, xc_ref, a_ref, b_ref, carry_ref) = refs
        hs = o_ref.at[0]
    else:
        (xp_ref, xm_ref, xn_ref, cw_ref, cb_ref, wg_ref, bg_ref, lam_ref,
         hb_ref, gy_ref, res_ref, wo_ref,
         o_ref, ext_ref, xc_ref, a_ref, b_ref, carry_ref, hs) = refs
    step = pl.program_id(1)
    nt = pl.num_programs(1)
    ti = nt - 1 - step if reverse else step
    t0 = ti * tt

    @pl.when(step == 0)
    def _():
        carry_ref[...] = jnp.zeros_like(carry_ref)

    def masked_rows(src, n_tok, tok0):
        rows = n_tok * SUBLANES
        tok = tok0 + (lax.broadcasted_iota(jnp.int32, (rows, 1), 0) >> 3)
        return jnp.where((tok >= 0) & (tok < seq), src[...], 0.0)

    main_rows = pl.ds(CONV_LEFT * SUBLANES, tt * SUBLANES)
    ext_ref[pl.ds(0, CONV_LEFT * SUBLANES), :] = masked_rows(
        xp_ref.at[0], CONV_LEFT, t0 - CONV_LEFT)
    ext_ref[main_rows, :] = xm_ref[0]
    ext_ref[pl.ds((CONV_LEFT + tt) * SUBLANES, CONV_RIGHT * SUBLANES), :] = masked_rows(
        xn_ref.at[0], CONV_RIGHT, t0 + tt)
    ragged = t0 + tt > seq

    @pl.when(ragged)
    def _():
        ext_ref[main_rows, :] = masked_rows(xm_ref.at[0], tt, t0)

    xc = cb_ref[...]
    for k in range(CONV_WIDTH):
        xc = xc + (ext_ref[pl.ds(k * SUBLANES, tt * SUBLANES), :]
                   .reshape(tt, SUBLANES, LANES) * cw_ref[k])
    xc_ref[...] = xc.reshape(tt * SUBLANES, LANES)

    for hh in range(LRU_HEADS):
        head_chunks = range(hh * CHUNKS_PER_HEAD, (hh + 1) * CHUNKS_PER_HEAD)
        xh = jnp.concatenate([xc_ref[pl.ds(c, tt, stride=SUBLANES), :] for c in head_chunks],
                             axis=1)
        th = jnp.tanh(jnp.dot(xh.astype(BF16), wg_ref[hh], preferred_element_type=F32)
                      + bg_ref[hh])
        lam = lam_ref[:, hh * LRU_HEAD_DIM:(hh + 1) * LRU_HEAD_DIM]
        half_c = (-0.5 * LRU_C) * _softplus(-lam)
        log_a = th[:, :LRU_HEAD_DIM] * half_c + half_c
        a = jnp.exp(log_a)
        bb = (jnp.sqrt(jnp.tanh(-log_a) * (a * a + 1.0))
              * ((th[:, LRU_HEAD_DIM:] + 1.0) * xh))
        for q, c in enumerate(head_chunks):
            a_ref[pl.ds(c, tt, stride=SUBLANES), :] = a[:, q * LANES:(q + 1) * LANES]
            b_ref[pl.ds(c, tt, stride=SUBLANES), :] = bb[:, q * LANES:(q + 1) * LANES]

    @pl.when(ragged)
    def _():
        tok = t0 + (lax.broadcasted_iota(jnp.int32, (tt * SUBLANES, 1), 0) >> 3)
        b_ref[...] = jnp.where(tok < seq, b_ref[...], 0.0)

    def scan_body(i, h):
        for k in range(SUBLANES):
            t = (tt - 1 - (i * SUBLANES + k)) if reverse else (i * SUBLANES + k)
            r0 = pl.multiple_of(t * SUBLANES, SUBLANES)
            h = a_ref[pl.ds(r0, SUBLANES), :] * h + b_ref[pl.ds(r0, SUBLANES), :]
            hs[pl.ds(r0, SUBLANES), :] = h
        return h

    carry_ref[...] = lax.fori_loop(0, tt // SUBLANES, scan_body, carry_ref[...])

    if not reverse:
        hs[...] = hs[...] + hb_ref[0]
        hsum = jnp.concatenate(_chunks(hs, 0, tt), axis=1)
        z = (hsum * gy_ref[0]).astype(BF16)
        y = jnp.dot(z, wo_ref[...], preferred_element_type=F32)
        res = res_ref.at[0]
        out = o_ref.at[0]
        for j in range(N_CHUNKS):
            out[pl.ds(j, tt, stride=SUBLANES), :] = (
                res[pl.ds(j, tt, stride=SUBLANES), :] + y[:, j * LANES:(j + 1) * LANES])


def _lru_scan(xb, conv_w, conv_b, wg, bg, lam, *, seq, tt, reverse,
              hb=None, gy=None, res=None, w_out=None):
    b = xb.shape[0]
    nt = _cdiv(seq, tt)
    right_rows = CONV_RIGHT * SUBLANES
    last_right_block = seq // CONV_RIGHT - 1

    def tix(i):
        return nt - 1 - i if reverse else i

    tile_spec = pl.BlockSpec((1, tt * SUBLANES, LANES), lambda bi, i: (bi, tix(i), 0))

    def const_spec(shape):
        return pl.BlockSpec(shape, lambda bi, i: (0,) * len(shape))

    in_specs = [
        pl.BlockSpec((1, CONV_LEFT * SUBLANES, LANES),
                     lambda bi, i: (bi, jnp.maximum(tix(i) * (tt // CONV_LEFT) - 1, 0), 0)),
        tile_spec,
        pl.BlockSpec((1, right_rows, LANES),
                     lambda bi, i: (bi, jnp.minimum((tix(i) + 1) * (tt // CONV_RIGHT),
                                                    last_right_block), 0)),
        const_spec((CONV_WIDTH, SUBLANES, LANES)),
        const_spec((SUBLANES, LANES)),
        const_spec((LRU_HEADS, LRU_HEAD_DIM, 2 * LRU_HEAD_DIM)),
        const_spec((LRU_HEADS, 1, 2 * LRU_HEAD_DIM)),
        const_spec((1, D_MODEL)),
    ]
    args = [xb, xb, xb, (0.5 * conv_w).reshape(CONV_WIDTH, SUBLANES, LANES),
            (0.5 * conv_b).reshape(SUBLANES, LANES), wg, bg, lam.reshape(1, D_MODEL)]
    scratch = [
        pltpu.VMEM(((tt + CONV_WIDTH - 1) * SUBLANES, LANES), F32),
        pltpu.VMEM((tt * SUBLANES, LANES), F32),
        pltpu.VMEM((tt * SUBLANES, LANES), F32),
        pltpu.VMEM((tt * SUBLANES, LANES), F32),
        pltpu.VMEM((SUBLANES, LANES), F32),
    ]
    if not reverse:
        in_specs += [
            tile_spec,
            pl.BlockSpec((1, tt, D_MODEL), lambda bi, i: (bi, i, 0)),
            tile_spec,
            const_spec((D_MODEL, D_MODEL)),
        ]
        args += [hb, gy, res, w_out]
        scratch.append(pltpu.VMEM((tt * SUBLANES, LANES), F32))
    return pl.pallas_call(
        functools.partial(_lru_scan_kernel, seq=seq, tt=tt, reverse=reverse),
        grid=(b, nt),
        in_specs=in_specs,
        out_specs=tile_spec,
        out_shape=jax.ShapeDtypeStruct(xb.shape, F32),
        scratch_shapes=scratch,
        compiler_params=_params(48, "parallel", "arbitrary"),
        name="lru_scan_rev" if reverse else "lru_scan_fwd_out",
    )(*args)


def _lru_layer(h, g, w_in, conv_w, conv_b, w_gates, b_gates, lam, w_out, *, seq, tt):
    b = h.shape[0]
    xb, gy = _lru_in(h.reshape(b * seq * SUBLANES, LANES), g, w_in, tf=tt)
    xb = xb.reshape(h.shape)
    gy = gy.reshape(b, seq, D_MODEL)
    wg = jnp.concatenate([w_gates[:, 0], w_gates[:, 1]], axis=-1).astype(BF16)
    bg = 0.5 * jnp.concatenate([b_gates[:, 0], b_gates[:, 1]], axis=-1)[:, :, None, :]
    scan = functools.partial(_lru_scan, xb, conv_w, conv_b, seq=seq, tt=tt)
    hb = scan(wg[1], bg[1], lam[1], reverse=True)
    return scan(wg[0], bg[0], lam[0], reverse=False,
                hb=hb, gy=gy, res=h, w_out=w_out.astype(BF16))


def _split_bf16(x):
    hi = x.astype(BF16)
    return hi, (x - hi.astype(F32)).astype(BF16)


def _route_kernel(h_ref, g_ref, rw_ref, aff_ref, *, seq, tt):
    valid = _token_ids(pl.program_id(1) * tt, tt) < seq
    ch = [jnp.where(valid, c, 0.0) for c in _chunks(h_ref.at[0], 0, tt)]
    u = jnp.concatenate(_rms_chunks(ch, g_ref), axis=1)
    u_hi, u_lo = _split_bf16(u)
    p = jnp.dot(u_hi, rw_ref[...], preferred_element_type=F32)
    q = jnp.dot(u_lo, rw_ref[:, :LANES], preferred_element_type=F32)
    logits = ((p[:, :LANES] + p[:, LANES:]) + q).T[:N_EXPERTS, :]
    m = jnp.max(logits, axis=0, keepdims=True)
    ex = jnp.exp(logits - m)
    aff_ref[0] = ex / jnp.sum(ex, axis=0, keepdims=True)


def _route(h, g, router_w, *, seq, tt, seq_pad):
    b = h.shape[0]
    w_hi, w_lo = _split_bf16(jnp.pad(router_w, ((0, 0), (0, LANES - N_EXPERTS))))
    return pl.pallas_call(
        functools.partial(_route_kernel, seq=seq, tt=tt),
        grid=(b, _cdiv(seq, tt)),
        in_specs=[
            pl.BlockSpec((1, tt * SUBLANES, LANES), lambda bi, i: (bi, i, 0)),
            pl.BlockSpec((1, D_MODEL), lambda bi, i: (0, 0)),
            pl.BlockSpec((D_MODEL, 2 * LANES), lambda bi, i: (0, 0)),
        ],
        out_specs=pl.BlockSpec((1, N_EXPERTS, tt), lambda bi, i: (bi, 0, i)),
        out_shape=jax.ShapeDtypeStruct((b, N_EXPERTS, seq_pad), F32),
        compiler_params=_params(40, "parallel", "parallel"),
        name="moe_route",
    )(h, g.reshape(1, D_MODEL), jnp.concatenate([w_hi, w_lo], axis=1))


def _cumsum_blocks(x01):
    r = lax.broadcasted_iota(jnp.int32, (LANES, LANES), 0)
    c = lax.broadcasted_iota(jnp.int32, (LANES, LANES), 1)
    tri = (r <= c).astype(BF16)
    local = [jnp.dot(x01[:, j * LANES:(j + 1) * LANES].astype(BF16), tri,
                     preferred_element_type=F32) for j in range(x01.shape[1] // LANES)]
    off = jnp.zeros((x01.shape[0], 1), F32)
    outs = []
    for cs in local:
        outs.append(cs + off)
        off = off + cs[:, LANES - 1:LANES]
    return outs


def _topk_kernel(aff_ref, idx_ref, cs_ref, *, seq, cap, cap_pad, slots_pad):
    aff = aff_ref[0]
    lane = lax.broadcasted_iota(jnp.int32, aff.shape, 1)
    aff = jnp.where(lane < seq, aff, -1.0)
    capf = float(cap)
    n_blk = aff.shape[1] // LANES

    def bit_step(i, cur):
        cand = cur | lax.shift_left(jnp.int32(1), 30 - i)
        candf = lax.bitcast_convert_type(cand, F32)
        cnt = jnp.sum((aff >= candf).astype(F32), axis=1, keepdims=True)
        return jnp.where(cnt >= capf, cand, cur)

    cur = lax.fori_loop(0, 31, bit_step, jnp.zeros((N_EXPERTS, 1), jnp.int32))
    thr = lax.bitcast_convert_type(cur, F32)
    gt = aff > thr
    eq = aff == thr
    need = capf - jnp.sum(gt.astype(F32), axis=1, keepdims=True)
    eq_cnt = jnp.concatenate(_cumsum_blocks(eq.astype(F32)), axis=1)
    sel = gt | (eq & (eq_cnt <= need))

    blocks = _cumsum_blocks(sel.astype(F32))
    for j, cs in enumerate(blocks):
        cs_ref[pl.ds(j * N_EXPERTS, N_EXPERTS), :] = cs
    cs_ref[pl.ds(n_blk * N_EXPERTS, (LANES - n_blk) * N_EXPERTS), :] = jnp.zeros(
        ((LANES - n_blk) * N_EXPERTS, LANES), F32)
    never = jnp.full((N_EXPERTS, LANES - n_blk), 2.0 * COUNT_RADIX * 256, F32)
    blk_end = jnp.concatenate([cs[:, LANES - 1:LANES] for cs in blocks] + [never], axis=1)

    slot = lax.broadcasted_iota(jnp.int32, (cap_pad, 1), 0).astype(F32)
    lane_s = lax.broadcasted_iota(jnp.int32, (cap_pad, LANES), 1).astype(F32)
    lane_e = lax.broadcasted_iota(jnp.int32, (slots_pad, LANES), 1)
    experts = range(N_EXPERTS)
    digits = []
    for e in experts:
        cmat = cs_ref[pl.ds(e, LANES, stride=N_EXPERTS), :]
        hi = jnp.floor(cmat * (1.0 / COUNT_RADIX))
        digits.append(jnp.concatenate([hi, cmat - COUNT_RADIX * hi], axis=1).astype(BF16))
    ones = jnp.ones((LANES, LANES), BF16)

    def count_le(x):
        return jnp.dot((x <= slot).astype(BF16), ones, preferred_element_type=F32)

    full = [count_le(blk_end[e:e + 1, :]) for e in experts]
    pick = [(lane_s == full[e]).astype(BF16) for e in experts]
    rows = [jnp.dot(pick[e], digits[e], preferred_element_type=F32) for e in experts]
    rows = [COUNT_RADIX * r_[:, :LANES] + r_[:, LANES:] for r_ in rows]
    inside = [count_le(rows[e]) for e in experts]
    acc = jnp.zeros((slots_pad, LANES), F32)
    pad_rows = jnp.zeros((slots_pad - cap_pad, LANES), F32)
    for e in experts:
        col = jnp.concatenate([LANES * full[e] + inside[e], pad_rows], axis=0)
        acc = jnp.where(lane_e == e, col, acc)
    idx = acc.T[:N_EXPERTS, :].astype(jnp.int32)
    slot_l = lax.broadcasted_iota(jnp.int32, idx.shape, 1)
    idx_ref[0] = jnp.where(slot_l < cap, idx, 0)


def _topk(aff_t, *, seq, cap, cap_pad, slots_pad):
    b, _, seq_pad = aff_t.shape
    assert seq_pad // LANES <= LANES and cap < COUNT_RADIX * 256
    return pl.pallas_call(
        functools.partial(_topk_kernel, seq=seq, cap=cap, cap_pad=cap_pad,
                          slots_pad=slots_pad),
        grid=(b,),
        in_specs=[pl.BlockSpec((1, N_EXPERTS, seq_pad), lambda bi: (bi, 0, 0))],
        out_specs=pl.BlockSpec((1, N_EXPERTS, slots_pad), lambda bi: (bi, 0, 0)),
        out_shape=jax.ShapeDtypeStruct((b, N_EXPERTS, slots_pad), jnp.int32),
        scratch_shapes=[pltpu.VMEM((LANES * N_EXPERTS, LANES), F32)],
        compiler_params=_params(40, "parallel"),
        name="moe_topk",
    )(aff_t)


def _gather_kernel(idx_ref, h_ref, g_ref, xs_ref, zx_ref, *, cap_pad):
    h = h_ref.at[0]

    def body(i, carry):
        for k in range(SUBLANES):
            s = i * SUBLANES + k
            t = idx_ref[0, 0, s]
            zx_ref[pl.ds(pl.multiple_of(s * SUBLANES, SUBLANES), SUBLANES), :] = (
                h[pl.ds(pl.multiple_of(t * SUBLANES, SUBLANES), SUBLANES), :])
        return carry

    lax.fori_loop(0, cap_pad // SUBLANES, body, 0)
    u = _rms_chunks(_chunks(zx_ref, 0, cap_pad), g_ref)
    xs_ref[0, 0] = jnp.concatenate(u, axis=1).astype(BF16)


def _gather(idx_smem, h, g, *, cap_pad):
    b = h.shape[0]
    slots_pad = idx_smem.shape[-1]
    return pl.pallas_call(
        functools.partial(_gather_kernel, cap_pad=cap_pad),
        grid=(b, N_EXPERTS),
        in_specs=[
            pl.BlockSpec((1, 1, slots_pad), lambda bi, e: (bi * N_EXPERTS + e, 0, 0),
                         memory_space=pltpu.SMEM),
            pl.BlockSpec((1,) + h.shape[1:], lambda bi, e: (bi, 0, 0)),
            pl.BlockSpec((1, D_MODEL), lambda bi, e: (0, 0)),
        ],
        out_specs=pl.BlockSpec((1, 1, cap_pad, D_MODEL), lambda bi, e: (e, bi, 0, 0)),
        out_shape=jax.ShapeDtypeStruct((N_EXPERTS, b, cap_pad, D_MODEL), BF16),
        scratch_shapes=[pltpu.VMEM((cap_pad * SUBLANES, LANES), F32)],
        compiler_params=_params(56, "parallel", "arbitrary"),
        name="moe_gather",
    )(idx_smem, h, g.reshape(1, D_MODEL))


def _ffn_kernel(x_ref, wg_ref, wu_ref, wd_ref, o_ref, hid_ref, wdb_ref, *, ft, n_f):
    r = pl.program_id(1)
    f = pl.program_id(2)

    @pl.when(r == 0)
    def _():
        wdb_ref[pl.ds(pl.multiple_of(f * ft, ft), ft), :] = wd_ref[0, 0].astype(BF16)

    x = x_ref[0]
    hg = jnp.dot(x, wg_ref[0, 0].astype(BF16), preferred_element_type=F32)
    hu = jnp.dot(x, wu_ref[0, 0].astype(BF16), preferred_element_type=F32)
    hid_ref[f] = (hg * jax.nn.sigmoid(hg) * hu).astype(BF16)

    @pl.when(f == n_f - 1)
    def _():
        hid = jnp.concatenate([hid_ref[k] for k in range(n_f)], axis=1)
        y = jnp.dot(hid, wdb_ref[...], preferred_element_type=F32)
        _store_chunks(o_ref.at[0], 0, y.shape[0],
                      [y[:, j * LANES:(j + 1) * LANES] for j in range(N_CHUNKS)])


def _ffn(xs, w_gate, w_up, w_down, layer):
    n_exp, rows, _ = xs.shape
    d_expert = w_gate.shape[-1]
    rt = rows // FFN_ROW_TILES
    ft = min(FFN_F_TILE, d_expert)
    n_f = d_expert // ft

    def wd_tile(e, r, f):
        return (layer, e, jnp.where(r == 0, f, n_f - 1), 0)

    return pl.pallas_call(
        functools.partial(_ffn_kernel, ft=ft, n_f=n_f),
        grid=(n_exp, FFN_ROW_TILES, n_f),
        in_specs=[
            pl.BlockSpec((1, rt, D_MODEL), lambda e, r, f: (e, r, 0)),
            pl.BlockSpec((1, 1, D_MODEL, ft), lambda e, r, f: (layer, e, 0, f)),
            pl.BlockSpec((1, 1, D_MODEL, ft), lambda e, r, f: (layer, e, 0, f)),
            pl.BlockSpec((1, 1, ft, D_MODEL), wd_tile),
        ],
        out_specs=pl.BlockSpec((1, rt * SUBLANES, LANES), lambda e, r, f: (e, r, 0)),
        out_shape=jax.ShapeDtypeStruct((n_exp, rows * SUBLANES, LANES), F32),
        scratch_shapes=[pltpu.VMEM((n_f, rt, ft), BF16),
                        pltpu.VMEM((d_expert, D_MODEL), BF16)],
        compiler_params=_params(60, "parallel", "arbitrary", "arbitrary"),
        name="moe_ffn",
    )(xs, w_gate, w_up, w_down)


def _scatter_kernel(idx_ref, aff_ref, ys_ref, h_ref, o_ref, *, cap):
    @pl.when(pl.program_id(1) == 0)
    def _():
        o_ref[...] = h_ref[...]

    ys = ys_ref.at[0, 0]
    out = o_ref.at[0]

    def add_rows(slots):
        rows, vals = [], []
        for s in slots:
            t = idx_ref[0, 0, s]
            gate = aff_ref[0, 0, t]
            r0 = pl.multiple_of(t * SUBLANES, SUBLANES)
            s0 = s * SUBLANES if isinstance(s, int) else pl.multiple_of(s * SUBLANES, SUBLANES)
            vals.append(out[pl.ds(r0, SUBLANES), :] + gate * ys[pl.ds(s0, SUBLANES), :])
            rows.append(r0)
        for r0, v in zip(rows, vals):
            out[pl.ds(r0, SUBLANES), :] = v

    def body(i, carry):
        add_rows([i * SCATTER_GROUP + k for k in range(SCATTER_GROUP)])
        return carry

    n_groups = cap // SCATTER_GROUP
    lax.fori_loop(0, n_groups, body, 0)
    if cap % SCATTER_GROUP:
        add_rows(list(range(n_groups * SCATTER_GROUP, cap)))


def _scatter(idx_smem, aff_smem, ys, h, *, cap, cap_pad):
    b = h.shape[0]
    slots_pad = idx_smem.shape[-1]
    seq_pad = aff_smem.shape[-1]
    resident = pl.BlockSpec((1,) + h.shape[1:], lambda bi, e: (bi, 0, 0),
                            pipeline_mode=pl.Buffered(1))
    return pl.pallas_call(
        functools.partial(_scatter_kernel, cap=cap),
        grid=(b, N_EXPERTS),
        in_specs=[
            pl.BlockSpec((1, 1, slots_pad), lambda bi, e: (bi * N_EXPERTS + e, 0, 0),
                         memory_space=pltpu.SMEM),
            pl.BlockSpec((1, 1, seq_pad), lambda bi, e: (bi * N_EXPERTS + e, 0, 0),
                         memory_space=pltpu.SMEM),
            pl.BlockSpec((1, 1, cap_pad * SUBLANES, LANES), lambda bi, e: (e, bi, 0, 0)),
            resident,
        ],
        out_specs=resident,
        out_shape=jax.ShapeDtypeStruct(h.shape, F32),
        compiler_params=_params(56, "parallel", "arbitrary"),
        name="moe_scatter",
    )(idx_smem, aff_smem, ys, h)


def _moe_layer(h, g, router_w, w_gate, w_up, w_down, layer, *, seq, tt):
    b = h.shape[0]
    cap = CAPACITY_FACTOR * seq // N_EXPERTS
    cap_pad = _round_up(cap, BF16_ROWS)
    assert (b * cap_pad) % (FFN_ROW_TILES * BF16_ROWS) == 0
    slots_pad = _round_up(cap_pad, LANES)
    seq_pad = _round_up(seq, LANES)
    aff_t = _route(h, g, router_w, seq=seq, tt=tt, seq_pad=seq_pad)
    idx = _topk(aff_t, seq=seq, cap=cap, cap_pad=cap_pad, slots_pad=slots_pad)
    idx_smem = idx.reshape(b * N_EXPERTS, 1, slots_pad)
    aff_smem = aff_t.reshape(b * N_EXPERTS, 1, seq_pad)
    xs = _gather(idx_smem, h, g, cap_pad=cap_pad)
    ys = _ffn(xs.reshape(N_EXPERTS, b * cap_pad, D_MODEL), w_gate, w_up, w_down, layer)
    ys = ys.reshape(N_EXPERTS, b, cap_pad * SUBLANES, LANES)
    return _scatter(idx_smem, aff_smem, ys, h, cap=cap, cap_pad=cap_pad)


def _final_kernel(ha_ref, hb_ref, g_ref, o_ref, *, tt):
    ha = ha_ref.at[0]
    hb = hb_ref.at[0]
    ch = [jnp.concatenate(
        [ha[pl.ds(N_META * SUBLANES + j, tt - N_META, stride=SUBLANES), :],
         hb[pl.ds(j, N_META, stride=SUBLANES), :]], axis=0) for j in range(N_CHUNKS)]
    o_ref[0] = jnp.concatenate(_rms_chunks(ch, g_ref), axis=1)


def _final(h, g, *, seq, tt):
    b = h.shape[0]
    s_out = seq - N_META
    assert s_out % tt == 0
    assert tt % N_META == 0
    return pl.pallas_call(
        functools.partial(_final_kernel, tt=tt),
        grid=(b, s_out // tt),
        in_specs=[
            pl.BlockSpec((1, tt * SUBLANES, LANES), lambda bi, i: (bi, i, 0)),
            pl.BlockSpec((1, N_META * SUBLANES, LANES),
                         lambda bi, i: (bi, (i + 1) * (tt // N_META), 0)),
            pl.BlockSpec((1, D_MODEL), lambda bi, i: (0, 0)),
        ],
        out_specs=pl.BlockSpec((1, tt, D_MODEL), lambda bi, i: (bi, i, 0)),
        out_shape=jax.ShapeDtypeStruct((b, s_out, D_MODEL), F32),
        compiler_params=_params(40, "parallel", "parallel"),
        name="final_norm",
    )(h, h, g.reshape(1, D_MODEL))


def kernel(x, meta_tokens, norm_mix, norm_ffn, norm_final, pool_w, pool_scale, lru_w_in, lru_conv_w, lru_conv_b, lru_w_gates, lru_b_gates, lru_lambda, lru_w_out, router_w, moe_w_gate, moe_w_up, moe_w_down):
    b, s, d = x.shape
    assert d == D_MODEL
    seq = s + N_META
    tt = TIME_TILE
    depth = norm_mix.shape[0]
    h = _embed(x, meta_tokens.astype(x.dtype), tt=tt)
    for i in range(depth):
        j = i // 2
        if i % 2 == 0:
            h = _pool_layer(h, norm_mix[i], pool_w[j], pool_scale[j], seq=seq, tt=tt)
        else:
            h = _lru_layer(h, norm_mix[i], lru_w_in[j], lru_conv_w[j], lru_conv_b[j],
                           lru_w_gates[j], lru_b_gates[j], lru_lambda[j], lru_w_out[j],
                           seq=seq, tt=tt)
        h = _moe_layer(h, norm_ffn[i], router_w[i], moe_w_gate, moe_w_up, moe_w_down, i,
                       seq=seq, tt=tt)
    return _final(h, norm_final, seq=seq, tt=tt)
```

```python
import functools

import jax
import jax.numpy as jnp
from jax import lax
from jax.experimental import pallas as pl
from jax.experimental.pallas import tpu as pltpu

F32 = jnp.float32
BF16 = jnp.bfloat16

LANES = 128
SUBLANES = 8
BF16_ROWS = 16
D_MODEL = 1024
N_CHUNKS = D_MODEL // LANES
assert N_CHUNKS == SUBLANES
N_META = 16
POOL_WINDOWS = (2, 4, 8, 16)
POOL_GROUP = D_MODEL // len(POOL_WINDOWS)
CHUNKS_PER_GROUP = POOL_GROUP // LANES
POOL_HALO = 8
LRU_HEADS = 4
LRU_HEAD_DIM = D_MODEL // LRU_HEADS
CHUNKS_PER_HEAD = LRU_HEAD_DIM // LANES
LRU_C = 8.0
CONV_WIDTH = 4
CONV_LEFT = 1
CONV_RIGHT = CONV_WIDTH - 1 - CONV_LEFT
N_EXPERTS = 16
CAPACITY_FACTOR = 2
RMS_EPS = 1e-6
TIME_TILE = 512
FFN_ROW_TILES = 4
FFN_F_TILE = 1024
LRU_IN_TILE = 1024
SCATTER_PREFETCH_STEP = 3
SCATTER_GROUP = 8
COUNT_RADIX = 32.0
MIB = 1024 * 1024


def _cdiv(a, b):
    return -(-a // b)


def _round_up(a, b):
    return _cdiv(a, b) * b


def _chunks(ref2d, tok0, n):
    return [ref2d[pl.ds(tok0 * SUBLANES + j, n, stride=SUBLANES), :]
            for j in range(N_CHUNKS)]


def _store_chunks(ref2d, tok0, n, chunks):
    for j in range(N_CHUNKS):
        ref2d[pl.ds(tok0 * SUBLANES + j, n, stride=SUBLANES), :] = chunks[j]


def _lane_chunk(ref, j):
    return ref[:, j * LANES:(j + 1) * LANES]


def _rms_chunks(chunks, g_ref):
    ss = chunks[0] * chunks[0]
    for c in chunks[1:]:
        ss = ss + c * c
    ms = jnp.sum(ss, axis=-1, keepdims=True) * (1.0 / D_MODEL)
    inv = lax.rsqrt(ms + RMS_EPS)
    return [c * inv * _lane_chunk(g_ref, j) for j, c in enumerate(chunks)]


def _token_ids(tok0, n):
    return tok0 + lax.broadcasted_iota(jnp.int32, (n, 1), 0)


def _params(vmem_mib, *sem):
    return pltpu.CompilerParams(dimension_semantics=sem,
                                vmem_limit_bytes=vmem_mib * MIB)


def _embed_kernel(meta_ref, xa_ref, xb_ref, o_ref, *, tt):
    i = pl.program_id(1)
    out = o_ref.at[0]
    body = xb_ref[0, :tt - N_META, :]
    _store_chunks(out, N_META, tt - N_META,
                  [body[:, j * LANES:(j + 1) * LANES] for j in range(N_CHUNKS)])

    def head(src):
        _store_chunks(out, 0, N_META,
                      [src[:, j * LANES:(j + 1) * LANES] for j in range(N_CHUNKS)])

    @pl.when(i == 0)
    def _():
        head(meta_ref[...])

    @pl.when(i > 0)
    def _():
        head(xa_ref[0])


def _embed(x, meta, *, tt):
    b, s, _ = x.shape
    seq = s + N_META
    assert s % tt == 0 and tt % N_META == 0
    last = s // tt - 1
    metas_per_tile = tt // N_META
    return pl.pallas_call(
        functools.partial(_embed_kernel, tt=tt),
        grid=(b, _cdiv(seq, tt)),
        in_specs=[
            pl.BlockSpec((N_META, D_MODEL), lambda bi, i: (0, 0)),
            pl.BlockSpec((1, N_META, D_MODEL),
                         lambda bi, i: (bi, jnp.maximum(i * metas_per_tile - 1, 0), 0)),
            pl.BlockSpec((1, tt, D_MODEL), lambda bi, i: (bi, jnp.minimum(i, last), 0)),
        ],
        out_specs=pl.BlockSpec((1, tt * SUBLANES, LANES), lambda bi, i: (bi, i, 0)),
        out_shape=jax.ShapeDtypeStruct((b, seq * SUBLANES, LANES), F32),
        compiler_params=_params(40, "parallel", "parallel"),
        name="embed",
    )(meta, x, x)


def _pool_kernel(hp_ref, hm_ref, hn_ref, g_ref, w_ref, sc_ref, o_ref, ext_ref,
                 s2_ref, s4_ref, s8_ref, p_ref, *, seq, tt):
    ti = pl.program_id(1)
    t0 = ti * tt
    hm = hm_ref.at[0]
    out = o_ref.at[0]

    def norm_into_ext(src, n, ext_tok0, tok0):
        tok = _token_ids(tok0, n)
        valid = (tok >= 0) & (tok < seq)
        ch = [jnp.where(valid, c, 0.0) for c in _chunks(src, 0, n)]
        _store_chunks(ext_ref, ext_tok0, n, _rms_chunks(ch, g_ref))

    norm_into_ext(hp_ref.at[0], POOL_HALO, 0, t0 - POOL_HALO)
    norm_into_ext(hm, tt, POOL_HALO, t0)
    norm_into_ext(hn_ref.at[0], POOL_HALO, POOL_HALO + tt, t0 + tt)

    def rows(a, b):
        return pl.ds(a * SUBLANES, (b - a) * SUBLANES)

    lo, hi = 1, tt + 2 * POOL_HALO
    s2_ref[rows(lo, hi), :] = ext_ref[rows(lo - 1, hi - 1), :] + ext_ref[rows(lo, hi), :]
    prev, half = s2_ref, 1
    for nxt in (s4_ref, s8_ref):
        lo, hi = lo + half, hi - half
        nxt[rows(lo, hi), :] = (prev[rows(lo - half, hi - half), :]
                                + prev[rows(lo + half, hi + half), :])
        prev, half = nxt, 2 * half
    m0, m1 = POOL_HALO, POOL_HALO + tt
    assert lo + half <= m0 and m1 <= hi - half and 4 * half == POOL_WINDOWS[-1]

    grp = lax.broadcasted_iota(jnp.int32, (SUBLANES, LANES), 0) // CHUNKS_PER_GROUP

    def window_sums():
        def tiles(v):
            return v.reshape(tt, SUBLANES, LANES)

        s16 = prev[rows(m0 - half, m1 - half), :] + prev[rows(m0 + half, m1 + half), :]
        return jnp.where(grp == 0, tiles(s2_ref[rows(m0, m1), :]),
                         jnp.where(grp == 1, tiles(s4_ref[rows(m0, m1), :]),
                                   jnp.where(grp == 2, tiles(s8_ref[rows(m0, m1), :]),
                                             tiles(s16))))

    near_end = (ti == 0) | (t0 + tt > seq - POOL_HALO)

    @pl.when(jnp.logical_not(near_end))
    def _():
        inv_win = jnp.where(grp == 0, 0.5, jnp.where(grp == 1, 0.25,
                                                     jnp.where(grp == 2, 0.125, 0.0625)))
        p_ref[...] = ((window_sums() * inv_win).reshape(tt * SUBLANES, LANES)
                      - ext_ref[rows(m0, m1), :])

    @pl.when(near_end)
    def _():
        r = lax.broadcasted_iota(jnp.int32, (tt * SUBLANES, 1), 0)
        tok = t0 + (r >> 3)
        left = lax.shift_left(jnp.int32(1), (r & (SUBLANES - 1)) // CHUNKS_PER_GROUP)
        cnt = jnp.minimum(tok + left, seq) - jnp.maximum(tok - left, 0)
        cnt = jnp.maximum(cnt, 1).astype(F32)
        p_ref[...] = (window_sums().reshape(tt * SUBLANES, LANES) / cnt
                      - ext_ref[rows(m0, m1), :])

    for g in range(len(POOL_WINDOWS)):
        group_chunks = range(g * CHUNKS_PER_GROUP, (g + 1) * CHUNKS_PER_GROUP)
        p = jnp.concatenate([p_ref[pl.ds(c, tt, stride=SUBLANES), :] for c in group_chunks],
                            axis=1).astype(BF16)
        y = jnp.dot(p, w_ref[g], preferred_element_type=F32)
        for q, c in enumerate(group_chunks):
            res = hm[pl.ds(c, tt, stride=SUBLANES), :]
            out[pl.ds(c, tt, stride=SUBLANES), :] = (
                res + y[:, q * LANES:(q + 1) * LANES] * _lane_chunk(sc_ref, c))


def _pool_layer(h, g, w, scale, *, seq, tt):
    b = h.shape[0]
    nt = _cdiv(seq, tt)
    halo_rows = POOL_HALO * SUBLANES
    blocks_per_tile = tt // POOL_HALO
    last_halo_block = seq // POOL_HALO - 1
    return pl.pallas_call(
        functools.partial(_pool_kernel, seq=seq, tt=tt),
        grid=(b, nt),
        in_specs=[
            pl.BlockSpec((1, halo_rows, LANES),
                         lambda bi, i: (bi, jnp.maximum(i * blocks_per_tile - 1, 0), 0)),
            pl.BlockSpec((1, tt * SUBLANES, LANES), lambda bi, i: (bi, i, 0)),
            pl.BlockSpec((1, halo_rows, LANES),
                         lambda bi, i: (bi, jnp.minimum((i + 1) * blocks_per_tile,
                                                        last_halo_block), 0)),
            pl.BlockSpec((1, D_MODEL), lambda bi, i: (0, 0)),
            pl.BlockSpec((len(POOL_WINDOWS), POOL_GROUP, POOL_GROUP),
                         lambda bi, i: (0, 0, 0)),
            pl.BlockSpec((1, D_MODEL), lambda bi, i: (0, 0)),
        ],
        out_specs=pl.BlockSpec((1, tt * SUBLANES, LANES), lambda bi, i: (bi, i, 0)),
        out_shape=jax.ShapeDtypeStruct(h.shape, F32),
        scratch_shapes=[pltpu.VMEM(((tt + 2 * POOL_HALO) * SUBLANES, LANES), F32)] * 4
        + [pltpu.VMEM((tt * SUBLANES, LANES), F32)],
        compiler_params=_params(40, "parallel", "arbitrary"),
        name="pool_mixer",
    )(h, h, h, g.reshape(1, D_MODEL), w.astype(BF16), scale.reshape(1, D_MODEL))


def _gelu_tanh(x):
    c = 0.7978845608028654
    return 0.5 * x * (1.0 + jnp.tanh(c * (x + 0.044715 * (x * x * x))))


def _lru_in_kernel(h_ref, g_ref, w_ref, xb_ref, gy_ref, *, tf):
    u = jnp.concatenate(_rms_chunks(_chunks(h_ref, 0, tf), g_ref), axis=1).astype(BF16)
    yb = jnp.dot(u, w_ref[:, D_MODEL:], preferred_element_type=F32)
    gy_ref[...] = _gelu_tanh(yb)
    xb = jnp.dot(u, w_ref[:, :D_MODEL], preferred_element_type=F32)
    _store_chunks(xb_ref, 0, tf,
                  [xb[:, j * LANES:(j + 1) * LANES] for j in range(N_CHUNKS)])


def _lru_in(h_flat, g, w_in, *, tf):
    n_tok = h_flat.shape[0] // SUBLANES
    return pl.pallas_call(
        functools.partial(_lru_in_kernel, tf=tf),
        grid=(_cdiv(n_tok, tf),),
        in_specs=[
            pl.BlockSpec((tf * SUBLANES, LANES), lambda i: (i, 0)),
            pl.BlockSpec((1, D_MODEL), lambda i: (0, 0)),
            pl.BlockSpec((D_MODEL, 2 * D_MODEL), lambda i: (0, 0)),
        ],
        out_specs=[
            pl.BlockSpec((tf * SUBLANES, LANES), lambda i: (i, 0)),
            pl.BlockSpec((tf, D_MODEL), lambda i: (i, 0)),
        ],
        out_shape=[
            jax.ShapeDtypeStruct(h_flat.shape, F32),
            jax.ShapeDtypeStruct((n_tok, D_MODEL), F32),
        ],
        compiler_params=_params(56, "parallel"),
        name="lru_in_proj",
    )(h_flat, g.reshape(1, D_MODEL), w_in.astype(BF16))


def _softplus(x):
    return jnp.maximum(x, 0.0) + jnp.log1p(jnp.exp(-jnp.abs(x)))


def _lru_scan_kernel(*refs, seq, tt, reverse):
    if reverse:
        (xp_ref, xm_ref, xn_ref, cw_ref, cb_ref, wg_ref, bg_ref, lam_ref,
         o_ref, ext_ref, xc_ref, a_ref, b_ref, carry_ref) = refs
        hs = o_ref.at[0]
    else:
        (xp_ref, xm_ref, xn_ref, cw_ref, cb_ref, wg_ref, bg_ref, lam_ref,
         hb_ref, gy_ref, res_ref, wo_ref,
         o_ref, ext_ref, xc_ref, a_ref, b_ref, carry_ref, hs) = refs
    step = pl.program_id(1)
    nt = pl.num_programs(1)
    ti = nt - 1 - step if reverse else step
    t0 = ti * tt

    @pl.when(step == 0)
    def _():
        carry_ref[...] = jnp.zeros_like(carry_ref)

    def masked_rows(src, n_tok, tok0):
        rows = n_tok * SUBLANES
        tok = tok0 + (lax.broadcasted_iota(jnp.int32, (rows, 1), 0) >> 3)
        return jnp.where((tok >= 0) & (tok < seq), src[...], 0.0)

    main_rows = pl.ds(CONV_LEFT * SUBLANES, tt * SUBLANES)
    ext_ref[pl.ds(0, CONV_LEFT * SUBLANES), :] = masked_rows(
        xp_ref.at[0], CONV_LEFT, t0 - CONV_LEFT)
    ext_ref[main_rows, :] = xm_ref[0]
    ext_ref[pl.ds((CONV_LEFT + tt) * SUBLANES, CONV_RIGHT * SUBLANES), :] = masked_rows(
        xn_ref.at[0], CONV_RIGHT, t0 + tt)
    ragged = t0 + tt > seq

    @pl.when(ragged)
    def _():
        ext_ref[main_rows, :] = masked_rows(xm_ref.at[0], tt, t0)

    xc = cb_ref[...]
    for k in range(CONV_WIDTH):
        xc = xc + (ext_ref[pl.ds(k * SUBLANES, tt * SUBLANES), :]
                   .reshape(tt, SUBLANES, LANES) * cw_ref[k])
    xc_ref[...] = xc.reshape(tt * SUBLANES, LANES)

    for hh in range(LRU_HEADS):
        head_chunks = range(hh * CHUNKS_PER_HEAD, (hh + 1) * CHUNKS_PER_HEAD)
        xh = jnp.concatenate([xc_ref[pl.ds(c, tt, stride=SUBLANES), :] for c in head_chunks],
                             axis=1)
        th = jnp.tanh(jnp.dot(xh.astype(BF16), wg_ref[hh], preferred_element_type=F32)
                      + bg_ref[hh])
        lam = lam_ref[:, hh * LRU_HEAD_DIM:(hh + 1) * LRU_HEAD_DIM]
        half_c = (-0.5 * LRU_C) * _softplus(-lam)
        log_a = th[:, :LRU_HEAD_DIM] * half_c + half_c
        a = jnp.exp(log_a)
        bb = (jnp.sqrt(jnp.tanh(-log_a) * (a * a + 1.0))
              * ((th[:, LRU_HEAD_DIM:] + 1.0) * xh))
        for q, c in enumerate(head_chunks):
            a_ref[pl.ds(c, tt, stride=SUBLANES), :] = a[:, q * LANES:(q + 1) * LANES]
            b_ref[pl.ds(c, tt, stride=SUBLANES), :] = bb[:, q * LANES:(q + 1) * LANES]

    @pl.when(ragged)
    def _():
        tok = t0 + (lax.broadcasted_iota(jnp.int32, (tt * SUBLANES, 1), 0) >> 3)
        b_ref[...] = jnp.where(tok < seq, b_ref[...], 0.0)

    def scan_body(i, h):
        for k in range(SUBLANES):
            t = (tt - 1 - (i * SUBLANES + k)) if reverse else (i * SUBLANES + k)
            r0 = pl.multiple_of(t * SUBLANES, SUBLANES)
            h = a_ref[pl.ds(r0, SUBLANES), :] * h + b_ref[pl.ds(r0, SUBLANES), :]
            hs[pl.ds(r0, SUBLANES), :] = h
        return h

    carry_ref[...] = lax.fori_loop(0, tt // SUBLANES, scan_body, carry_ref[...])

    if not reverse:
        hs[...] = hs[...] + hb_ref[0]
        hsum = jnp.concatenate(_chunks(hs, 0, tt), axis=1)
        z = (hsum * gy_ref[0]).astype(BF16)
        y = jnp.dot(z, wo_ref[...], preferred_element_type=F32)
        res = res_ref.at[0]
        out = o_ref.at[0]
        for j in range(N_CHUNKS):
            out[pl.ds(j, tt, stride=SUBLANES), :] = (
                res[pl.ds(j, tt, stride=SUBLANES), :] + y[:, j * LANES:(j + 1) * LANES])


def _lru_scan(xb, conv_w, conv_b, wg, bg, lam, *, seq, tt, reverse,
              hb=None, gy=None, res=None, w_out=None):
    b = xb.shape[0]
    nt = _cdiv(seq, tt)
    right_rows = CONV_RIGHT * SUBLANES
    last_right_block = seq // CONV_RIGHT - 1

    def tix(i):
        return nt - 1 - i if reverse else i

    tile_spec = pl.BlockSpec((1, tt * SUBLANES, LANES), lambda bi, i: (bi, tix(i), 0))

    def const_spec(shape):
        return pl.BlockSpec(shape, lambda bi, i: (0,) * len(shape))

    in_specs = [
        pl.BlockSpec((1, CONV_LEFT * SUBLANES, LANES),
                     lambda bi, i: (bi, jnp.maximum(tix(i) * (tt // CONV_LEFT) - 1, 0), 0)),
        tile_spec,
        pl.BlockSpec((1, right_rows, LANES),
                     lambda bi, i: (bi, jnp.minimum((tix(i) + 1) * (tt // CONV_RIGHT),
                                                    last_right_block), 0)),
        const_spec((CONV_WIDTH, SUBLANES, LANES)),
        const_spec((SUBLANES, LANES)),
        const_spec((LRU_HEADS, LRU_HEAD_DIM, 2 * LRU_HEAD_DIM)),
        const_spec((LRU_HEADS, 1, 2 * LRU_HEAD_DIM)),
        const_spec((1, D_MODEL)),
    ]
    args = [xb, xb, xb, (0.5 * conv_w).reshape(CONV_WIDTH, SUBLANES, LANES),
            (0.5 * conv_b).reshape(SUBLANES, LANES), wg, bg, lam.reshape(1, D_MODEL)]
    scratch = [
        pltpu.VMEM(((tt + CONV_WIDTH - 1) * SUBLANES, LANES), F32),
        pltpu.VMEM((tt * SUBLANES, LANES), F32),
        pltpu.VMEM((tt * SUBLANES, LANES), F32),
        pltpu.VMEM((tt * SUBLANES, LANES), F32),
        pltpu.VMEM((SUBLANES, LANES), F32),
    ]
    if not reverse:
        in_specs += [
            tile_spec,
            pl.BlockSpec((1, tt, D_MODEL), lambda bi, i: (bi, i, 0)),
            tile_spec,
            const_spec((D_MODEL, D_MODEL)),
        ]
        args += [hb, gy, res, w_out]
        scratch.append(pltpu.VMEM((tt * SUBLANES, LANES), F32))
    return pl.pallas_call(
        functools.partial(_lru_scan_kernel, seq=seq, tt=tt, reverse=reverse),
        grid=(b, nt),
        in_specs=in_specs,
        out_specs=tile_spec,
        out_shape=jax.ShapeDtypeStruct(xb.shape, F32),
        scratch_shapes=scratch,
        compiler_params=_params(48, "parallel", "arbitrary"),
        name="lru_scan_rev" if reverse else "lru_scan_fwd_out",
    )(*args)


def _lru_layer(h, g, w_in, conv_w, conv_b, w_gates, b_gates, lam, w_out, *, seq, tt):
    b = h.shape[0]
    xb, gy = _lru_in(h.reshape(b * seq * SUBLANES, LANES), g, w_in, tf=LRU_IN_TILE)
    xb = xb.reshape(h.shape)
    gy = gy.reshape(b, seq, D_MODEL)
    wg = jnp.concatenate([w_gates[:, 0], w_gates[:, 1]], axis=-1).astype(BF16)
    bg = 0.5 * jnp.concatenate([b_gates[:, 0], b_gates[:, 1]], axis=-1)[:, :, None, :]
    scan = functools.partial(_lru_scan, xb, conv_w, conv_b, seq=seq, tt=tt)
    hb = scan(wg[1], bg[1], lam[1], reverse=True)
    return scan(wg[0], bg[0], lam[0], reverse=False,
                hb=hb, gy=gy, res=h, w_out=w_out.astype(BF16))


def _split_bf16(x):
    hi = x.astype(BF16)
    return hi, (x - hi.astype(F32)).astype(BF16)


def _route_kernel(h_ref, g_ref, rw_ref, aff_ref, *, seq, tt):
    valid = _token_ids(pl.program_id(1) * tt, tt) < seq
    ch = [jnp.where(valid, c, 0.0) for c in _chunks(h_ref.at[0], 0, tt)]
    u = jnp.concatenate(_rms_chunks(ch, g_ref), axis=1)
    u_hi, u_lo = _split_bf16(u)
    p = jnp.dot(u_hi, rw_ref[...], preferred_element_type=F32)
    q = jnp.dot(u_lo, rw_ref[:, :LANES], preferred_element_type=F32)
    logits = ((p[:, :LANES] + p[:, LANES:]) + q).T[:N_EXPERTS, :]
    m = jnp.max(logits, axis=0, keepdims=True)
    ex = jnp.exp(logits - m)
    aff_ref[0] = ex / jnp.sum(ex, axis=0, keepdims=True)


def _route(h, g, router_w, *, seq, tt, seq_pad):
    b = h.shape[0]
    w_hi, w_lo = _split_bf16(jnp.pad(router_w, ((0, 0), (0, LANES - N_EXPERTS))))
    return pl.pallas_call(
        functools.partial(_route_kernel, seq=seq, tt=tt),
        grid=(b, _cdiv(seq, tt)),
        in_specs=[
            pl.BlockSpec((1, tt * SUBLANES, LANES), lambda bi, i: (bi, i, 0)),
            pl.BlockSpec((1, D_MODEL), lambda bi, i: (0, 0)),
            pl.BlockSpec((D_MODEL, 2 * LANES), lambda bi, i: (0, 0)),
        ],
        out_specs=pl.BlockSpec((1, N_EXPERTS, tt), lambda bi, i: (bi, 0, i)),
        out_shape=jax.ShapeDtypeStruct((b, N_EXPERTS, seq_pad), F32),
        compiler_params=_params(40, "parallel", "parallel"),
        name="moe_route",
    )(h, g.reshape(1, D_MODEL), jnp.concatenate([w_hi, w_lo], axis=1))


def _cumsum_blocks(x01):
    r = lax.broadcasted_iota(jnp.int32, (LANES, LANES), 0)
    c = lax.broadcasted_iota(jnp.int32, (LANES, LANES), 1)
    tri = (r <= c).astype(BF16)
    local = [jnp.dot(x01[:, j * LANES:(j + 1) * LANES].astype(BF16), tri,
                     preferred_element_type=F32) for j in range(x01.shape[1] // LANES)]
    off = jnp.zeros((x01.shape[0], 1), F32)
    outs = []
    for cs in local:
        outs.append(cs + off)
        off = off + cs[:, LANES - 1:LANES]
    return outs


def _topk_kernel(aff_ref, idx_ref, cs_ref, *, seq, cap, cap_pad, slots_pad):
    aff = aff_ref[0]
    lane = lax.broadcasted_iota(jnp.int32, aff.shape, 1)
    aff = jnp.where(lane < seq, aff, -1.0)
    capf = float(cap)
    n_blk = aff.shape[1] // LANES

    def bit_step(i, cur):
        cand = cur | lax.shift_left(jnp.int32(1), 30 - i)
        candf = lax.bitcast_convert_type(cand, F32)
        cnt = jnp.sum((aff >= candf).astype(F32), axis=1, keepdims=True)
        return jnp.where(cnt >= capf, cand, cur)

    cur = lax.fori_loop(0, 31, bit_step, jnp.zeros((N_EXPERTS, 1), jnp.int32))
    thr = lax.bitcast_convert_type(cur, F32)
    gt = aff > thr
    eq = aff == thr
    need = capf - jnp.sum(gt.astype(F32), axis=1, keepdims=True)
    eq_cnt = jnp.concatenate(_cumsum_blocks(eq.astype(F32)), axis=1)
    sel = gt | (eq & (eq_cnt <= need))

    blocks = _cumsum_blocks(sel.astype(F32))
    for j, cs in enumerate(blocks):
        cs_ref[pl.ds(j * N_EXPERTS, N_EXPERTS), :] = cs
    cs_ref[pl.ds(n_blk * N_EXPERTS, (LANES - n_blk) * N_EXPERTS), :] = jnp.zeros(
        ((LANES - n_blk) * N_EXPERTS, LANES), F32)
    never = jnp.full((N_EXPERTS, LANES - n_blk), 2.0 * COUNT_RADIX * 256, F32)
    blk_end = jnp.concatenate([cs[:, LANES - 1:LANES] for cs in blocks] + [never], axis=1)

    slot = lax.broadcasted_iota(jnp.int32, (cap_pad, 1), 0).astype(F32)
    lane_s = lax.broadcasted_iota(jnp.int32, (cap_pad, LANES), 1).astype(F32)
    lane_e = lax.broadcasted_iota(jnp.int32, (slots_pad, LANES), 1)
    experts = range(N_EXPERTS)
    digits = []
    for e in experts:
        cmat = cs_ref[pl.ds(e, LANES, stride=N_EXPERTS), :]
        hi = jnp.floor(cmat * (1.0 / COUNT_RADIX))
        digits.append(jnp.concatenate([hi, cmat - COUNT_RADIX * hi], axis=1).astype(BF16))
    ones = jnp.ones((LANES, LANES), BF16)

    def count_le(x):
        return jnp.dot((x <= slot).astype(BF16), ones, preferred_element_type=F32)

    full = [count_le(blk_end[e:e + 1, :]) for e in experts]
    pick = [(lane_s == full[e]).astype(BF16) for e in experts]
    rows = [jnp.dot(pick[e], digits[e], preferred_element_type=F32) for e in experts]
    rows = [COUNT_RADIX * r_[:, :LANES] + r_[:, LANES:] for r_ in rows]
    inside = [count_le(rows[e]) for e in experts]
    acc = jnp.zeros((slots_pad, LANES), F32)
    pad_rows = jnp.zeros((slots_pad - cap_pad, LANES), F32)
    for e in experts:
        col = jnp.concatenate([LANES * full[e] + inside[e], pad_rows], axis=0)
        acc = jnp.where(lane_e == e, col, acc)
    idx = acc.T[:N_EXPERTS, :].astype(jnp.int32)
    slot_l = lax.broadcasted_iota(jnp.int32, idx.shape, 1)
    idx_ref[0] = jnp.where(slot_l < cap, idx, 0)


def _topk(aff_t, *, seq, cap, cap_pad, slots_pad):
    b, _, seq_pad = aff_t.shape
    assert seq_pad // LANES <= LANES and cap < COUNT_RADIX * 256
    return pl.pallas_call(
        functools.partial(_topk_kernel, seq=seq, cap=cap, cap_pad=cap_pad,
                          slots_pad=slots_pad),
        grid=(b,),
        in_specs=[pl.BlockSpec((1, N_EXPERTS, seq_pad), lambda bi: (bi, 0, 0))],
        out_specs=pl.BlockSpec((1, N_EXPERTS, slots_pad), lambda bi: (bi, 0, 0)),
        out_shape=jax.ShapeDtypeStruct((b, N_EXPERTS, slots_pad), jnp.int32),
        scratch_shapes=[pltpu.VMEM((LANES * N_EXPERTS, LANES), F32)],
        compiler_params=_params(40, "parallel"),
        name="moe_topk",
    )(aff_t)


def _gather_kernel(idx_ref, h_ref, g_ref, xs_ref, zx_ref, *, cap_pad):
    h = h_ref.at[0]

    def body(i, carry):
        for k in range(SUBLANES):
            s = i * SUBLANES + k
            t = idx_ref[0, 0, s]
            zx_ref[pl.ds(pl.multiple_of(s * SUBLANES, SUBLANES), SUBLANES), :] = (
                h[pl.ds(pl.multiple_of(t * SUBLANES, SUBLANES), SUBLANES), :])
        return carry

    lax.fori_loop(0, cap_pad // SUBLANES, body, 0)
    u = _rms_chunks(_chunks(zx_ref, 0, cap_pad), g_ref)
    xs_ref[0, 0] = jnp.concatenate(u, axis=1).astype(BF16)


def _gather(idx_smem, h, g, *, cap_pad):
    b = h.shape[0]
    slots_pad = idx_smem.shape[-1]
    return pl.pallas_call(
        functools.partial(_gather_kernel, cap_pad=cap_pad),
        grid=(b, N_EXPERTS),
        in_specs=[
            pl.BlockSpec((1, 1, slots_pad), lambda bi, e: (bi * N_EXPERTS + e, 0, 0),
                         memory_space=pltpu.SMEM),
            pl.BlockSpec((1,) + h.shape[1:], lambda bi, e: (bi, 0, 0)),
            pl.BlockSpec((1, D_MODEL), lambda bi, e: (0, 0)),
        ],
        out_specs=pl.BlockSpec((1, 1, cap_pad, D_MODEL), lambda bi, e: (e, bi, 0, 0)),
        out_shape=jax.ShapeDtypeStruct((N_EXPERTS, b, cap_pad, D_MODEL), BF16),
        scratch_shapes=[pltpu.VMEM((cap_pad * SUBLANES, LANES), F32)],
        compiler_params=_params(56, "parallel", "arbitrary"),
        name="moe_gather",
    )(idx_smem, h, g.reshape(1, D_MODEL))


def _ffn_kernel(x_ref, wg_ref, wu_ref, wd_ref, o_ref, hid_ref, wdb_ref, *, ft, n_f):
    r = pl.program_id(1)
    f = pl.program_id(2)

    @pl.when(r == 0)
    def _():
        wdb_ref[pl.ds(pl.multiple_of(f * ft, ft), ft), :] = wd_ref[0, 0].astype(BF16)

    x = x_ref[0]
    hg = jnp.dot(x, wg_ref[0, 0].astype(BF16), preferred_element_type=F32)
    hu = jnp.dot(x, wu_ref[0, 0].astype(BF16), preferred_element_type=F32)
    hid_ref[f] = (hg * jax.nn.sigmoid(hg) * hu).astype(BF16)

    @pl.when(f == n_f - 1)
    def _():
        hid = jnp.concatenate([hid_ref[k] for k in range(n_f)], axis=1)
        y = jnp.dot(hid, wdb_ref[...], preferred_element_type=F32)
        _store_chunks(o_ref.at[0], 0, y.shape[0],
                      [y[:, j * LANES:(j + 1) * LANES] for j in range(N_CHUNKS)])


def _ffn(xs, w_gate, w_up, w_down, layer):
    n_exp, rows, _ = xs.shape
    d_expert = w_gate.shape[-1]
    rt = rows // FFN_ROW_TILES
    ft = min(FFN_F_TILE, d_expert)
    n_f = d_expert // ft

    def wd_tile(e, r, f):
        return (layer, e, jnp.where(r == 0, f, n_f - 1), 0)

    return pl.pallas_call(
        functools.partial(_ffn_kernel, ft=ft, n_f=n_f),
        grid=(n_exp, FFN_ROW_TILES, n_f),
        in_specs=[
            pl.BlockSpec((1, rt, D_MODEL), lambda e, r, f: (e, r, 0)),
            pl.BlockSpec((1, 1, D_MODEL, ft), lambda e, r, f: (layer, e, 0, f)),
            pl.BlockSpec((1, 1, D_MODEL, ft), lambda e, r, f: (layer, e, 0, f)),
            pl.BlockSpec((1, 1, ft, D_MODEL), wd_tile),
        ],
        out_specs=pl.BlockSpec((1, rt * SUBLANES, LANES), lambda e, r, f: (e, r, 0)),
        out_shape=jax.ShapeDtypeStruct((n_exp, rows * SUBLANES, LANES), F32),
        scratch_shapes=[pltpu.VMEM((n_f, rt, ft), BF16),
                        pltpu.VMEM((d_expert, D_MODEL), BF16)],
        compiler_params=_params(60, "parallel", "arbitrary", "arbitrary"),
        name="moe_ffn",
    )(xs, w_gate, w_up, w_down)


def _scatter_kernel(idx_ref, aff_ref, ys_ref, h_hbm, o_hbm, acc0, acc1, load_sem, store_sem,
                    *, cap):
    b = pl.program_id(0)
    e = pl.program_id(1)
    n_b = pl.num_programs(0)
    n_e = pl.num_programs(1)
    accs = (acc0, acc1)
    ys = ys_ref.at[0, 0]

    def load(seq_i, slot):
        return pltpu.make_async_copy(h_hbm.at[seq_i], accs[slot], load_sem.at[slot])

    def store(seq_i, slot):
        return pltpu.make_async_copy(accs[slot], o_hbm.at[seq_i], store_sem.at[slot])

    def run(slot):
        acc = accs[slot]
        other = 1 - slot

        @pl.when((b == 0) & (e == 0))
        def _():
            load(b, slot).start()

        @pl.when(e == 0)
        def _():
            load(b, slot).wait()

        @pl.when((e == SCATTER_PREFETCH_STEP) & (b >= 1))
        def _():
            store(b - 1, other).wait()

        @pl.when((e == SCATTER_PREFETCH_STEP) & (b + 1 < n_b))
        def _():
            load(b + 1, other).start()

        def add_rows(slots):
            rows, vals = [], []
            for s in slots:
                t = idx_ref[0, 0, s]
                gate = aff_ref[0, 0, t]
                r0 = pl.multiple_of(t * SUBLANES, SUBLANES)
                s0 = (s * SUBLANES if isinstance(s, int)
                      else pl.multiple_of(s * SUBLANES, SUBLANES))
                vals.append(acc[pl.ds(r0, SUBLANES), :] + gate * ys[pl.ds(s0, SUBLANES), :])
                rows.append(r0)
            for r0, v in zip(rows, vals):
                acc[pl.ds(r0, SUBLANES), :] = v

        def body(i, carry):
            add_rows([i * SCATTER_GROUP + k for k in range(SCATTER_GROUP)])
            return carry

        n_groups = cap // SCATTER_GROUP
        lax.fori_loop(0, n_groups, body, 0)
        if cap % SCATTER_GROUP:
            add_rows(list(range(n_groups * SCATTER_GROUP, cap)))

        @pl.when(e == n_e - 1)
        def _():
            store(b, slot).start()

        @pl.when((e == n_e - 1) & (b == n_b - 1))
        def _():
            store(b, slot).wait()

    for slot in range(2):
        pl.when(b % 2 == slot)(functools.partial(run, slot))


def _scatter(idx_smem, aff_smem, ys, h, *, cap, cap_pad):
    b = h.shape[0]
    slots_pad = idx_smem.shape[-1]
    seq_pad = aff_smem.shape[-1]
    assert 1 <= SCATTER_PREFETCH_STEP < N_EXPERTS - 1
    return pl.pallas_call(
        functools.partial(_scatter_kernel, cap=cap),
        grid=(b, N_EXPERTS),
        in_specs=[
            pl.BlockSpec((1, 1, slots_pad), lambda bi, e: (bi * N_EXPERTS + e, 0, 0),
                         memory_space=pltpu.SMEM),
            pl.BlockSpec((1, 1, seq_pad), lambda bi, e: (bi * N_EXPERTS + e, 0, 0),
                         memory_space=pltpu.SMEM),
            pl.BlockSpec((1, 1, cap_pad * SUBLANES, LANES), lambda bi, e: (e, bi, 0, 0)),
            pl.BlockSpec(memory_space=pl.ANY),
        ],
        out_specs=pl.BlockSpec(memory_space=pl.ANY),
        out_shape=jax.ShapeDtypeStruct(h.shape, F32),
        scratch_shapes=[pltpu.VMEM(h.shape[1:], F32), pltpu.VMEM(h.shape[1:], F32),
                        pltpu.SemaphoreType.DMA((2,)), pltpu.SemaphoreType.DMA((2,))],
        compiler_params=_params(56, "arbitrary", "arbitrary"),
        name="moe_scatter",
    )(idx_smem, aff_smem, ys, h)


def _moe_layer(h, g, router_w, w_gate, w_up, w_down, layer, *, seq, tt):
    b = h.shape[0]
    cap = CAPACITY_FACTOR * seq // N_EXPERTS
    cap_pad = _round_up(cap, BF16_ROWS)
    assert (b * cap_pad) % (FFN_ROW_TILES * BF16_ROWS) == 0
    slots_pad = _round_up(cap_pad, LANES)
    seq_pad = _round_up(seq, LANES)
    aff_t = _route(h, g, router_w, seq=seq, tt=tt, seq_pad=seq_pad)
    idx = _topk(aff_t, seq=seq, cap=cap, cap_pad=cap_pad, slots_pad=slots_pad)
    idx_smem = idx.reshape(b * N_EXPERTS, 1, slots_pad)
    aff_smem = aff_t.reshape(b * N_EXPERTS, 1, seq_pad)
    xs = _gather(idx_smem, h, g, cap_pad=cap_pad)
    ys = _ffn(xs.reshape(N_EXPERTS, b * cap_pad, D_MODEL), w_gate, w_up, w_down, layer)
    ys = ys.reshape(N_EXPERTS, b, cap_pad * SUBLANES, LANES)
    return _scatter(idx_smem, aff_smem, ys, h, cap=cap, cap_pad=cap_pad)


def _final_kernel(ha_ref, hb_ref, g_ref, o_ref, *, tt):
    ha = ha_ref.at[0]
    hb = hb_ref.at[0]
    ch = [jnp.concatenate(
        [ha[pl.ds(N_META * SUBLANES + j, tt - N_META, stride=SUBLANES), :],
         hb[pl.ds(j, N_META, stride=SUBLANES), :]], axis=0) for j in range(N_CHUNKS)]
    o_ref[0] = jnp.concatenate(_rms_chunks(ch, g_ref), axis=1)


def _final(h, g, *, seq, tt):
    b = h.shape[0]
    s_out = seq - N_META
    assert s_out % tt == 0
    assert tt % N_META == 0
    return pl.pallas_call(
        functools.partial(_final_kernel, tt=tt),
        grid=(b, s_out // tt),
        in_specs=[
            pl.BlockSpec((1, tt * SUBLANES, LANES), lambda bi, i: (bi, i, 0)),
            pl.BlockSpec((1, N_META * SUBLANES, LANES),
                         lambda bi, i: (bi, (i + 1) * (tt // N_META), 0)),
            pl.BlockSpec((1, D_MODEL), lambda bi, i: (0, 0)),
        ],
        out_specs=pl.BlockSpec((1, tt, D_MODEL), lambda bi, i: (bi, i, 0)),
        out_shape=jax.ShapeDtypeStruct((b, s_out, D_MODEL), F32),
        compiler_params=_params(40, "parallel", "parallel"),
        name="final_norm",
    )(h, h, g.reshape(1, D_MODEL))


def kernel(x, meta_tokens, norm_mix, norm_ffn, norm_final, pool_w, pool_scale, lru_w_in, lru_conv_w, lru_conv_b, lru_w_gates, lru_b_gates, lru_lambda, lru_w_out, router_w, moe_w_gate, moe_w_up, moe_w_down):
    b, s, d = x.shape
    assert d == D_MODEL
    seq = s + N_META
    tt = TIME_TILE
    depth = norm_mix.shape[0]
    h = _embed(x, meta_tokens.astype(x.dtype), tt=tt)
    for i in range(depth):
        j = i // 2
        if i % 2 == 0:
            h = _pool_layer(h, norm_mix[i], pool_w[j], pool_scale[j], seq=seq, tt=tt)
        else:
            h = _lru_layer(h, norm_mix[i], lru_w_in[j], lru_conv_w[j], lru_conv_b[j],
                           lru_w_gates[j], lru_b_gates[j], lru_lambda[j], lru_w_out[j],
                           seq=seq, tt=tt)
        h = _moe_layer(h, norm_ffn[i], router_w[i], moe_w_gate, moe_w_up, moe_w_down, i,
                       seq=seq, tt=tt)
    return _final(h, norm_final, seq=seq, tt=tt)
```

```python
import functools

import jax
import jax.numpy as jnp
from jax import lax
from jax.experimental import pallas as pl
from jax.experimental.pallas import tpu as pltpu

F32 = jnp.float32
BF16 = jnp.bfloat16

LANES = 128
SUBLANES = 8
BF16_ROWS = 16
D_MODEL = 1024
N_CHUNKS = D_MODEL // LANES
assert N_CHUNKS == SUBLANES
N_META = 16
POOL_WINDOWS = (2, 4, 8, 16)
POOL_GROUP = D_MODEL // len(POOL_WINDOWS)
CHUNKS_PER_GROUP = POOL_GROUP // LANES
POOL_HALO = 8
LRU_HEADS = 4
LRU_HEAD_DIM = D_MODEL // LRU_HEADS
CHUNKS_PER_HEAD = LRU_HEAD_DIM // LANES
LRU_C = 8.0
CONV_WIDTH = 4
CONV_LEFT = 1
CONV_RIGHT = CONV_WIDTH - 1 - CONV_LEFT
N_EXPERTS = 16
CAPACITY_FACTOR = 2
RMS_EPS = 1e-6
TIME_TILE = 512
FFN_ROW_TILES = 4
FFN_F_TILE = 1024
LRU_IN_TILE = 1024
GATHER_GROUP = BF16_ROWS
SCATTER_PREFETCH_STEP = 3
SCATTER_GROUP = 8
COUNT_RADIX = 32.0
MIB = 1024 * 1024


def _cdiv(a, b):
    return -(-a // b)


def _round_up(a, b):
    return _cdiv(a, b) * b


def _chunks(ref2d, tok0, n):
    return [ref2d[pl.ds(tok0 * SUBLANES + j, n, stride=SUBLANES), :]
            for j in range(N_CHUNKS)]


def _store_chunks(ref2d, tok0, n, chunks):
    for j in range(N_CHUNKS):
        ref2d[pl.ds(tok0 * SUBLANES + j, n, stride=SUBLANES), :] = chunks[j]


def _lane_chunk(ref, j):
    return ref[:, j * LANES:(j + 1) * LANES]


def _rms_chunks(chunks, g_ref):
    ss = chunks[0] * chunks[0]
    for c in chunks[1:]:
        ss = ss + c * c
    ms = jnp.sum(ss, axis=-1, keepdims=True) * (1.0 / D_MODEL)
    inv = lax.rsqrt(ms + RMS_EPS)
    return [c * inv * _lane_chunk(g_ref, j) for j, c in enumerate(chunks)]


def _token_ids(tok0, n):
    return tok0 + lax.broadcasted_iota(jnp.int32, (n, 1), 0)


def _params(vmem_mib, *sem):
    return pltpu.CompilerParams(dimension_semantics=sem,
                                vmem_limit_bytes=vmem_mib * MIB)


def _embed_kernel(meta_ref, xa_ref, xb_ref, o_ref, *, tt):
    i = pl.program_id(1)
    out = o_ref.at[0]
    body = xb_ref[0, :tt - N_META, :]
    _store_chunks(out, N_META, tt - N_META,
                  [body[:, j * LANES:(j + 1) * LANES] for j in range(N_CHUNKS)])

    def head(src):
        _store_chunks(out, 0, N_META,
                      [src[:, j * LANES:(j + 1) * LANES] for j in range(N_CHUNKS)])

    @pl.when(i == 0)
    def _():
        head(meta_ref[...])

    @pl.when(i > 0)
    def _():
        head(xa_ref[0])


def _embed(x, meta, *, tt):
    b, s, _ = x.shape
    seq = s + N_META
    assert s % tt == 0 and tt % N_META == 0
    last = s // tt - 1
    metas_per_tile = tt // N_META
    return pl.pallas_call(
        functools.partial(_embed_kernel, tt=tt),
        grid=(b, _cdiv(seq, tt)),
        in_specs=[
            pl.BlockSpec((N_META, D_MODEL), lambda bi, i: (0, 0)),
            pl.BlockSpec((1, N_META, D_MODEL),
                         lambda bi, i: (bi, jnp.maximum(i * metas_per_tile - 1, 0), 0)),
            pl.BlockSpec((1, tt, D_MODEL), lambda bi, i: (bi, jnp.minimum(i, last), 0)),
        ],
        out_specs=pl.BlockSpec((1, tt * SUBLANES, LANES), lambda bi, i: (bi, i, 0)),
        out_shape=jax.ShapeDtypeStruct((b, seq * SUBLANES, LANES), F32),
        compiler_params=_params(40, "parallel", "parallel"),
        name="embed",
    )(meta, x, x)


def _pool_kernel(hp_ref, hm_ref, hn_ref, g_ref, w_ref, sc_ref, gf_ref, rw_ref,
                 o_ref, aff_ref, ext_ref, s2_ref, s4_ref, s8_ref, p_ref, *, seq, tt):
    ti = pl.program_id(1)
    t0 = ti * tt
    hm = hm_ref.at[0]
    out = o_ref.at[0]

    def norm_into_ext(src, n, ext_tok0, tok0):
        tok = _token_ids(tok0, n)
        valid = (tok >= 0) & (tok < seq)
        ch = [jnp.where(valid, c, 0.0) for c in _chunks(src, 0, n)]
        _store_chunks(ext_ref, ext_tok0, n, _rms_chunks(ch, g_ref))

    norm_into_ext(hp_ref.at[0], POOL_HALO, 0, t0 - POOL_HALO)
    norm_into_ext(hm, tt, POOL_HALO, t0)
    norm_into_ext(hn_ref.at[0], POOL_HALO, POOL_HALO + tt, t0 + tt)

    def rows(a, b):
        return pl.ds(a * SUBLANES, (b - a) * SUBLANES)

    lo, hi = 1, tt + 2 * POOL_HALO
    s2_ref[rows(lo, hi), :] = ext_ref[rows(lo - 1, hi - 1), :] + ext_ref[rows(lo, hi), :]
    prev, half = s2_ref, 1
    for nxt in (s4_ref, s8_ref):
        lo, hi = lo + half, hi - half
        nxt[rows(lo, hi), :] = (prev[rows(lo - half, hi - half), :]
                                + prev[rows(lo + half, hi + half), :])
        prev, half = nxt, 2 * half
    m0, m1 = POOL_HALO, POOL_HALO + tt
    assert lo + half <= m0 and m1 <= hi - half and 4 * half == POOL_WINDOWS[-1]

    grp = lax.broadcasted_iota(jnp.int32, (SUBLANES, LANES), 0) // CHUNKS_PER_GROUP

    def window_sums():
        def tiles(v):
            return v.reshape(tt, SUBLANES, LANES)

        s16 = prev[rows(m0 - half, m1 - half), :] + prev[rows(m0 + half, m1 + half), :]
        return jnp.where(grp == 0, tiles(s2_ref[rows(m0, m1), :]),
                         jnp.where(grp == 1, tiles(s4_ref[rows(m0, m1), :]),
                                   jnp.where(grp == 2, tiles(s8_ref[rows(m0, m1), :]),
                                             tiles(s16))))

    near_end = (ti == 0) | (t0 + tt > seq - POOL_HALO)

    @pl.when(jnp.logical_not(near_end))
    def _():
        inv_win = jnp.where(grp == 0, 0.5, jnp.where(grp == 1, 0.25,
                                                     jnp.where(grp == 2, 0.125, 0.0625)))
        p_ref[...] = ((window_sums() * inv_win).reshape(tt * SUBLANES, LANES)
                      - ext_ref[rows(m0, m1), :])

    @pl.when(near_end)
    def _():
        r = lax.broadcasted_iota(jnp.int32, (tt * SUBLANES, 1), 0)
        tok = t0 + (r >> 3)
        left = lax.shift_left(jnp.int32(1), (r & (SUBLANES - 1)) // CHUNKS_PER_GROUP)
        cnt = jnp.minimum(tok + left, seq) - jnp.maximum(tok - left, 0)
        cnt = jnp.maximum(cnt, 1).astype(F32)
        p_ref[...] = (window_sums().reshape(tt * SUBLANES, LANES) / cnt
                      - ext_ref[rows(m0, m1), :])

    new_chunks = []
    for g in range(len(POOL_WINDOWS)):
        group_chunks = range(g * CHUNKS_PER_GROUP, (g + 1) * CHUNKS_PER_GROUP)
        p = jnp.concatenate([p_ref[pl.ds(c, tt, stride=SUBLANES), :] for c in group_chunks],
                            axis=1).astype(BF16)
        y = jnp.dot(p, w_ref[g], preferred_element_type=F32)
        for q, c in enumerate(group_chunks):
            res = hm[pl.ds(c, tt, stride=SUBLANES), :]
            new_chunks.append(res + y[:, q * LANES:(q + 1) * LANES] * _lane_chunk(sc_ref, c))
            out[pl.ds(c, tt, stride=SUBLANES), :] = new_chunks[-1]
    aff_ref[0] = _route_affinities(new_chunks, t0, gf_ref, rw_ref, seq=seq)


def _pool_layer(h, g, w, scale, g_ffn, router_w, *, seq, tt):
    b = h.shape[0]
    nt = _cdiv(seq, tt)
    halo_rows = POOL_HALO * SUBLANES
    blocks_per_tile = tt // POOL_HALO
    last_halo_block = seq // POOL_HALO - 1
    route_in, route_out, route_shape = _route_specs(b, seq, tt)
    return pl.pallas_call(
        functools.partial(_pool_kernel, seq=seq, tt=tt),
        grid=(b, nt),
        in_specs=[
            pl.BlockSpec((1, halo_rows, LANES),
                         lambda bi, i: (bi, jnp.maximum(i * blocks_per_tile - 1, 0), 0)),
            pl.BlockSpec((1, tt * SUBLANES, LANES), lambda bi, i: (bi, i, 0)),
            pl.BlockSpec((1, halo_rows, LANES),
                         lambda bi, i: (bi, jnp.minimum((i + 1) * blocks_per_tile,
                                                        last_halo_block), 0)),
            pl.BlockSpec((1, D_MODEL), lambda bi, i: (0, 0)),
            pl.BlockSpec((len(POOL_WINDOWS), POOL_GROUP, POOL_GROUP),
                         lambda bi, i: (0, 0, 0)),
            pl.BlockSpec((1, D_MODEL), lambda bi, i: (0, 0)),
        ] + route_in,
        out_specs=[pl.BlockSpec((1, tt * SUBLANES, LANES), lambda bi, i: (bi, i, 0)),
                   route_out],
        out_shape=[jax.ShapeDtypeStruct(h.shape, F32), route_shape],
        scratch_shapes=[pltpu.VMEM(((tt + 2 * POOL_HALO) * SUBLANES, LANES), F32)] * 4
        + [pltpu.VMEM((tt * SUBLANES, LANES), F32)],
        compiler_params=_params(40, "parallel", "arbitrary"),
        name="pool_mixer",
    )(h, h, h, g.reshape(1, D_MODEL), w.astype(BF16), scale.reshape(1, D_MODEL),
      g_ffn.reshape(1, D_MODEL), _router_operand(router_w))


def _gelu_tanh(x):
    c = 0.7978845608028654
    return 0.5 * x * (1.0 + jnp.tanh(c * (x + 0.044715 * (x * x * x))))


def _lru_in_kernel(h_ref, g_ref, w_ref, xb_ref, gy_ref, *, tf):
    u = jnp.concatenate(_rms_chunks(_chunks(h_ref, 0, tf), g_ref), axis=1).astype(BF16)
    yb = jnp.dot(u, w_ref[:, D_MODEL:], preferred_element_type=F32)
    gy_ref[...] = _gelu_tanh(yb)
    xb = jnp.dot(u, w_ref[:, :D_MODEL], preferred_element_type=F32)
    _store_chunks(xb_ref, 0, tf,
                  [xb[:, j * LANES:(j + 1) * LANES] for j in range(N_CHUNKS)])


def _lru_in(h_flat, g, w_in, *, tf):
    n_tok = h_flat.shape[0] // SUBLANES
    return pl.pallas_call(
        functools.partial(_lru_in_kernel, tf=tf),
        grid=(_cdiv(n_tok, tf),),
        in_specs=[
            pl.BlockSpec((tf * SUBLANES, LANES), lambda i: (i, 0)),
            pl.BlockSpec((1, D_MODEL), lambda i: (0, 0)),
            pl.BlockSpec((D_MODEL, 2 * D_MODEL), lambda i: (0, 0)),
        ],
        out_specs=[
            pl.BlockSpec((tf * SUBLANES, LANES), lambda i: (i, 0)),
            pl.BlockSpec((tf, D_MODEL), lambda i: (i, 0)),
        ],
        out_shape=[
            jax.ShapeDtypeStruct(h_flat.shape, F32),
            jax.ShapeDtypeStruct((n_tok, D_MODEL), F32),
        ],
        compiler_params=_params(56, "parallel"),
        name="lru_in_proj",
    )(h_flat, g.reshape(1, D_MODEL), w_in.astype(BF16))


def _softplus(x):
    return jnp.maximum(x, 0.0) + jnp.log1p(jnp.exp(-jnp.abs(x)))


def _lru_scan_kernel(*refs, seq, tt, reverse):
    if reverse:
        (xp_ref, xm_ref, xn_ref, cw_ref, cb_ref, wg_ref, bg_ref, lam_ref,
         o_ref, ext_ref, xc_ref, a_ref, b_ref, carry_ref) = refs
        hs = o_ref.at[0]
    else:
        (xp_ref, xm_ref, xn_ref, cw_ref, cb_ref, wg_ref, bg_ref, lam_ref,
         hb_ref, gy_ref, res_ref, wo_ref, gf_ref, rw_ref,
         o_ref, aff_ref, ext_ref, xc_ref, a_ref, b_ref, carry_ref, hs) = refs
    step = pl.program_id(1)
    nt = pl.num_programs(1)
    ti = nt - 1 - step if reverse else step
    t0 = ti * tt

    @pl.when(step == 0)
    def _():
        carry_ref[...] = jnp.zeros_like(carry_ref)

    def masked_rows(src, n_tok, tok0):
        rows = n_tok * SUBLANES
        tok = tok0 + (lax.broadcasted_iota(jnp.int32, (rows, 1), 0) >> 3)
        return jnp.where((tok >= 0) & (tok < seq), src[...], 0.0)

    main_rows = pl.ds(CONV_LEFT * SUBLANES, tt * SUBLANES)
    ext_ref[pl.ds(0, CONV_LEFT * SUBLANES), :] = masked_rows(
        xp_ref.at[0], CONV_LEFT, t0 - CONV_LEFT)
    ext_ref[main_rows, :] = xm_ref[0]
    ext_ref[pl.ds((CONV_LEFT + tt) * SUBLANES, CONV_RIGHT * SUBLANES), :] = masked_rows(
        xn_ref.at[0], CONV_RIGHT, t0 + tt)
    ragged = t0 + tt > seq

    @pl.when(ragged)
    def _():
        ext_ref[main_rows, :] = masked_rows(xm_ref.at[0], tt, t0)

    xc = cb_ref[...]
    for k in range(CONV_WIDTH):
        xc = xc + (ext_ref[pl.ds(k * SUBLANES, tt * SUBLANES), :]
                   .reshape(tt, SUBLANES, LANES) * cw_ref[k])
    xc_ref[...] = xc.reshape(tt * SUBLANES, LANES)

    for hh in range(LRU_HEADS):
        head_chunks = range(hh * CHUNKS_PER_HEAD, (hh + 1) * CHUNKS_PER_HEAD)
        xh = jnp.concatenate([xc_ref[pl.ds(c, tt, stride=SUBLANES), :] for c in head_chunks],
                             axis=1)
        th = jnp.tanh(jnp.dot(xh.astype(BF16), wg_ref[hh], preferred_element_type=F32)
                      + bg_ref[hh])
        lam = lam_ref[:, hh * LRU_HEAD_DIM:(hh + 1) * LRU_HEAD_DIM]
        half_c = (-0.5 * LRU_C) * _softplus(-lam)
        log_a = th[:, :LRU_HEAD_DIM] * half_c + half_c
        a = jnp.exp(log_a)
        bb = (jnp.sqrt(jnp.tanh(-log_a) * (a * a + 1.0))
              * ((th[:, LRU_HEAD_DIM:] + 1.0) * xh))
        for q, c in enumerate(head_chunks):
            a_ref[pl.ds(c, tt, stride=SUBLANES), :] = a[:, q * LANES:(q + 1) * LANES]
            b_ref[pl.ds(c, tt, stride=SUBLANES), :] = bb[:, q * LANES:(q + 1) * LANES]

    @pl.when(ragged)
    def _():
        tok = t0 + (lax.broadcasted_iota(jnp.int32, (tt * SUBLANES, 1), 0) >> 3)
        b_ref[...] = jnp.where(tok < seq, b_ref[...], 0.0)

    def scan_body(i, h):
        for k in range(SUBLANES):
            t = (tt - 1 - (i * SUBLANES + k)) if reverse else (i * SUBLANES + k)
            r0 = pl.multiple_of(t * SUBLANES, SUBLANES)
            h = a_ref[pl.ds(r0, SUBLANES), :] * h + b_ref[pl.ds(r0, SUBLANES), :]
            hs[pl.ds(r0, SUBLANES), :] = h
        return h

    carry_ref[...] = lax.fori_loop(0, tt // SUBLANES, scan_body, carry_ref[...])

    if not reverse:
        hs[...] = hs[...] + hb_ref[0]
        hsum = jnp.concatenate(_chunks(hs, 0, tt), axis=1)
        z = (hsum * gy_ref[0]).astype(BF16)
        y = jnp.dot(z, wo_ref[...], preferred_element_type=F32)
        res = res_ref.at[0]
        new_chunks = [res[pl.ds(j, tt, stride=SUBLANES), :] + y[:, j * LANES:(j + 1) * LANES]
                      for j in range(N_CHUNKS)]
        _store_chunks(o_ref.at[0], 0, tt, new_chunks)
        aff_ref[0] = _route_affinities(new_chunks, t0, gf_ref, rw_ref, seq=seq)


def _lru_scan(xb, conv_w, conv_b, wg, bg, lam, *, seq, tt, reverse,
              hb=None, gy=None, res=None, w_out=None, g_ffn=None, router_w=None):
    b = xb.shape[0]
    nt = _cdiv(seq, tt)
    right_rows = CONV_RIGHT * SUBLANES
    last_right_block = seq // CONV_RIGHT - 1

    def tix(i):
        return nt - 1 - i if reverse else i

    tile_spec = pl.BlockSpec((1, tt * SUBLANES, LANES), lambda bi, i: (bi, tix(i), 0))

    def const_spec(shape):
        return pl.BlockSpec(shape, lambda bi, i: (0,) * len(shape))

    in_specs = [
        pl.BlockSpec((1, CONV_LEFT * SUBLANES, LANES),
                     lambda bi, i: (bi, jnp.maximum(tix(i) * (tt // CONV_LEFT) - 1, 0), 0)),
        tile_spec,
        pl.BlockSpec((1, right_rows, LANES),
                     lambda bi, i: (bi, jnp.minimum((tix(i) + 1) * (tt // CONV_RIGHT),
                                                    last_right_block), 0)),
        const_spec((CONV_WIDTH, SUBLANES, LANES)),
        const_spec((SUBLANES, LANES)),
        const_spec((LRU_HEADS, LRU_HEAD_DIM, 2 * LRU_HEAD_DIM)),
        const_spec((LRU_HEADS, 1, 2 * LRU_HEAD_DIM)),
        const_spec((1, D_MODEL)),
    ]
    args = [xb, xb, xb, (0.5 * conv_w).reshape(CONV_WIDTH, SUBLANES, LANES),
            (0.5 * conv_b).reshape(SUBLANES, LANES), wg, bg, lam.reshape(1, D_MODEL)]
    scratch = [
        pltpu.VMEM(((tt + CONV_WIDTH - 1) * SUBLANES, LANES), F32),
        pltpu.VMEM((tt * SUBLANES, LANES), F32),
        pltpu.VMEM((tt * SUBLANES, LANES), F32),
        pltpu.VMEM((tt * SUBLANES, LANES), F32),
        pltpu.VMEM((SUBLANES, LANES), F32),
    ]
    out_specs = tile_spec
    out_shape = jax.ShapeDtypeStruct(xb.shape, F32)
    if not reverse:
        route_in, route_out, route_shape = _route_specs(b, seq, tt)
        in_specs += [
            tile_spec,
            pl.BlockSpec((1, tt, D_MODEL), lambda bi, i: (bi, i, 0)),
            tile_spec,
            const_spec((D_MODEL, D_MODEL)),
        ] + route_in
        args += [hb, gy, res, w_out, g_ffn.reshape(1, D_MODEL), _router_operand(router_w)]
        scratch.append(pltpu.VMEM((tt * SUBLANES, LANES), F32))
        out_specs = [tile_spec, route_out]
        out_shape = [out_shape, route_shape]
    return pl.pallas_call(
        functools.partial(_lru_scan_kernel, seq=seq, tt=tt, reverse=reverse),
        grid=(b, nt),
        in_specs=in_specs,
        out_specs=out_specs,
        out_shape=out_shape,
        scratch_shapes=scratch,
        compiler_params=_params(48, "parallel", "arbitrary"),
        name="lru_scan_rev" if reverse else "lru_scan_fwd_out",
    )(*args)


def _lru_layer(h, g, w_in, conv_w, conv_b, w_gates, b_gates, lam, w_out, g_ffn, router_w,
               *, seq, tt):
    b = h.shape[0]
    xb, gy = _lru_in(h.reshape(b * seq * SUBLANES, LANES), g, w_in, tf=LRU_IN_TILE)
    xb = xb.reshape(h.shape)
    gy = gy.reshape(b, seq, D_MODEL)
    wg = jnp.concatenate([w_gates[:, 0], w_gates[:, 1]], axis=-1).astype(BF16)
    bg = 0.5 * jnp.concatenate([b_gates[:, 0], b_gates[:, 1]], axis=-1)[:, :, None, :]
    scan = functools.partial(_lru_scan, xb, conv_w, conv_b, seq=seq, tt=tt)
    hb = scan(wg[1], bg[1], lam[1], reverse=True)
    return scan(wg[0], bg[0], lam[0], reverse=False,
                hb=hb, gy=gy, res=h, w_out=w_out.astype(BF16),
                g_ffn=g_ffn, router_w=router_w)


def _split_bf16(x):
    hi = x.astype(BF16)
    return hi, (x - hi.astype(F32)).astype(BF16)


def _route_affinities(chunks, tok0, g_ref, rw_ref, *, seq):
    n = chunks[0].shape[0]
    valid = _token_ids(tok0, n) < seq
    u = jnp.concatenate(_rms_chunks([jnp.where(valid, c, 0.0) for c in chunks], g_ref), axis=1)
    u_hi, u_lo = _split_bf16(u)
    p = jnp.dot(u_hi, rw_ref[...], preferred_element_type=F32)
    q = jnp.dot(u_lo, rw_ref[:, :LANES], preferred_element_type=F32)
    logits = ((p[:, :LANES] + p[:, LANES:]) + q).T[:N_EXPERTS, :]
    m = jnp.max(logits, axis=0, keepdims=True)
    ex = jnp.exp(logits - m)
    return ex / jnp.sum(ex, axis=0, keepdims=True)


def _router_operand(router_w):
    w_hi, w_lo = _split_bf16(jnp.pad(router_w, ((0, 0), (0, LANES - N_EXPERTS))))
    return jnp.concatenate([w_hi, w_lo], axis=1)


def _route_specs(b, seq, tt):
    seq_pad = _round_up(seq, LANES)
    return ([pl.BlockSpec((1, D_MODEL), lambda bi, i: (0, 0)),
             pl.BlockSpec((D_MODEL, 2 * LANES), lambda bi, i: (0, 0))],
            pl.BlockSpec((1, N_EXPERTS, tt), lambda bi, i: (bi, 0, i)),
            jax.ShapeDtypeStruct((b, N_EXPERTS, seq_pad), F32))


def _cumsum_blocks(x01):
    r = lax.broadcasted_iota(jnp.int32, (LANES, LANES), 0)
    c = lax.broadcasted_iota(jnp.int32, (LANES, LANES), 1)
    tri = (r <= c).astype(BF16)
    local = [jnp.dot(x01[:, j * LANES:(j + 1) * LANES].astype(BF16), tri,
                     preferred_element_type=F32) for j in range(x01.shape[1] // LANES)]
    off = jnp.zeros((x01.shape[0], 1), F32)
    outs = []
    for cs in local:
        outs.append(cs + off)
        off = off + cs[:, LANES - 1:LANES]
    return outs


def _topk_kernel(aff_ref, idx_ref, cs_ref, *, seq, cap, cap_pad, slots_pad):
    aff = aff_ref[0]
    lane = lax.broadcasted_iota(jnp.int32, aff.shape, 1)
    aff = jnp.where(lane < seq, aff, -1.0)
    capf = float(cap)
    n_blk = aff.shape[1] // LANES

    def bit_step(i, cur):
        cand = cur | lax.shift_left(jnp.int32(1), 30 - i)
        candf = lax.bitcast_convert_type(cand, F32)
        cnt = jnp.sum((aff >= candf).astype(F32), axis=1, keepdims=True)
        return jnp.where(cnt >= capf, cand, cur)

    cur = lax.fori_loop(0, 31, bit_step, jnp.zeros((N_EXPERTS, 1), jnp.int32))
    thr = lax.bitcast_convert_type(cur, F32)
    gt = aff > thr
    eq = aff == thr
    need = capf - jnp.sum(gt.astype(F32), axis=1, keepdims=True)
    eq_cnt = jnp.concatenate(_cumsum_blocks(eq.astype(F32)), axis=1)
    sel = gt | (eq & (eq_cnt <= need))

    blocks = _cumsum_blocks(sel.astype(F32))
    for j, cs in enumerate(blocks):
        cs_ref[pl.ds(j * N_EXPERTS, N_EXPERTS), :] = cs
    cs_ref[pl.ds(n_blk * N_EXPERTS, (LANES - n_blk) * N_EXPERTS), :] = jnp.zeros(
        ((LANES - n_blk) * N_EXPERTS, LANES), F32)
    never = jnp.full((N_EXPERTS, LANES - n_blk), 2.0 * COUNT_RADIX * 256, F32)
    blk_end = jnp.concatenate([cs[:, LANES - 1:LANES] for cs in blocks] + [never], axis=1)

    slot = lax.broadcasted_iota(jnp.int32, (cap_pad, 1), 0).astype(F32)
    lane_s = lax.broadcasted_iota(jnp.int32, (cap_pad, LANES), 1).astype(F32)
    lane_e = lax.broadcasted_iota(jnp.int32, (slots_pad, LANES), 1)
    experts = range(N_EXPERTS)
    digits = []
    for e in experts:
        cmat = cs_ref[pl.ds(e, LANES, stride=N_EXPERTS), :]
        hi = jnp.floor(cmat * (1.0 / COUNT_RADIX))
        digits.append(jnp.concatenate([hi, cmat - COUNT_RADIX * hi], axis=1).astype(BF16))
    ones = jnp.ones((LANES, LANES), BF16)

    def count_le(x):
        return jnp.dot((x <= slot).astype(BF16), ones, preferred_element_type=F32)

    full = [count_le(blk_end[e:e + 1, :]) for e in experts]
    pick = [(lane_s == full[e]).astype(BF16) for e in experts]
    rows = [jnp.dot(pick[e], digits[e], preferred_element_type=F32) for e in experts]
    rows = [COUNT_RADIX * r_[:, :LANES] + r_[:, LANES:] for r_ in rows]
    inside = [count_le(rows[e]) for e in experts]
    acc = jnp.zeros((slots_pad, LANES), F32)
    pad_rows = jnp.zeros((slots_pad - cap_pad, LANES), F32)
    for e in experts:
        col = jnp.concatenate([LANES * full[e] + inside[e], pad_rows], axis=0)
        acc = jnp.where(lane_e == e, col, acc)
    idx = acc.T[:N_EXPERTS, :].astype(jnp.int32)
    slot_l = lax.broadcasted_iota(jnp.int32, idx.shape, 1)
    idx_ref[0] = jnp.where(slot_l < cap, idx, 0)


def _topk(aff_t, *, seq, cap, cap_pad, slots_pad):
    b, _, seq_pad = aff_t.shape
    assert seq_pad // LANES <= LANES and cap < COUNT_RADIX * 256
    return pl.pallas_call(
        functools.partial(_topk_kernel, seq=seq, cap=cap, cap_pad=cap_pad,
                          slots_pad=slots_pad),
        grid=(b,),
        in_specs=[pl.BlockSpec((1, N_EXPERTS, seq_pad), lambda bi: (bi, 0, 0))],
        out_specs=pl.BlockSpec((1, N_EXPERTS, slots_pad), lambda bi: (bi, 0, 0)),
        out_shape=jax.ShapeDtypeStruct((b, N_EXPERTS, slots_pad), jnp.int32),
        scratch_shapes=[pltpu.VMEM((LANES * N_EXPERTS, LANES), F32)],
        compiler_params=_params(40, "parallel"),
        name="moe_topk",
    )(aff_t)


def _gather_kernel(idx_ref, h_ref, g_ref, xs_ref, zx_ref, *, cap_pad):
    h = h_ref.at[0]

    def body(i, carry):
        toks = [idx_ref[0, 0, i * GATHER_GROUP + k] for k in range(GATHER_GROUP)]
        tiles = [h[pl.ds(pl.multiple_of(t * SUBLANES, SUBLANES), SUBLANES), :] for t in toks]
        for k, tile in enumerate(tiles):
            s0 = pl.multiple_of((i * GATHER_GROUP + k) * SUBLANES, SUBLANES)
            zx_ref[pl.ds(s0, SUBLANES), :] = tile
        return carry

    lax.fori_loop(0, cap_pad // GATHER_GROUP, body, 0)
    u = _rms_chunks(_chunks(zx_ref, 0, cap_pad), g_ref)
    xs_ref[0, 0] = jnp.concatenate(u, axis=1).astype(BF16)


def _gather(idx_smem, h, g, *, cap_pad):
    b = h.shape[0]
    slots_pad = idx_smem.shape[-1]
    return pl.pallas_call(
        functools.partial(_gather_kernel, cap_pad=cap_pad),
        grid=(b, N_EXPERTS),
        in_specs=[
            pl.BlockSpec((1, 1, slots_pad), lambda bi, e: (bi * N_EXPERTS + e, 0, 0),
                         memory_space=pltpu.SMEM),
            pl.BlockSpec((1,) + h.shape[1:], lambda bi, e: (bi, 0, 0)),
            pl.BlockSpec((1, D_MODEL), lambda bi, e: (0, 0)),
        ],
        out_specs=pl.BlockSpec((1, 1, cap_pad, D_MODEL), lambda bi, e: (e, bi, 0, 0)),
        out_shape=jax.ShapeDtypeStruct((N_EXPERTS, b, cap_pad, D_MODEL), BF16),
        scratch_shapes=[pltpu.VMEM((cap_pad * SUBLANES, LANES), F32)],
        compiler_params=_params(56, "parallel", "arbitrary"),
        name="moe_gather",
    )(idx_smem, h, g.reshape(1, D_MODEL))


def _ffn_kernel(x_ref, wg_ref, wu_ref, wd_ref, o_ref, hid_ref, wdb_ref, *, ft, n_f):
    r = pl.program_id(1)
    f = pl.program_id(2)

    @pl.when(r == 0)
    def _():
        wdb_ref[pl.ds(pl.multiple_of(f * ft, ft), ft), :] = wd_ref[0, 0].astype(BF16)

    x = x_ref[0]
    hg = jnp.dot(x, wg_ref[0, 0].astype(BF16), preferred_element_type=F32)
    hu = jnp.dot(x, wu_ref[0, 0].astype(BF16), preferred_element_type=F32)
    hid_ref[f] = (hg * jax.nn.sigmoid(hg) * hu).astype(BF16)

    @pl.when(f == n_f - 1)
    def _():
        hid = jnp.concatenate([hid_ref[k] for k in range(n_f)], axis=1)
        y = jnp.dot(hid, wdb_ref[...], preferred_element_type=F32)
        _store_chunks(o_ref.at[0], 0, y.shape[0],
                      [y[:, j * LANES:(j + 1) * LANES] for j in range(N_CHUNKS)])


def _ffn(xs, w_gate, w_up, w_down, layer):
    n_exp, rows, _ = xs.shape
    d_expert = w_gate.shape[-1]
    rt = rows // FFN_ROW_TILES
    ft = min(FFN_F_TILE, d_expert)
    n_f = d_expert // ft

    def wd_tile(e, r, f):
        return (layer, e, jnp.where(r == 0, f, n_f - 1), 0)

    return pl.pallas_call(
        functools.partial(_ffn_kernel, ft=ft, n_f=n_f),
        grid=(n_exp, FFN_ROW_TILES, n_f),
        in_specs=[
            pl.BlockSpec((1, rt, D_MODEL), lambda e, r, f: (e, r, 0)),
            pl.BlockSpec((1, 1, D_MODEL, ft), lambda e, r, f: (layer, e, 0, f)),
            pl.BlockSpec((1, 1, D_MODEL, ft), lambda e, r, f: (layer, e, 0, f)),
            pl.BlockSpec((1, 1, ft, D_MODEL), wd_tile),
        ],
        out_specs=pl.BlockSpec((1, rt * SUBLANES, LANES), lambda e, r, f: (e, r, 0)),
        out_shape=jax.ShapeDtypeStruct((n_exp, rows * SUBLANES, LANES), F32),
        scratch_shapes=[pltpu.VMEM((n_f, rt, ft), BF16),
                        pltpu.VMEM((d_expert, D_MODEL), BF16)],
        compiler_params=_params(60, "parallel", "arbitrary", "arbitrary"),
        name="moe_ffn",
    )(xs, w_gate, w_up, w_down)


def _scatter_kernel(idx_ref, aff_ref, ys_ref, h_hbm, o_hbm, acc0, acc1, load_sem, store_sem,
                    *, cap):
    b = pl.program_id(0)
    e = pl.program_id(1)
    n_b = pl.num_programs(0)
    n_e = pl.num_programs(1)
    accs = (acc0, acc1)
    ys = ys_ref.at[0, 0]

    def load(seq_i, slot):
        return pltpu.make_async_copy(h_hbm.at[seq_i], accs[slot], load_sem.at[slot])

    def store(seq_i, slot):
        return pltpu.make_async_copy(accs[slot], o_hbm.at[seq_i], store_sem.at[slot])

    def run(slot):
        acc = accs[slot]
        other = 1 - slot

        @pl.when((b == 0) & (e == 0))
        def _():
            load(b, slot).start()

        @pl.when(e == 0)
        def _():
            load(b, slot).wait()

        @pl.when((e == SCATTER_PREFETCH_STEP) & (b >= 1))
        def _():
            store(b - 1, other).wait()

        @pl.when((e == SCATTER_PREFETCH_STEP) & (b + 1 < n_b))
        def _():
            load(b + 1, other).start()

        def add_rows(slots):
            rows, vals = [], []
            for s in slots:
                t = idx_ref[0, 0, s]
                gate = aff_ref[0, 0, t]
                r0 = pl.multiple_of(t * SUBLANES, SUBLANES)
                s0 = (s * SUBLANES if isinstance(s, int)
                      else pl.multiple_of(s * SUBLANES, SUBLANES))
                vals.append(acc[pl.ds(r0, SUBLANES), :] + gate * ys[pl.ds(s0, SUBLANES), :])
                rows.append(r0)
            for r0, v in zip(rows, vals):
                acc[pl.ds(r0, SUBLANES), :] = v

        def body(i, carry):
            add_rows([i * SCATTER_GROUP + k for k in range(SCATTER_GROUP)])
            return carry

        n_groups = cap // SCATTER_GROUP
        lax.fori_loop(0, n_groups, body, 0)
        if cap % SCATTER_GROUP:
            add_rows(list(range(n_groups * SCATTER_GROUP, cap)))

        @pl.when(e == n_e - 1)
        def _():
            store(b, slot).start()

        @pl.when((e == n_e - 1) & (b == n_b - 1))
        def _():
            store(b, slot).wait()

    for slot in range(2):
        pl.when(b % 2 == slot)(functools.partial(run, slot))


def _scatter(idx_smem, aff_smem, ys, h, *, cap, cap_pad):
    b = h.shape[0]
    slots_pad = idx_smem.shape[-1]
    seq_pad = aff_smem.shape[-1]
    assert 1 <= SCATTER_PREFETCH_STEP < N_EXPERTS - 1
    return pl.pallas_call(
        functools.partial(_scatter_kernel, cap=cap),
        grid=(b, N_EXPERTS),
        in_specs=[
            pl.BlockSpec((1, 1, slots_pad), lambda bi, e: (bi * N_EXPERTS + e, 0, 0),
                         memory_space=pltpu.SMEM),
            pl.BlockSpec((1, 1, seq_pad), lambda bi, e: (bi * N_EXPERTS + e, 0, 0),
                         memory_space=pltpu.SMEM),
            pl.BlockSpec((1, 1, cap_pad * SUBLANES, LANES), lambda bi, e: (e, bi, 0, 0)),
            pl.BlockSpec(memory_space=pl.ANY),
        ],
        out_specs=pl.BlockSpec(memory_space=pl.ANY),
        out_shape=jax.ShapeDtypeStruct(h.shape, F32),
        scratch_shapes=[pltpu.VMEM(h.shape[1:], F32), pltpu.VMEM(h.shape[1:], F32),
                        pltpu.SemaphoreType.DMA((2,)), pltpu.SemaphoreType.DMA((2,))],
        compiler_params=_params(56, "arbitrary", "arbitrary"),
        name="moe_scatter",
    )(idx_smem, aff_smem, ys, h)


def _moe_layer(h, aff_t, g, w_gate, w_up, w_down, layer, *, seq):
    b = h.shape[0]
    cap = CAPACITY_FACTOR * seq // N_EXPERTS
    cap_pad = _round_up(cap, BF16_ROWS)
    assert (b * cap_pad) % (FFN_ROW_TILES * BF16_ROWS) == 0
    slots_pad = _round_up(cap_pad, LANES)
    seq_pad = aff_t.shape[-1]
    idx = _topk(aff_t, seq=seq, cap=cap, cap_pad=cap_pad, slots_pad=slots_pad)
    idx_smem = idx.reshape(b * N_EXPERTS, 1, slots_pad)
    aff_smem = aff_t.reshape(b * N_EXPERTS, 1, seq_pad)
    xs = _gather(idx_smem, h, g, cap_pad=cap_pad)
    ys = _ffn(xs.reshape(N_EXPERTS, b * cap_pad, D_MODEL), w_gate, w_up, w_down, layer)
    ys = ys.reshape(N_EXPERTS, b, cap_pad * SUBLANES, LANES)
    return _scatter(idx_smem, aff_smem, ys, h, cap=cap, cap_pad=cap_pad)


def _final_kernel(ha_ref, hb_ref, g_ref, o_ref, *, tt):
    ha = ha_ref.at[0]
    hb = hb_ref.at[0]
    ch = [jnp.concatenate(
        [ha[pl.ds(N_META * SUBLANES + j, tt - N_META, stride=SUBLANES), :],
         hb[pl.ds(j, N_META, stride=SUBLANES), :]], axis=0) for j in range(N_CHUNKS)]
    o_ref[0] = jnp.concatenate(_rms_chunks(ch, g_ref), axis=1)


def _final(h, g, *, seq, tt):
    b = h.shape[0]
    s_out = seq - N_META
    assert s_out % tt == 0
    assert tt % N_META == 0
    return pl.pallas_call(
        functools.partial(_final_kernel, tt=tt),
        grid=(b, s_out // tt),
        in_specs=[
            pl.BlockSpec((1, tt * SUBLANES, LANES), lambda bi, i: (bi, i, 0)),
            pl.BlockSpec((1, N_META * SUBLANES, LANES),
                         lambda bi, i: (bi, (i + 1) * (tt // N_META), 0)),
            pl.BlockSpec((1, D_MODEL), lambda bi, i: (0, 0)),
        ],
        out_specs=pl.BlockSpec((1, tt, D_MODEL), lambda bi, i: (bi, i, 0)),
        out_shape=jax.ShapeDtypeStruct((b, s_out, D_MODEL), F32),
        compiler_params=_params(40, "parallel", "parallel"),
        name="final_norm",
    )(h, h, g.reshape(1, D_MODEL))


def kernel(x, meta_tokens, norm_mix, norm_ffn, norm_final, pool_w, pool_scale, lru_w_in, lru_conv_w, lru_conv_b, lru_w_gates, lru_b_gates, lru_lambda, lru_w_out, router_w, moe_w_gate, moe_w_up, moe_w_down):
    b, s, d = x.shape
    assert d == D_MODEL
    seq = s + N_META
    tt = TIME_TILE
    depth = norm_mix.shape[0]
    h = _embed(x, meta_tokens.astype(x.dtype), tt=tt)
    for i in range(depth):
        j = i // 2
        if i % 2 == 0:
            h, aff_t = _pool_layer(h, norm_mix[i], pool_w[j], pool_scale[j],
                                   norm_ffn[i], router_w[i], seq=seq, tt=tt)
        else:
            h, aff_t = _lru_layer(h, norm_mix[i], lru_w_in[j], lru_conv_w[j], lru_conv_b[j],
                                  lru_w_gates[j], lru_b_gates[j], lru_lambda[j], lru_w_out[j],
                                  norm_ffn[i], router_w[i], seq=seq, tt=tt)
        h = _moe_layer(h, aff_t, norm_ffn[i], moe_w_gate, moe_w_up, moe_w_down, i, seq=seq)
    return _final(h, norm_final, seq=seq, tt=tt)
```

```python
import functools

import jax
import jax.numpy as jnp
from jax import lax
from jax.experimental import pallas as pl
from jax.experimental.pallas import tpu as pltpu

F32 = jnp.float32
BF16 = jnp.bfloat16

LANES = 128
SUBLANES = 8
BF16_ROWS = 16
D_MODEL = 1024
N_CHUNKS = D_MODEL // LANES
assert N_CHUNKS == SUBLANES
N_META = 16
POOL_WINDOWS = (2, 4, 8, 16)
POOL_GROUP = D_MODEL // len(POOL_WINDOWS)
CHUNKS_PER_GROUP = POOL_GROUP // LANES
POOL_HALO = 8
LRU_HEADS = 4
LRU_HEAD_DIM = D_MODEL // LRU_HEADS
CHUNKS_PER_HEAD = LRU_HEAD_DIM // LANES
LRU_C = 8.0
CONV_WIDTH = 4
CONV_LEFT = 1
CONV_RIGHT = CONV_WIDTH - 1 - CONV_LEFT
N_EXPERTS = 16
CAPACITY_FACTOR = 2
RMS_EPS = 1e-6
TIME_TILE = 512
FFN_ROW_TILES = 4
FFN_F_TILE = 1024
LRU_IN_TILE = 1024
GATHER_GROUP_MAX = 48
SCATTER_PREFETCH_STEP = 3
SCATTER_GROUP = 8
COUNT_RADIX = 32.0
MIB = 1024 * 1024


def _cdiv(a, b):
    return -(-a // b)


def _round_up(a, b):
    return _cdiv(a, b) * b


def _chunks(ref2d, tok0, n):
    return [ref2d[pl.ds(tok0 * SUBLANES + j, n, stride=SUBLANES), :]
            for j in range(N_CHUNKS)]


def _store_chunks(ref2d, tok0, n, chunks):
    for j in range(N_CHUNKS):
        ref2d[pl.ds(tok0 * SUBLANES + j, n, stride=SUBLANES), :] = chunks[j]


def _lane_chunk(ref, j):
    return ref[:, j * LANES:(j + 1) * LANES]


def _rms_chunks(chunks, g_ref):
    ss = chunks[0] * chunks[0]
    for c in chunks[1:]:
        ss = ss + c * c
    ms = jnp.sum(ss, axis=-1, keepdims=True) * (1.0 / D_MODEL)
    inv = lax.rsqrt(ms + RMS_EPS)
    return [c * inv * _lane_chunk(g_ref, j) for j, c in enumerate(chunks)]


def _token_ids(tok0, n):
    return tok0 + lax.broadcasted_iota(jnp.int32, (n, 1), 0)


def _params(vmem_mib, *sem):
    return pltpu.CompilerParams(dimension_semantics=sem,
                                vmem_limit_bytes=vmem_mib * MIB)


def _embed_kernel(meta_ref, xa_ref, xb_ref, o_ref, *, tt):
    i = pl.program_id(1)
    out = o_ref.at[0]
    body = xb_ref[0, :tt - N_META, :]
    _store_chunks(out, N_META, tt - N_META,
                  [body[:, j * LANES:(j + 1) * LANES] for j in range(N_CHUNKS)])

    def head(src):
        _store_chunks(out, 0, N_META,
                      [src[:, j * LANES:(j + 1) * LANES] for j in range(N_CHUNKS)])

    @pl.when(i == 0)
    def _():
        head(meta_ref[...])

    @pl.when(i > 0)
    def _():
        head(xa_ref[0])


def _embed(x, meta, *, tt):
    b, s, _ = x.shape
    seq = s + N_META
    assert s % tt == 0 and tt % N_META == 0
    last = s // tt - 1
    metas_per_tile = tt // N_META
    return pl.pallas_call(
        functools.partial(_embed_kernel, tt=tt),
        grid=(b, _cdiv(seq, tt)),
        in_specs=[
            pl.BlockSpec((N_META, D_MODEL), lambda bi, i: (0, 0)),
            pl.BlockSpec((1, N_META, D_MODEL),
                         lambda bi, i: (bi, jnp.maximum(i * metas_per_tile - 1, 0), 0)),
            pl.BlockSpec((1, tt, D_MODEL), lambda bi, i: (bi, jnp.minimum(i, last), 0)),
        ],
        out_specs=pl.BlockSpec((1, tt * SUBLANES, LANES), lambda bi, i: (bi, i, 0)),
        out_shape=jax.ShapeDtypeStruct((b, seq * SUBLANES, LANES), F32),
        compiler_params=_params(40, "parallel", "parallel"),
        name="embed",
    )(meta, x, x)


def _pool_kernel(hp_ref, hm_ref, hn_ref, g_ref, w_ref, sc_ref, rw_ref,
                 o_ref, aff_ref, ext_ref, s2_ref, s4_ref, s8_ref, p_ref, *, seq, tt):
    ti = pl.program_id(1)
    t0 = ti * tt
    hm = hm_ref.at[0]
    out = o_ref.at[0]

    def norm_into_ext(src, n, ext_tok0, tok0):
        tok = _token_ids(tok0, n)
        valid = (tok >= 0) & (tok < seq)
        ch = [jnp.where(valid, c, 0.0) for c in _chunks(src, 0, n)]
        _store_chunks(ext_ref, ext_tok0, n, _rms_chunks(ch, g_ref))

    norm_into_ext(hp_ref.at[0], POOL_HALO, 0, t0 - POOL_HALO)
    norm_into_ext(hm, tt, POOL_HALO, t0)
    norm_into_ext(hn_ref.at[0], POOL_HALO, POOL_HALO + tt, t0 + tt)

    def rows(a, b):
        return pl.ds(a * SUBLANES, (b - a) * SUBLANES)

    lo, hi = 1, tt + 2 * POOL_HALO
    s2_ref[rows(lo, hi), :] = ext_ref[rows(lo - 1, hi - 1), :] + ext_ref[rows(lo, hi), :]
    prev, half = s2_ref, 1
    for nxt in (s4_ref, s8_ref):
        lo, hi = lo + half, hi - half
        nxt[rows(lo, hi), :] = (prev[rows(lo - half, hi - half), :]
                                + prev[rows(lo + half, hi + half), :])
        prev, half = nxt, 2 * half
    m0, m1 = POOL_HALO, POOL_HALO + tt
    assert lo + half <= m0 and m1 <= hi - half and 4 * half == POOL_WINDOWS[-1]

    grp = lax.broadcasted_iota(jnp.int32, (SUBLANES, LANES), 0) // CHUNKS_PER_GROUP

    def window_sums():
        def tiles(v):
            return v.reshape(tt, SUBLANES, LANES)

        s16 = prev[rows(m0 - half, m1 - half), :] + prev[rows(m0 + half, m1 + half), :]
        return jnp.where(grp == 0, tiles(s2_ref[rows(m0, m1), :]),
                         jnp.where(grp == 1, tiles(s4_ref[rows(m0, m1), :]),
                                   jnp.where(grp == 2, tiles(s8_ref[rows(m0, m1), :]),
                                             tiles(s16))))

    near_end = (ti == 0) | (t0 + tt > seq - POOL_HALO)

    @pl.when(jnp.logical_not(near_end))
    def _():
        inv_win = jnp.where(grp == 0, 0.5, jnp.where(grp == 1, 0.25,
                                                     jnp.where(grp == 2, 0.125, 0.0625)))
        p_ref[...] = ((window_sums() * inv_win).reshape(tt * SUBLANES, LANES)
                      - ext_ref[rows(m0, m1), :])

    @pl.when(near_end)
    def _():
        r = lax.broadcasted_iota(jnp.int32, (tt * SUBLANES, 1), 0)
        tok = t0 + (r >> 3)
        left = lax.shift_left(jnp.int32(1), (r & (SUBLANES - 1)) // CHUNKS_PER_GROUP)
        cnt = jnp.minimum(tok + left, seq) - jnp.maximum(tok - left, 0)
        cnt = jnp.maximum(cnt, 1).astype(F32)
        p_ref[...] = (window_sums().reshape(tt * SUBLANES, LANES) / cnt
                      - ext_ref[rows(m0, m1), :])

    new_chunks = []
    for g in range(len(POOL_WINDOWS)):
        group_chunks = range(g * CHUNKS_PER_GROUP, (g + 1) * CHUNKS_PER_GROUP)
        p = jnp.concatenate([p_ref[pl.ds(c, tt, stride=SUBLANES), :] for c in group_chunks],
                            axis=1).astype(BF16)
        y = jnp.dot(p, w_ref[g], preferred_element_type=F32)
        for q, c in enumerate(group_chunks):
            res = hm[pl.ds(c, tt, stride=SUBLANES), :]
            new_chunks.append(res + y[:, q * LANES:(q + 1) * LANES] * _lane_chunk(sc_ref, c))
            out[pl.ds(c, tt, stride=SUBLANES), :] = new_chunks[-1]
    aff_ref[0] = _route_affinities(new_chunks, t0, rw_ref, seq=seq)


def _pool_layer(h, g, w, scale, g_ffn, router_w, *, seq, tt):
    b = h.shape[0]
    nt = _cdiv(seq, tt)
    halo_rows = POOL_HALO * SUBLANES
    blocks_per_tile = tt // POOL_HALO
    last_halo_block = seq // POOL_HALO - 1
    route_in, route_out, route_shape = _route_specs(b, seq, tt)
    return pl.pallas_call(
        functools.partial(_pool_kernel, seq=seq, tt=tt),
        grid=(b, nt),
        in_specs=[
            pl.BlockSpec((1, halo_rows, LANES),
                         lambda bi, i: (bi, jnp.maximum(i * blocks_per_tile - 1, 0), 0)),
            pl.BlockSpec((1, tt * SUBLANES, LANES), lambda bi, i: (bi, i, 0)),
            pl.BlockSpec((1, halo_rows, LANES),
                         lambda bi, i: (bi, jnp.minimum((i + 1) * blocks_per_tile,
                                                        last_halo_block), 0)),
            pl.BlockSpec((1, D_MODEL), lambda bi, i: (0, 0)),
            pl.BlockSpec((len(POOL_WINDOWS), POOL_GROUP, POOL_GROUP),
                         lambda bi, i: (0, 0, 0)),
            pl.BlockSpec((1, D_MODEL), lambda bi, i: (0, 0)),
        ] + route_in,
        out_specs=[pl.BlockSpec((1, tt * SUBLANES, LANES), lambda bi, i: (bi, i, 0)),
                   route_out],
        out_shape=[jax.ShapeDtypeStruct(h.shape, F32), route_shape],
        scratch_shapes=[pltpu.VMEM(((tt + 2 * POOL_HALO) * SUBLANES, LANES), F32)] * 4
        + [pltpu.VMEM((tt * SUBLANES, LANES), F32)],
        compiler_params=_params(40, "parallel", "arbitrary"),
        name="pool_mixer",
    )(h, h, h, g.reshape(1, D_MODEL), w.astype(BF16), scale.reshape(1, D_MODEL),
      _router_operand(router_w, g_ffn))


def _gelu_tanh(x):
    c = 0.7978845608028654
    return 0.5 * x * (1.0 + jnp.tanh(c * (x + 0.044715 * (x * x * x))))


def _lru_in_kernel(h_ref, g_ref, w_ref, xb_ref, gy_ref, *, tf):
    u = jnp.concatenate(_rms_chunks(_chunks(h_ref, 0, tf), g_ref), axis=1).astype(BF16)
    yb = jnp.dot(u, w_ref[:, D_MODEL:], preferred_element_type=F32)
    gy_ref[...] = _gelu_tanh(yb)
    xb = jnp.dot(u, w_ref[:, :D_MODEL], preferred_element_type=F32)
    _store_chunks(xb_ref, 0, tf,
                  [xb[:, j * LANES:(j + 1) * LANES] for j in range(N_CHUNKS)])


def _lru_in(h_flat, g, w_in, *, tf):
    n_tok = h_flat.shape[0] // SUBLANES
    return pl.pallas_call(
        functools.partial(_lru_in_kernel, tf=tf),
        grid=(_cdiv(n_tok, tf),),
        in_specs=[
            pl.BlockSpec((tf * SUBLANES, LANES), lambda i: (i, 0)),
            pl.BlockSpec((1, D_MODEL), lambda i: (0, 0)),
            pl.BlockSpec((D_MODEL, 2 * D_MODEL), lambda i: (0, 0)),
        ],
        out_specs=[
            pl.BlockSpec((tf * SUBLANES, LANES), lambda i: (i, 0)),
            pl.BlockSpec((tf, D_MODEL), lambda i: (i, 0)),
        ],
        out_shape=[
            jax.ShapeDtypeStruct(h_flat.shape, F32),
            jax.ShapeDtypeStruct((n_tok, D_MODEL), F32),
        ],
        compiler_params=_params(56, "parallel"),
        name="lru_in_proj",
    )(h_flat, g.reshape(1, D_MODEL), w_in.astype(BF16))


def _softplus(x):
    return jnp.maximum(x, 0.0) + jnp.log1p(jnp.exp(-jnp.abs(x)))


def _lru_scan_kernel(*refs, seq, tt, reverse):
    if reverse:
        (xp_ref, xm_ref, xn_ref, cw_ref, cb_ref, wg_ref, bg_ref, lam_ref,
         o_ref, ext_ref, xc_ref, a_ref, b_ref, carry_ref) = refs
        hs = o_ref.at[0]
    else:
        (xp_ref, xm_ref, xn_ref, cw_ref, cb_ref, wg_ref, bg_ref, lam_ref,
         hb_ref, gy_ref, res_ref, wo_ref, rw_ref,
         o_ref, aff_ref, ext_ref, xc_ref, a_ref, b_ref, carry_ref, hs) = refs
    step = pl.program_id(1)
    nt = pl.num_programs(1)
    ti = nt - 1 - step if reverse else step
    t0 = ti * tt

    @pl.when(step == 0)
    def _():
        carry_ref[...] = jnp.zeros_like(carry_ref)

    def masked_rows(src, n_tok, tok0):
        rows = n_tok * SUBLANES
        tok = tok0 + (lax.broadcasted_iota(jnp.int32, (rows, 1), 0) >> 3)
        return jnp.where((tok >= 0) & (tok < seq), src[...], 0.0)

    main_rows = pl.ds(CONV_LEFT * SUBLANES, tt * SUBLANES)
    ext_ref[pl.ds(0, CONV_LEFT * SUBLANES), :] = masked_rows(
        xp_ref.at[0], CONV_LEFT, t0 - CONV_LEFT)
    ext_ref[main_rows, :] = xm_ref[0]
    ext_ref[pl.ds((CONV_LEFT + tt) * SUBLANES, CONV_RIGHT * SUBLANES), :] = masked_rows(
        xn_ref.at[0], CONV_RIGHT, t0 + tt)
    ragged = t0 + tt > seq

    @pl.when(ragged)
    def _():
        ext_ref[main_rows, :] = masked_rows(xm_ref.at[0], tt, t0)

    xc = cb_ref[...]
    for k in range(CONV_WIDTH):
        xc = xc + (ext_ref[pl.ds(k * SUBLANES, tt * SUBLANES), :]
                   .reshape(tt, SUBLANES, LANES) * cw_ref[k])
    xc_ref[...] = xc.reshape(tt * SUBLANES, LANES)

    for hh in range(LRU_HEADS):
        head_chunks = range(hh * CHUNKS_PER_HEAD, (hh + 1) * CHUNKS_PER_HEAD)
        xh = jnp.concatenate([xc_ref[pl.ds(c, tt, stride=SUBLANES), :] for c in head_chunks],
                             axis=1)
        th = jnp.tanh(jnp.dot(xh.astype(BF16), wg_ref[hh], preferred_element_type=F32)
                      + bg_ref[hh])
        lam = lam_ref[:, hh * LRU_HEAD_DIM:(hh + 1) * LRU_HEAD_DIM]
        half_c = (-0.5 * LRU_C) * _softplus(-lam)
        log_a = th[:, :LRU_HEAD_DIM] * half_c + half_c
        a = jnp.exp(log_a)
        bb = (jnp.sqrt(jnp.tanh(-log_a) * (a * a + 1.0))
              * ((th[:, LRU_HEAD_DIM:] + 1.0) * xh))
        for q, c in enumerate(head_chunks):
            a_ref[pl.ds(c, tt, stride=SUBLANES), :] = a[:, q * LANES:(q + 1) * LANES]
            b_ref[pl.ds(c, tt, stride=SUBLANES), :] = bb[:, q * LANES:(q + 1) * LANES]

    @pl.when(ragged)
    def _():
        tok = t0 + (lax.broadcasted_iota(jnp.int32, (tt * SUBLANES, 1), 0) >> 3)
        b_ref[...] = jnp.where(tok < seq, b_ref[...], 0.0)

    def scan_body(i, h):
        for k in range(SUBLANES):
            t = (tt - 1 - (i * SUBLANES + k)) if reverse else (i * SUBLANES + k)
            r0 = pl.multiple_of(t * SUBLANES, SUBLANES)
            h = a_ref[pl.ds(r0, SUBLANES), :] * h + b_ref[pl.ds(r0, SUBLANES), :]
            hs[pl.ds(r0, SUBLANES), :] = h
        return h

    carry_ref[...] = lax.fori_loop(0, tt // SUBLANES, scan_body, carry_ref[...])

    if not reverse:
        hs[...] = hs[...] + hb_ref[0]
        hsum = jnp.concatenate(_chunks(hs, 0, tt), axis=1)
        z = (hsum * gy_ref[0]).astype(BF16)
        y = jnp.dot(z, wo_ref[...], preferred_element_type=F32)
        res = res_ref.at[0]
        new_chunks = [res[pl.ds(j, tt, stride=SUBLANES), :] + y[:, j * LANES:(j + 1) * LANES]
                      for j in range(N_CHUNKS)]
        _store_chunks(o_ref.at[0], 0, tt, new_chunks)
        aff_ref[0] = _route_affinities(new_chunks, t0, rw_ref, seq=seq)


def _lru_scan(xb, conv_w, conv_b, wg, bg, lam, *, seq, tt, reverse,
              hb=None, gy=None, res=None, w_out=None, g_ffn=None, router_w=None):
    b = xb.shape[0]
    nt = _cdiv(seq, tt)
    right_rows = CONV_RIGHT * SUBLANES
    last_right_block = seq // CONV_RIGHT - 1

    def tix(i):
        return nt - 1 - i if reverse else i

    tile_spec = pl.BlockSpec((1, tt * SUBLANES, LANES), lambda bi, i: (bi, tix(i), 0))

    def const_spec(shape):
        return pl.BlockSpec(shape, lambda bi, i: (0,) * len(shape))

    in_specs = [
        pl.BlockSpec((1, CONV_LEFT * SUBLANES, LANES),
                     lambda bi, i: (bi, jnp.maximum(tix(i) * (tt // CONV_LEFT) - 1, 0), 0)),
        tile_spec,
        pl.BlockSpec((1, right_rows, LANES),
                     lambda bi, i: (bi, jnp.minimum((tix(i) + 1) * (tt // CONV_RIGHT),
                                                    last_right_block), 0)),
        const_spec((CONV_WIDTH, SUBLANES, LANES)),
        const_spec((SUBLANES, LANES)),
        const_spec((LRU_HEADS, LRU_HEAD_DIM, 2 * LRU_HEAD_DIM)),
        const_spec((LRU_HEADS, 1, 2 * LRU_HEAD_DIM)),
        const_spec((1, D_MODEL)),
    ]
    args = [xb, xb, xb, (0.5 * conv_w).reshape(CONV_WIDTH, SUBLANES, LANES),
            (0.5 * conv_b).reshape(SUBLANES, LANES), wg, bg, lam.reshape(1, D_MODEL)]
    scratch = [
        pltpu.VMEM(((tt + CONV_WIDTH - 1) * SUBLANES, LANES), F32),
        pltpu.VMEM((tt * SUBLANES, LANES), F32),
        pltpu.VMEM((tt * SUBLANES, LANES), F32),
        pltpu.VMEM((tt * SUBLANES, LANES), F32),
        pltpu.VMEM((SUBLANES, LANES), F32),
    ]
    out_specs = tile_spec
    out_shape = jax.ShapeDtypeStruct(xb.shape, F32)
    if not reverse:
        route_in, route_out, route_shape = _route_specs(b, seq, tt)
        in_specs += [
            tile_spec,
            pl.BlockSpec((1, tt, D_MODEL), lambda bi, i: (bi, i, 0)),
            tile_spec,
            const_spec((D_MODEL, D_MODEL)),
        ] + route_in
        args += [hb, gy, res, w_out, _router_operand(router_w, g_ffn)]
        scratch.append(pltpu.VMEM((tt * SUBLANES, LANES), F32))
        out_specs = [tile_spec, route_out]
        out_shape = [out_shape, route_shape]
    return pl.pallas_call(
        functools.partial(_lru_scan_kernel, seq=seq, tt=tt, reverse=reverse),
        grid=(b, nt),
        in_specs=in_specs,
        out_specs=out_specs,
        out_shape=out_shape,
        scratch_shapes=scratch,
        compiler_params=_params(48, "parallel", "arbitrary"),
        name="lru_scan_rev" if reverse else "lru_scan_fwd_out",
    )(*args)


def _lru_layer(h, g, w_in, conv_w, conv_b, w_gates, b_gates, lam, w_out, g_ffn, router_w,
               *, seq, tt):
    b = h.shape[0]
    xb, gy = _lru_in(h.reshape(b * seq * SUBLANES, LANES), g, w_in, tf=LRU_IN_TILE)
    xb = xb.reshape(h.shape)
    gy = gy.reshape(b, seq, D_MODEL)
    wg = jnp.concatenate([w_gates[:, 0], w_gates[:, 1]], axis=-1).astype(BF16)
    bg = 0.5 * jnp.concatenate([b_gates[:, 0], b_gates[:, 1]], axis=-1)[:, :, None, :]
    scan = functools.partial(_lru_scan, xb, conv_w, conv_b, seq=seq, tt=tt)
    hb = scan(wg[1], bg[1], lam[1], reverse=True)
    return scan(wg[0], bg[0], lam[0], reverse=False,
                hb=hb, gy=gy, res=h, w_out=w_out.astype(BF16),
                g_ffn=g_ffn, router_w=router_w)


def _split_bf16(x):
    hi = x.astype(BF16)
    return hi, (x - hi.astype(F32)).astype(BF16)


def _route_affinities(chunks, tok0, rw_ref, *, seq):
    n = chunks[0].shape[0]
    ss = chunks[0] * chunks[0]
    for c in chunks[1:]:
        ss = ss + c * c
    inv = lax.rsqrt(jnp.sum(ss, axis=-1, keepdims=True) * (1.0 / D_MODEL) + RMS_EPS)
    x_hi, x_lo = _split_bf16(jnp.concatenate(chunks, axis=1))
    p = jnp.dot(x_hi, rw_ref[...], preferred_element_type=F32)
    q = jnp.dot(x_lo, rw_ref[:, :LANES], preferred_element_type=F32)
    logits = (((p[:, :LANES] + p[:, LANES:]) + q) * inv).T[:N_EXPERTS, :]
    tok = tok0 + lax.broadcasted_iota(jnp.int32, (1, n), 1)
    logits = jnp.where(tok < seq, logits, 0.0)
    m = jnp.max(logits, axis=0, keepdims=True)
    ex = jnp.exp(logits - m)
    return ex / jnp.sum(ex, axis=0, keepdims=True)


def _router_operand(router_w, g_ffn):
    w = jnp.pad(g_ffn[:, None] * router_w, ((0, 0), (0, LANES - N_EXPERTS)))
    w_hi, w_lo = _split_bf16(w)
    return jnp.concatenate([w_hi, w_lo], axis=1)


def _route_specs(b, seq, tt):
    seq_pad = _round_up(seq, LANES)
    return ([pl.BlockSpec((D_MODEL, 2 * LANES), lambda bi, i: (0, 0))],
            pl.BlockSpec((1, N_EXPERTS, tt), lambda bi, i: (bi, 0, i)),
            jax.ShapeDtypeStruct((b, N_EXPERTS, seq_pad), F32))


def _cumsum_blocks(x01):
    r = lax.broadcasted_iota(jnp.int32, (LANES, LANES), 0)
    c = lax.broadcasted_iota(jnp.int32, (LANES, LANES), 1)
    tri = (r <= c).astype(BF16)
    local = [jnp.dot(x01[:, j * LANES:(j + 1) * LANES].astype(BF16), tri,
                     preferred_element_type=F32) for j in range(x01.shape[1] // LANES)]
    off = jnp.zeros((x01.shape[0], 1), F32)
    outs = []
    for cs in local:
        outs.append(cs + off)
        off = off + cs[:, LANES - 1:LANES]
    return outs


def _topk_kernel(aff_ref, idx_ref, cs_ref, *, seq, cap, cap_pad, slots_pad):
    aff = aff_ref[0]
    lane = lax.broadcasted_iota(jnp.int32, aff.shape, 1)
    aff = jnp.where(lane < seq, aff, -1.0)
    capf = float(cap)
    n_blk = aff.shape[1] // LANES

    def bit_step(i, cur):
        cand = cur | lax.shift_left(jnp.int32(1), 30 - i)
        candf = lax.bitcast_convert_type(cand, F32)
        cnt = jnp.sum((aff >= candf).astype(F32), axis=1, keepdims=True)
        return jnp.where(cnt >= capf, cand, cur)

    cur = lax.fori_loop(0, 31, bit_step, jnp.zeros((N_EXPERTS, 1), jnp.int32))
    thr = lax.bitcast_convert_type(cur, F32)
    gt = aff > thr
    eq = aff == thr
    need = capf - jnp.sum(gt.astype(F32), axis=1, keepdims=True)
    eq_cnt = jnp.concatenate(_cumsum_blocks(eq.astype(F32)), axis=1)
    sel = gt | (eq & (eq_cnt <= need))

    blocks = _cumsum_blocks(sel.astype(F32))
    for j, cs in enumerate(blocks):
        cs_ref[pl.ds(j * N_EXPERTS, N_EXPERTS), :] = cs
    cs_ref[pl.ds(n_blk * N_EXPERTS, (LANES - n_blk) * N_EXPERTS), :] = jnp.zeros(
        ((LANES - n_blk) * N_EXPERTS, LANES), F32)
    never = jnp.full((N_EXPERTS, LANES - n_blk), 2.0 * COUNT_RADIX * 256, F32)
    blk_end = jnp.concatenate([cs[:, LANES - 1:LANES] for cs in blocks] + [never], axis=1)

    slot = lax.broadcasted_iota(jnp.int32, (cap_pad, 1), 0).astype(F32)
    lane_s = lax.broadcasted_iota(jnp.int32, (cap_pad, LANES), 1).astype(F32)
    lane_e = lax.broadcasted_iota(jnp.int32, (slots_pad, LANES), 1)
    experts = range(N_EXPERTS)
    digits = []
    for e in experts:
        cmat = cs_ref[pl.ds(e, LANES, stride=N_EXPERTS), :]
        hi = jnp.floor(cmat * (1.0 / COUNT_RADIX))
        digits.append(jnp.concatenate([hi, cmat - COUNT_RADIX * hi], axis=1).astype(BF16))
    ones = jnp.ones((LANES, LANES), BF16)

    def count_le(x):
        return jnp.dot((x <= slot).astype(BF16), ones, preferred_element_type=F32)

    full = [count_le(blk_end[e:e + 1, :]) for e in experts]
    pick = [(lane_s == full[e]).astype(BF16) for e in experts]
    rows = [jnp.dot(pick[e], digits[e], preferred_element_type=F32) for e in experts]
    rows = [COUNT_RADIX * r_[:, :LANES] + r_[:, LANES:] for r_ in rows]
    inside = [count_le(rows[e]) for e in experts]
    acc = jnp.zeros((slots_pad, LANES), F32)
    pad_rows = jnp.zeros((slots_pad - cap_pad, LANES), F32)
    for e in experts:
        col = jnp.concatenate([LANES * full[e] + inside[e], pad_rows], axis=0)
        acc = jnp.where(lane_e == e, col, acc)
    idx = acc.T[:N_EXPERTS, :].astype(jnp.int32)
    slot_l = lax.broadcasted_iota(jnp.int32, idx.shape, 1)
    idx_ref[0] = jnp.where(slot_l < cap, idx, 0)


def _topk(aff_t, *, seq, cap, cap_pad, slots_pad):
    b, _, seq_pad = aff_t.shape
    assert seq_pad // LANES <= LANES and cap < COUNT_RADIX * 256
    return pl.pallas_call(
        functools.partial(_topk_kernel, seq=seq, cap=cap, cap_pad=cap_pad,
                          slots_pad=slots_pad),
        grid=(b,),
        in_specs=[pl.BlockSpec((1, N_EXPERTS, seq_pad), lambda bi: (bi, 0, 0))],
        out_specs=pl.BlockSpec((1, N_EXPERTS, slots_pad), lambda bi: (bi, 0, 0)),
        out_shape=jax.ShapeDtypeStruct((b, N_EXPERTS, slots_pad), jnp.int32),
        scratch_shapes=[pltpu.VMEM((LANES * N_EXPERTS, LANES), F32)],
        compiler_params=_params(40, "parallel"),
        name="moe_topk",
    )(aff_t)


def _gather_kernel(idx_ref, h_ref, g_ref, xs_ref, zx_ref, *, cap_pad):
    h = h_ref.at[0]
    group = max(d for d in range(SUBLANES, GATHER_GROUP_MAX + 1, SUBLANES) if cap_pad % d == 0)

    def body(i, carry):
        toks = [idx_ref[0, 0, i * group + k] for k in range(group)]
        tiles = [h[pl.ds(pl.multiple_of(t * SUBLANES, SUBLANES), SUBLANES), :] for t in toks]
        for k, tile in enumerate(tiles):
            s0 = pl.multiple_of((i * group + k) * SUBLANES, SUBLANES)
            zx_ref[pl.ds(s0, SUBLANES), :] = tile
        return carry

    lax.fori_loop(0, cap_pad // group, body, 0)
    u = _rms_chunks(_chunks(zx_ref, 0, cap_pad), g_ref)
    xs_ref[0, 0] = jnp.concatenate(u, axis=1).astype(BF16)


def _gather(idx_smem, h, g, *, cap_pad):
    b = h.shape[0]
    slots_pad = idx_smem.shape[-1]
    return pl.pallas_call(
        functools.partial(_gather_kernel, cap_pad=cap_pad),
        grid=(b, N_EXPERTS),
        in_specs=[
            pl.BlockSpec((1, 1, slots_pad), lambda bi, e: (bi * N_EXPERTS + e, 0, 0),
                         memory_space=pltpu.SMEM),
            pl.BlockSpec((1,) + h.shape[1:], lambda bi, e: (bi, 0, 0)),
            pl.BlockSpec((1, D_MODEL), lambda bi, e: (0, 0)),
        ],
        out_specs=pl.BlockSpec((1, 1, cap_pad, D_MODEL), lambda bi, e: (e, bi, 0, 0)),
        out_shape=jax.ShapeDtypeStruct((N_EXPERTS, b, cap_pad, D_MODEL), BF16),
        scratch_shapes=[pltpu.VMEM((cap_pad * SUBLANES, LANES), F32)],
        compiler_params=_params(56, "parallel", "arbitrary"),
        name="moe_gather",
    )(idx_smem, h, g.reshape(1, D_MODEL))


def _ffn_kernel(x_ref, wg_ref, wu_ref, wd_ref, o_ref, hid_ref, wdb_ref, *, ft, n_f):
    r = pl.program_id(1)
    f = pl.program_id(2)

    @pl.when(r == 0)
    def _():
        wdb_ref[pl.ds(pl.multiple_of(f * ft, ft), ft), :] = wd_ref[0, 0].astype(BF16)

    x = x_ref[0]
    hg = jnp.dot(x, wg_ref[0, 0].astype(BF16), preferred_element_type=F32)
    hu = jnp.dot(x, wu_ref[0, 0].astype(BF16), preferred_element_type=F32)
    hid_ref[f] = (hg * jax.nn.sigmoid(hg) * hu).astype(BF16)

    @pl.when(f == n_f - 1)
    def _():
        hid = jnp.concatenate([hid_ref[k] for k in range(n_f)], axis=1)
        y = jnp.dot(hid, wdb_ref[...], preferred_element_type=F32)
        _store_chunks(o_ref.at[0], 0, y.shape[0],
                      [y[:, j * LANES:(j + 1) * LANES] for j in range(N_CHUNKS)])


def _ffn(xs, w_gate, w_up, w_down, layer):
    n_exp, rows, _ = xs.shape
    d_expert = w_gate.shape[-1]
    rt = rows // FFN_ROW_TILES
    ft = min(FFN_F_TILE, d_expert)
    n_f = d_expert // ft

    def wd_tile(e, r, f):
        return (layer, e, jnp.where(r == 0, f, n_f - 1), 0)

    return pl.pallas_call(
        functools.partial(_ffn_kernel, ft=ft, n_f=n_f),
        grid=(n_exp, FFN_ROW_TILES, n_f),
        in_specs=[
            pl.BlockSpec((1, rt, D_MODEL), lambda e, r, f: (e, r, 0)),
            pl.BlockSpec((1, 1, D_MODEL, ft), lambda e, r, f: (layer, e, 0, f)),
            pl.BlockSpec((1, 1, D_MODEL, ft), lambda e, r, f: (layer, e, 0, f)),
            pl.BlockSpec((1, 1, ft, D_MODEL), wd_tile),
        ],
        out_specs=pl.BlockSpec((1, rt * SUBLANES, LANES), lambda e, r, f: (e, r, 0)),
        out_shape=jax.ShapeDtypeStruct((n_exp, rows * SUBLANES, LANES), F32),
        scratch_shapes=[pltpu.VMEM((n_f, rt, ft), BF16),
                        pltpu.VMEM((d_expert, D_MODEL), BF16)],
        compiler_params=_params(60, "parallel", "arbitrary", "arbitrary"),
        name="moe_ffn",
    )(xs, w_gate, w_up, w_down)


def _scatter_kernel(idx_ref, aff_ref, ys_ref, h_hbm, o_hbm, acc0, acc1, load_sem, store_sem,
                    *, cap):
    b = pl.program_id(0)
    e = pl.program_id(1)
    n_b = pl.num_programs(0)
    n_e = pl.num_programs(1)
    accs = (acc0, acc1)
    ys = ys_ref.at[0, 0]

    def load(seq_i, slot):
        return pltpu.make_async_copy(h_hbm.at[seq_i], accs[slot], load_sem.at[slot])

    def store(seq_i, slot):
        return pltpu.make_async_copy(accs[slot], o_hbm.at[seq_i], store_sem.at[slot])

    def run(slot):
        acc = accs[slot]
        other = 1 - slot

        @pl.when((b == 0) & (e == 0))
        def _():
            load(b, slot).start()

        @pl.when(e == 0)
        def _():
            load(b, slot).wait()

        @pl.when((e == SCATTER_PREFETCH_STEP) & (b >= 1))
        def _():
            store(b - 1, other).wait()

        @pl.when((e == SCATTER_PREFETCH_STEP) & (b + 1 < n_b))
        def _():
            load(b + 1, other).start()

        def add_rows(slots):
            rows, vals = [], []
            for s in slots:
                t = idx_ref[0, 0, s]
                gate = aff_ref[0, 0, t]
                r0 = pl.multiple_of(t * SUBLANES, SUBLANES)
                s0 = (s * SUBLANES if isinstance(s, int)
                      else pl.multiple_of(s * SUBLANES, SUBLANES))
                vals.append(acc[pl.ds(r0, SUBLANES), :] + gate * ys[pl.ds(s0, SUBLANES), :])
                rows.append(r0)
            for r0, v in zip(rows, vals):
                acc[pl.ds(r0, SUBLANES), :] = v

        def body(i, carry):
            add_rows([i * SCATTER_GROUP + k for k in range(SCATTER_GROUP)])
            return carry

        n_groups = cap // SCATTER_GROUP
        lax.fori_loop(0, n_groups, body, 0)
        if cap % SCATTER_GROUP:
            add_rows(list(range(n_groups * SCATTER_GROUP, cap)))

        @pl.when(e == n_e - 1)
        def _():
            store(b, slot).start()

        @pl.when((e == n_e - 1) & (b == n_b - 1))
        def _():
            store(b, slot).wait()

    for slot in range(2):
        pl.when(b % 2 == slot)(functools.partial(run, slot))


def _scatter(idx_smem, aff_smem, ys, h, *, cap, cap_pad):
    b = h.shape[0]
    slots_pad = idx_smem.shape[-1]
    seq_pad = aff_smem.shape[-1]
    assert 1 <= SCATTER_PREFETCH_STEP < N_EXPERTS - 1
    return pl.pallas_call(
        functools.partial(_scatter_kernel, cap=cap),
        grid=(b, N_EXPERTS),
        in_specs=[
            pl.BlockSpec((1, 1, slots_pad), lambda bi, e: (bi * N_EXPERTS + e, 0, 0),
                         memory_space=pltpu.SMEM),
            pl.BlockSpec((1, 1, seq_pad), lambda bi, e: (bi * N_EXPERTS + e, 0, 0),
                         memory_space=pltpu.SMEM),
            pl.BlockSpec((1, 1, cap_pad * SUBLANES, LANES), lambda bi, e: (e, bi, 0, 0)),
            pl.BlockSpec(memory_space=pl.ANY),
        ],
        out_specs=pl.BlockSpec(memory_space=pl.ANY),
        out_shape=jax.ShapeDtypeStruct(h.shape, F32),
        scratch_shapes=[pltpu.VMEM(h.shape[1:], F32), pltpu.VMEM(h.shape[1:], F32),
                        pltpu.SemaphoreType.DMA((2,)), pltpu.SemaphoreType.DMA((2,))],
        compiler_params=_params(56, "arbitrary", "arbitrary"),
        name="moe_scatter",
    )(idx_smem, aff_smem, ys, h)


def _moe_layer(h, aff_t, g, w_gate, w_up, w_down, layer, *, seq):
    b = h.shape[0]
    cap = CAPACITY_FACTOR * seq // N_EXPERTS
    cap_pad = _round_up(cap, BF16_ROWS)
    assert (b * cap_pad) % (FFN_ROW_TILES * BF16_ROWS) == 0
    slots_pad = _round_up(cap_pad, LANES)
    seq_pad = aff_t.shape[-1]
    idx = _topk(aff_t, seq=seq, cap=cap, cap_pad=cap_pad, slots_pad=slots_pad)
    idx_smem = idx.reshape(b * N_EXPERTS, 1, slots_pad)
    aff_smem = aff_t.reshape(b * N_EXPERTS, 1, seq_pad)
    xs = _gather(idx_smem, h, g, cap_pad=cap_pad)
    ys = _ffn(xs.reshape(N_EXPERTS, b * cap_pad, D_MODEL), w_gate, w_up, w_down, layer)
    ys = ys.reshape(N_EXPERTS, b, cap_pad * SUBLANES, LANES)
    return _scatter(idx_smem, aff_smem, ys, h, cap=cap, cap_pad=cap_pad)


def _final_kernel(ha_ref, hb_ref, g_ref, o_ref, *, tt):
    ha = ha_ref.at[0]
    hb = hb_ref.at[0]
    ch = [jnp.concatenate(
        [ha[pl.ds(N_META * SUBLANES + j, tt - N_META, stride=SUBLANES), :],
         hb[pl.ds(j, N_META, stride=SUBLANES), :]], axis=0) for j in range(N_CHUNKS)]
    o_ref[0] = jnp.concatenate(_rms_chunks(ch, g_ref), axis=1)


def _final(h, g, *, seq, tt):
    b = h.shape[0]
    s_out = seq - N_META
    assert s_out % tt == 0
    assert tt % N_META == 0
    return pl.pallas_call(
        functools.partial(_final_kernel, tt=tt),
        grid=(b, s_out // tt),
        in_specs=[
            pl.BlockSpec((1, tt * SUBLANES, LANES), lambda bi, i: (bi, i, 0)),
            pl.BlockSpec((1, N_META * SUBLANES, LANES),
                         lambda bi, i: (bi, (i + 1) * (tt // N_META), 0)),
            pl.BlockSpec((1, D_MODEL), lambda bi, i: (0, 0)),
        ],
        out_specs=pl.BlockSpec((1, tt, D_MODEL), lambda bi, i: (bi, i, 0)),
        out_shape=jax.ShapeDtypeStruct((b, s_out, D_MODEL), F32),
        compiler_params=_params(40, "parallel", "parallel"),
        name="final_norm",
    )(h, h, g.reshape(1, D_MODEL))


def kernel(x, meta_tokens, norm_mix, norm_ffn, norm_final, pool_w, pool_scale, lru_w_in, lru_conv_w, lru_conv_b, lru_w_gates, lru_b_gates, lru_lambda, lru_w_out, router_w, moe_w_gate, moe_w_up, moe_w_down):
    b, s, d = x.shape
    assert d == D_MODEL
    seq = s + N_META
    tt = TIME_TILE
    depth = norm_mix.shape[0]
    h = _embed(x, meta_tokens.astype(x.dtype), tt=tt)
    for i in range(depth):
        j = i // 2
        if i % 2 == 0:
            h, aff_t = _pool_layer(h, norm_mix[i], pool_w[j], pool_scale[j],
                                   norm_ffn[i], router_w[i], seq=seq, tt=tt)
        else:
            h, aff_t = _lru_layer(h, norm_mix[i], lru_w_in[j], lru_conv_w[j], lru_conv_b[j],
                                  lru_w_gates[j], lru_b_gates[j], lru_lambda[j], lru_w_out[j],
                                  norm_ffn[i], router_w[i], seq=seq, tt=tt)
        h = _moe_layer(h, aff_t, norm_ffn[i], moe_w_gate, moe_w_up, moe_w_down, i, seq=seq)
    return _final(h, norm_final, seq=seq, tt=tt)
```

```python
import functools

import jax
import jax.numpy as jnp
from jax import lax
from jax.experimental import pallas as pl
from jax.experimental.pallas import tpu as pltpu

F32 = jnp.float32
BF16 = jnp.bfloat16

LANES = 128
SUBLANES = 8
BF16_ROWS = 16
D_MODEL = 1024
N_CHUNKS = D_MODEL // LANES
assert N_CHUNKS == SUBLANES
N_META = 16
POOL_WINDOWS = (2, 4, 8, 16)
POOL_GROUP = D_MODEL // len(POOL_WINDOWS)
CHUNKS_PER_GROUP = POOL_GROUP // LANES
POOL_HALO = 8
LRU_HEADS = 4
LRU_HEAD_DIM = D_MODEL // LRU_HEADS
CHUNKS_PER_HEAD = LRU_HEAD_DIM // LANES
LRU_C = 8.0
CONV_WIDTH = 4
CONV_LEFT = 1
CONV_RIGHT = CONV_WIDTH - 1 - CONV_LEFT
N_EXPERTS = 16
CAPACITY_FACTOR = 2
RMS_EPS = 1e-6
TIME_TILE = 512
FFN_ROW_TILES = 4
FFN_F_TILE = 1024
LRU_IN_TILE = 1024
GATHER_GROUP_MAX = 48
EXPERTS_PER_STEP = 4
SMEM_ROWS = 8
SCATTER_PREFETCH_STEP = 1
SCATTER_GROUP = 8
COUNT_RADIX = 32.0
MIB = 1024 * 1024


def _cdiv(a, b):
    return -(-a // b)


def _round_up(a, b):
    return _cdiv(a, b) * b


def _chunks(ref2d, tok0, n):
    return [ref2d[pl.ds(tok0 * SUBLANES + j, n, stride=SUBLANES), :]
            for j in range(N_CHUNKS)]


def _store_chunks(ref2d, tok0, n, chunks):
    for j in range(N_CHUNKS):
        ref2d[pl.ds(tok0 * SUBLANES + j, n, stride=SUBLANES), :] = chunks[j]


def _lane_chunk(ref, j):
    return ref[:, j * LANES:(j + 1) * LANES]


def _rms_chunks(chunks, g_ref):
    ss = chunks[0] * chunks[0]
    for c in chunks[1:]:
        ss = ss + c * c
    ms = jnp.sum(ss, axis=-1, keepdims=True) * (1.0 / D_MODEL)
    inv = lax.rsqrt(ms + RMS_EPS)
    return [c * inv * _lane_chunk(g_ref, j) for j, c in enumerate(chunks)]


def _token_ids(tok0, n):
    return tok0 + lax.broadcasted_iota(jnp.int32, (n, 1), 0)


def _params(vmem_mib, *sem):
    return pltpu.CompilerParams(dimension_semantics=sem,
                                vmem_limit_bytes=vmem_mib * MIB)


def _embed_kernel(meta_ref, xa_ref, xb_ref, o_ref, *, tt):
    i = pl.program_id(1)
    out = o_ref.at[0]
    body = xb_ref[0, :tt - N_META, :]
    _store_chunks(out, N_META, tt - N_META,
                  [body[:, j * LANES:(j + 1) * LANES] for j in range(N_CHUNKS)])

    def head(src):
        _store_chunks(out, 0, N_META,
                      [src[:, j * LANES:(j + 1) * LANES] for j in range(N_CHUNKS)])

    @pl.when(i == 0)
    def _():
        head(meta_ref[...])

    @pl.when(i > 0)
    def _():
        head(xa_ref[0])


def _embed(x, meta, *, tt):
    b, s, _ = x.shape
    seq = s + N_META
    assert s % tt == 0 and tt % N_META == 0
    last = s // tt - 1
    metas_per_tile = tt // N_META
    return pl.pallas_call(
        functools.partial(_embed_kernel, tt=tt),
        grid=(b, _cdiv(seq, tt)),
        in_specs=[
            pl.BlockSpec((N_META, D_MODEL), lambda bi, i: (0, 0)),
            pl.BlockSpec((1, N_META, D_MODEL),
                         lambda bi, i: (bi, jnp.maximum(i * metas_per_tile - 1, 0), 0)),
            pl.BlockSpec((1, tt, D_MODEL), lambda bi, i: (bi, jnp.minimum(i, last), 0)),
        ],
        out_specs=pl.BlockSpec((1, tt * SUBLANES, LANES), lambda bi, i: (bi, i, 0)),
        out_shape=jax.ShapeDtypeStruct((b, seq * SUBLANES, LANES), F32),
        compiler_params=_params(40, "parallel", "parallel"),
        name="embed",
    )(meta, x, x)


def _pool_kernel(hp_ref, hm_ref, hn_ref, g_ref, w_ref, sc_ref, rw_ref,
                 o_ref, aff_ref, ext_ref, s2_ref, s4_ref, s8_ref, p_ref, *, seq, tt):
    ti = pl.program_id(1)
    t0 = ti * tt
    hm = hm_ref.at[0]
    out = o_ref.at[0]

    def norm_into_ext(src, n, ext_tok0, tok0):
        tok = _token_ids(tok0, n)
        valid = (tok >= 0) & (tok < seq)
        ch = [jnp.where(valid, c, 0.0) for c in _chunks(src, 0, n)]
        _store_chunks(ext_ref, ext_tok0, n, _rms_chunks(ch, g_ref))

    norm_into_ext(hp_ref.at[0], POOL_HALO, 0, t0 - POOL_HALO)
    norm_into_ext(hm, tt, POOL_HALO, t0)
    norm_into_ext(hn_ref.at[0], POOL_HALO, POOL_HALO + tt, t0 + tt)

    def rows(a, b):
        return pl.ds(a * SUBLANES, (b - a) * SUBLANES)

    lo, hi = 1, tt + 2 * POOL_HALO
    s2_ref[rows(lo, hi), :] = ext_ref[rows(lo - 1, hi - 1), :] + ext_ref[rows(lo, hi), :]
    prev, half = s2_ref, 1
    for nxt in (s4_ref, s8_ref):
        lo, hi = lo + half, hi - half
        nxt[rows(lo, hi), :] = (prev[rows(lo - half, hi - half), :]
                                + prev[rows(lo + half, hi + half), :])
        prev, half = nxt, 2 * half
    m0, m1 = POOL_HALO, POOL_HALO + tt
    assert lo + half <= m0 and m1 <= hi - half and 4 * half == POOL_WINDOWS[-1]

    grp = lax.broadcasted_iota(jnp.int32, (SUBLANES, LANES), 0) // CHUNKS_PER_GROUP

    def window_sums():
        def tiles(v):
            return v.reshape(tt, SUBLANES, LANES)

        s16 = prev[rows(m0 - half, m1 - half), :] + prev[rows(m0 + half, m1 + half), :]
        return jnp.where(grp == 0, tiles(s2_ref[rows(m0, m1), :]),
                         jnp.where(grp == 1, tiles(s4_ref[rows(m0, m1), :]),
                                   jnp.where(grp == 2, tiles(s8_ref[rows(m0, m1), :]),
                                             tiles(s16))))

    near_end = (ti == 0) | (t0 + tt > seq - POOL_HALO)

    @pl.when(jnp.logical_not(near_end))
    def _():
        inv_win = jnp.where(grp == 0, 0.5, jnp.where(grp == 1, 0.25,
                                                     jnp.where(grp == 2, 0.125, 0.0625)))
        p_ref[...] = ((window_sums() * inv_win).reshape(tt * SUBLANES, LANES)
                      - ext_ref[rows(m0, m1), :])

    @pl.when(near_end)
    def _():
        r = lax.broadcasted_iota(jnp.int32, (tt * SUBLANES, 1), 0)
        tok = t0 + (r >> 3)
        left = lax.shift_left(jnp.int32(1), (r & (SUBLANES - 1)) // CHUNKS_PER_GROUP)
        cnt = jnp.minimum(tok + left, seq) - jnp.maximum(tok - left, 0)
        cnt = jnp.maximum(cnt, 1).astype(F32)
        p_ref[...] = (window_sums().reshape(tt * SUBLANES, LANES) / cnt
                      - ext_ref[rows(m0, m1), :])

    new_chunks = []
    for g in range(len(POOL_WINDOWS)):
        group_chunks = range(g * CHUNKS_PER_GROUP, (g + 1) * CHUNKS_PER_GROUP)
        p = jnp.concatenate([p_ref[pl.ds(c, tt, stride=SUBLANES), :] for c in group_chunks],
                            axis=1).astype(BF16)
        y = jnp.dot(p, w_ref[g], preferred_element_type=F32)
        for q, c in enumerate(group_chunks):
            res = hm[pl.ds(c, tt, stride=SUBLANES), :]
            new_chunks.append(res + y[:, q * LANES:(q + 1) * LANES] * _lane_chunk(sc_ref, c))
            out[pl.ds(c, tt, stride=SUBLANES), :] = new_chunks[-1]
    aff_ref[0] = _route_affinities(new_chunks, t0, rw_ref, seq=seq)


def _pool_layer(h, g, w, scale, g_ffn, router_w, *, seq, tt):
    b = h.shape[0]
    nt = _cdiv(seq, tt)
    halo_rows = POOL_HALO * SUBLANES
    blocks_per_tile = tt // POOL_HALO
    last_halo_block = seq // POOL_HALO - 1
    route_in, route_out, route_shape = _route_specs(b, seq, tt)
    return pl.pallas_call(
        functools.partial(_pool_kernel, seq=seq, tt=tt),
        grid=(b, nt),
        in_specs=[
            pl.BlockSpec((1, halo_rows, LANES),
                         lambda bi, i: (bi, jnp.maximum(i * blocks_per_tile - 1, 0), 0)),
            pl.BlockSpec((1, tt * SUBLANES, LANES), lambda bi, i: (bi, i, 0)),
            pl.BlockSpec((1, halo_rows, LANES),
                         lambda bi, i: (bi, jnp.minimum((i + 1) * blocks_per_tile,
                                                        last_halo_block), 0)),
            pl.BlockSpec((1, D_MODEL), lambda bi, i: (0, 0)),
            pl.BlockSpec((len(POOL_WINDOWS), POOL_GROUP, POOL_GROUP),
                         lambda bi, i: (0, 0, 0)),
            pl.BlockSpec((1, D_MODEL), lambda bi, i: (0, 0)),
        ] + route_in,
        out_specs=[pl.BlockSpec((1, tt * SUBLANES, LANES), lambda bi, i: (bi, i, 0)),
                   route_out],
        out_shape=[jax.ShapeDtypeStruct(h.shape, F32), route_shape],
        scratch_shapes=[pltpu.VMEM(((tt + 2 * POOL_HALO) * SUBLANES, LANES), F32)] * 4
        + [pltpu.VMEM((tt * SUBLANES, LANES), F32)],
        compiler_params=_params(40, "parallel", "arbitrary"),
        name="pool_mixer",
    )(h, h, h, g.reshape(1, D_MODEL), w.astype(BF16), scale.reshape(1, D_MODEL),
      _router_operand(router_w, g_ffn))


def _gelu_tanh(x):
    c = 0.7978845608028654
    return 0.5 * x * (1.0 + jnp.tanh(c * (x + 0.044715 * (x * x * x))))


def _lru_in_kernel(h_ref, g_ref, w_ref, xb_ref, gy_ref, *, tf):
    u = jnp.concatenate(_rms_chunks(_chunks(h_ref, 0, tf), g_ref), axis=1).astype(BF16)
    yb = jnp.dot(u, w_ref[:, D_MODEL:], preferred_element_type=F32)
    gy_ref[...] = _gelu_tanh(yb)
    xb = jnp.dot(u, w_ref[:, :D_MODEL], preferred_element_type=F32)
    _store_chunks(xb_ref, 0, tf,
                  [xb[:, j * LANES:(j + 1) * LANES] for j in range(N_CHUNKS)])


def _lru_in(h_flat, g, w_in, *, tf):
    n_tok = h_flat.shape[0] // SUBLANES
    return pl.pallas_call(
        functools.partial(_lru_in_kernel, tf=tf),
        grid=(_cdiv(n_tok, tf),),
        in_specs=[
            pl.BlockSpec((tf * SUBLANES, LANES), lambda i: (i, 0)),
            pl.BlockSpec((1, D_MODEL), lambda i: (0, 0)),
            pl.BlockSpec((D_MODEL, 2 * D_MODEL), lambda i: (0, 0)),
        ],
        out_specs=[
            pl.BlockSpec((tf * SUBLANES, LANES), lambda i: (i, 0)),
            pl.BlockSpec((tf, D_MODEL), lambda i: (i, 0)),
        ],
        out_shape=[
            jax.ShapeDtypeStruct(h_flat.shape, F32),
            jax.ShapeDtypeStruct((n_tok, D_MODEL), F32),
        ],
        compiler_params=_params(56, "parallel"),
        name="lru_in_proj",
    )(h_flat, g.reshape(1, D_MODEL), w_in.astype(BF16))


def _softplus(x):
    return jnp.maximum(x, 0.0) + jnp.log1p(jnp.exp(-jnp.abs(x)))


def _lru_scan_kernel(*refs, seq, tt, reverse):
    if reverse:
        (xp_ref, xm_ref, xn_ref, cw_ref, cb_ref, wg_ref, bg_ref, lam_ref,
         o_ref, ext_ref, xc_ref, a_ref, b_ref, carry_ref) = refs
        hs = o_ref.at[0]
    else:
        (xp_ref, xm_ref, xn_ref, cw_ref, cb_ref, wg_ref, bg_ref, lam_ref,
         hb_ref, gy_ref, res_ref, wo_ref, rw_ref,
         o_ref, aff_ref, ext_ref, xc_ref, a_ref, b_ref, carry_ref, hs) = refs
    step = pl.program_id(1)
    nt = pl.num_programs(1)
    ti = nt - 1 - step if reverse else step
    t0 = ti * tt

    @pl.when(step == 0)
    def _():
        carry_ref[...] = jnp.zeros_like(carry_ref)

    def masked_rows(src, n_tok, tok0):
        rows = n_tok * SUBLANES
        tok = tok0 + (lax.broadcasted_iota(jnp.int32, (rows, 1), 0) >> 3)
        return jnp.where((tok >= 0) & (tok < seq), src[...], 0.0)

    main_rows = pl.ds(CONV_LEFT * SUBLANES, tt * SUBLANES)
    ext_ref[pl.ds(0, CONV_LEFT * SUBLANES), :] = masked_rows(
        xp_ref.at[0], CONV_LEFT, t0 - CONV_LEFT)
    ext_ref[main_rows, :] = xm_ref[0]
    ext_ref[pl.ds((CONV_LEFT + tt) * SUBLANES, CONV_RIGHT * SUBLANES), :] = masked_rows(
        xn_ref.at[0], CONV_RIGHT, t0 + tt)
    ragged = t0 + tt > seq

    @pl.when(ragged)
    def _():
        ext_ref[main_rows, :] = masked_rows(xm_ref.at[0], tt, t0)

    xc = cb_ref[...]
    for k in range(CONV_WIDTH):
        xc = xc + (ext_ref[pl.ds(k * SUBLANES, tt * SUBLANES), :]
                   .reshape(tt, SUBLANES, LANES) * cw_ref[k])
    xc_ref[...] = xc.reshape(tt * SUBLANES, LANES)

    for hh in range(LRU_HEADS):
        head_chunks = range(hh * CHUNKS_PER_HEAD, (hh + 1) * CHUNKS_PER_HEAD)
        xh = jnp.concatenate([xc_ref[pl.ds(c, tt, stride=SUBLANES), :] for c in head_chunks],
                             axis=1)
        th = jnp.tanh(jnp.dot(xh.astype(BF16), wg_ref[hh], preferred_element_type=F32)
                      + bg_ref[hh])
        lam = lam_ref[:, hh * LRU_HEAD_DIM:(hh + 1) * LRU_HEAD_DIM]
        half_c = (-0.5 * LRU_C) * _softplus(-lam)
        log_a = th[:, :LRU_HEAD_DIM] * half_c + half_c
        a = jnp.exp(log_a)
        bb = (jnp.sqrt(jnp.tanh(-log_a) * (a * a + 1.0))
              * ((th[:, LRU_HEAD_DIM:] + 1.0) * xh))
        for q, c in enumerate(head_chunks):
            a_ref[pl.ds(c, tt, stride=SUBLANES), :] = a[:, q * LANES:(q + 1) * LANES]
            b_ref[pl.ds(c, tt, stride=SUBLANES), :] = bb[:, q * LANES:(q + 1) * LANES]

    @pl.when(ragged)
    def _():
        tok = t0 + (lax.broadcasted_iota(jnp.int32, (tt * SUBLANES, 1), 0) >> 3)
        b_ref[...] = jnp.where(tok < seq, b_ref[...], 0.0)

    def scan_body(i, h):
        for k in range(SUBLANES):
            t = (tt - 1 - (i * SUBLANES + k)) if reverse else (i * SUBLANES + k)
            r0 = pl.multiple_of(t * SUBLANES, SUBLANES)
            h = a_ref[pl.ds(r0, SUBLANES), :] * h + b_ref[pl.ds(r0, SUBLANES), :]
            hs[pl.ds(r0, SUBLANES), :] = h
        return h

    carry_ref[...] = lax.fori_loop(0, tt // SUBLANES, scan_body, carry_ref[...])

    if not reverse:
        hs[...] = hs[...] + hb_ref[0]
        hsum = jnp.concatenate(_chunks(hs, 0, tt), axis=1)
        z = (hsum * gy_ref[0]).astype(BF16)
        y = jnp.dot(z, wo_ref[...], preferred_element_type=F32)
        res = res_ref.at[0]
        new_chunks = [res[pl.ds(j, tt, stride=SUBLANES), :] + y[:, j * LANES:(j + 1) * LANES]
                      for j in range(N_CHUNKS)]
        _store_chunks(o_ref.at[0], 0, tt, new_chunks)
        aff_ref[0] = _route_affinities(new_chunks, t0, rw_ref, seq=seq)


def _lru_scan(xb, conv_w, conv_b, wg, bg, lam, *, seq, tt, reverse,
              hb=None, gy=None, res=None, w_out=None, g_ffn=None, router_w=None):
    b = xb.shape[0]
    nt = _cdiv(seq, tt)
    right_rows = CONV_RIGHT * SUBLANES
    last_right_block = seq // CONV_RIGHT - 1

    def tix(i):
        return nt - 1 - i if reverse else i

    tile_spec = pl.BlockSpec((1, tt * SUBLANES, LANES), lambda bi, i: (bi, tix(i), 0))

    def const_spec(shape):
        return pl.BlockSpec(shape, lambda bi, i: (0,) * len(shape))

    in_specs = [
        pl.BlockSpec((1, CONV_LEFT * SUBLANES, LANES),
                     lambda bi, i: (bi, jnp.maximum(tix(i) * (tt // CONV_LEFT) - 1, 0), 0)),
        tile_spec,
        pl.BlockSpec((1, right_rows, LANES),
                     lambda bi, i: (bi, jnp.minimum((tix(i) + 1) * (tt // CONV_RIGHT),
                                                    last_right_block), 0)),
        const_spec((CONV_WIDTH, SUBLANES, LANES)),
        const_spec((SUBLANES, LANES)),
        const_spec((LRU_HEADS, LRU_HEAD_DIM, 2 * LRU_HEAD_DIM)),
        const_spec((LRU_HEADS, 1, 2 * LRU_HEAD_DIM)),
        const_spec((1, D_MODEL)),
    ]
    args = [xb, xb, xb, (0.5 * conv_w).reshape(CONV_WIDTH, SUBLANES, LANES),
            (0.5 * conv_b).reshape(SUBLANES, LANES), wg, bg, lam.reshape(1, D_MODEL)]
    scratch = [
        pltpu.VMEM(((tt + CONV_WIDTH - 1) * SUBLANES, LANES), F32),
        pltpu.VMEM((tt * SUBLANES, LANES), F32),
        pltpu.VMEM((tt * SUBLANES, LANES), F32),
        pltpu.VMEM((tt * SUBLANES, LANES), F32),
        pltpu.VMEM((SUBLANES, LANES), F32),
    ]
    out_specs = tile_spec
    out_shape = jax.ShapeDtypeStruct(xb.shape, F32)
    if not reverse:
        route_in, route_out, route_shape = _route_specs(b, seq, tt)
        in_specs += [
            tile_spec,
            pl.BlockSpec((1, tt, D_MODEL), lambda bi, i: (bi, i, 0)),
            tile_spec,
            const_spec((D_MODEL, D_MODEL)),
        ] + route_in
        args += [hb, gy, res, w_out, _router_operand(router_w, g_ffn)]
        scratch.append(pltpu.VMEM((tt * SUBLANES, LANES), F32))
        out_specs = [tile_spec, route_out]
        out_shape = [out_shape, route_shape]
    return pl.pallas_call(
        functools.partial(_lru_scan_kernel, seq=seq, tt=tt, reverse=reverse),
        grid=(b, nt),
        in_specs=in_specs,
        out_specs=out_specs,
        out_shape=out_shape,
        scratch_shapes=scratch,
        compiler_params=_params(48, "parallel", "arbitrary"),
        name="lru_scan_rev" if reverse else "lru_scan_fwd_out",
    )(*args)


def _lru_layer(h, g, w_in, conv_w, conv_b, w_gates, b_gates, lam, w_out, g_ffn, router_w,
               *, seq, tt):
    b = h.shape[0]
    xb, gy = _lru_in(h.reshape(b * seq * SUBLANES, LANES), g, w_in, tf=LRU_IN_TILE)
    xb = xb.reshape(h.shape)
    gy = gy.reshape(b, seq, D_MODEL)
    wg = jnp.concatenate([w_gates[:, 0], w_gates[:, 1]], axis=-1).astype(BF16)
    bg = 0.5 * jnp.concatenate([b_gates[:, 0], b_gates[:, 1]], axis=-1)[:, :, None, :]
    scan = functools.partial(_lru_scan, xb, conv_w, conv_b, seq=seq, tt=tt)
    hb = scan(wg[1], bg[1], lam[1], reverse=True)
    return scan(wg[0], bg[0], lam[0], reverse=False,
                hb=hb, gy=gy, res=h, w_out=w_out.astype(BF16),
                g_ffn=g_ffn, router_w=router_w)


def _split_bf16(x):
    hi = x.astype(BF16)
    return hi, (x - hi.astype(F32)).astype(BF16)


def _route_affinities(chunks, tok0, rw_ref, *, seq):
    n = chunks[0].shape[0]
    ss = chunks[0] * chunks[0]
    for c in chunks[1:]:
        ss = ss + c * c
    inv = lax.rsqrt(jnp.sum(ss, axis=-1, keepdims=True) * (1.0 / D_MODEL) + RMS_EPS)
    x_hi, x_lo = _split_bf16(jnp.concatenate(chunks, axis=1))
    p = jnp.dot(x_hi, rw_ref[...], preferred_element_type=F32)
    q = jnp.dot(x_lo, rw_ref[:, :LANES], preferred_element_type=F32)
    logits = (((p[:, :LANES] + p[:, LANES:]) + q) * inv).T[:N_EXPERTS, :]
    tok = tok0 + lax.broadcasted_iota(jnp.int32, (1, n), 1)
    logits = jnp.where(tok < seq, logits, 0.0)
    m = jnp.max(logits, axis=0, keepdims=True)
    ex = jnp.exp(logits - m)
    return ex / jnp.sum(ex, axis=0, keepdims=True)


def _router_operand(router_w, g_ffn):
    w = jnp.pad(g_ffn[:, None] * router_w, ((0, 0), (0, LANES - N_EXPERTS)))
    w_hi, w_lo = _split_bf16(w)
    return jnp.concatenate([w_hi, w_lo], axis=1)


def _route_specs(b, seq, tt):
    seq_pad = _round_up(seq, LANES)
    return ([pl.BlockSpec((D_MODEL, 2 * LANES), lambda bi, i: (0, 0))],
            pl.BlockSpec((1, N_EXPERTS, tt), lambda bi, i: (bi, 0, i)),
            jax.ShapeDtypeStruct((b, N_EXPERTS, seq_pad), F32))


def _cumsum_blocks(x01):
    r = lax.broadcasted_iota(jnp.int32, (LANES, LANES), 0)
    c = lax.broadcasted_iota(jnp.int32, (LANES, LANES), 1)
    tri = (r <= c).astype(BF16)
    local = [jnp.dot(x01[:, j * LANES:(j + 1) * LANES].astype(BF16), tri,
                     preferred_element_type=F32) for j in range(x01.shape[1] // LANES)]
    off = jnp.zeros((x01.shape[0], 1), F32)
    outs = []
    for cs in local:
        outs.append(cs + off)
        off = off + cs[:, LANES - 1:LANES]
    return outs


def _topk_kernel(aff_ref, idx_ref, cs_ref, *, seq, cap, cap_pad, slots_pad):
    aff = aff_ref[0]
    lane = lax.broadcasted_iota(jnp.int32, aff.shape, 1)
    aff = jnp.where(lane < seq, aff, -1.0)
    capf = float(cap)
    n_blk = aff.shape[1] // LANES

    def enough(cand):
        candf = lax.bitcast_convert_type(cand, F32)
        return jnp.sum((aff >= candf).astype(F32), axis=1, keepdims=True) >= capf

    def bit_pair_step(i, cur):
        hi = cur | lax.shift_left(jnp.int32(1), 30 - 2 * i)
        lo = cur | lax.shift_left(jnp.int32(1), 29 - 2 * i)
        both = hi | lo
        return jnp.where(enough(hi), jnp.where(enough(both), both, hi),
                         jnp.where(enough(lo), lo, cur))

    cur = lax.fori_loop(0, 15, bit_pair_step, jnp.zeros((N_EXPERTS, 1), jnp.int32))
    cur = jnp.where(enough(cur | 1), cur | 1, cur)
    thr = lax.bitcast_convert_type(cur, F32)
    gt = aff > thr
    eq = aff == thr
    need = capf - jnp.sum(gt.astype(F32), axis=1, keepdims=True)
    eq_cnt = jnp.concatenate(_cumsum_blocks(eq.astype(F32)), axis=1)
    sel = gt | (eq & (eq_cnt <= need))

    blocks = _cumsum_blocks(sel.astype(F32))
    for j, cs in enumerate(blocks):
        cs_ref[pl.ds(j * N_EXPERTS, N_EXPERTS), :] = cs
    cs_ref[pl.ds(n_blk * N_EXPERTS, (LANES - n_blk) * N_EXPERTS), :] = jnp.zeros(
        ((LANES - n_blk) * N_EXPERTS, LANES), F32)
    never = jnp.full((N_EXPERTS, LANES - n_blk), 2.0 * COUNT_RADIX * 256, F32)
    blk_end = jnp.concatenate([cs[:, LANES - 1:LANES] for cs in blocks] + [never], axis=1)

    slot = lax.broadcasted_iota(jnp.int32, (cap_pad, 1), 0).astype(F32)
    lane_s = lax.broadcasted_iota(jnp.int32, (cap_pad, LANES), 1).astype(F32)
    lane_e = lax.broadcasted_iota(jnp.int32, (slots_pad, LANES), 1)
    experts = range(N_EXPERTS)
    digits = []
    for e in experts:
        cmat = cs_ref[pl.ds(e, LANES, stride=N_EXPERTS), :]
        hi = jnp.floor(cmat * (1.0 / COUNT_RADIX))
        digits.append(jnp.concatenate([hi, cmat - COUNT_RADIX * hi], axis=1).astype(BF16))
    ones = jnp.ones((LANES, LANES), BF16)

    def count_le(x):
        return jnp.dot((x <= slot).astype(BF16), ones, preferred_element_type=F32)

    full = [count_le(blk_end[e:e + 1, :]) for e in experts]
    pick = [(lane_s == full[e]).astype(BF16) for e in experts]
    rows = [jnp.dot(pick[e], digits[e], preferred_element_type=F32) for e in experts]
    rows = [COUNT_RADIX * r_[:, :LANES] + r_[:, LANES:] for r_ in rows]
    inside = [count_le(rows[e]) for e in experts]
    acc = jnp.zeros((slots_pad, LANES), F32)
    pad_rows = jnp.zeros((slots_pad - cap_pad, LANES), F32)
    for e in experts:
        col = jnp.concatenate([LANES * full[e] + inside[e], pad_rows], axis=0)
        acc = jnp.where(lane_e == e, col, acc)
    idx = acc.T[:N_EXPERTS, :].astype(jnp.int32)
    slot_l = lax.broadcasted_iota(jnp.int32, idx.shape, 1)
    idx_ref[0] = jnp.where(slot_l < cap, idx, 0)


def _topk(aff_t, *, seq, cap, cap_pad, slots_pad):
    b, _, seq_pad = aff_t.shape
    assert seq_pad // LANES <= LANES and cap < COUNT_RADIX * 256
    return pl.pallas_call(
        functools.partial(_topk_kernel, seq=seq, cap=cap, cap_pad=cap_pad,
                          slots_pad=slots_pad),
        grid=(b,),
        in_specs=[pl.BlockSpec((1, N_EXPERTS, seq_pad), lambda bi: (bi, 0, 0))],
        out_specs=pl.BlockSpec((1, N_EXPERTS, slots_pad), lambda bi: (bi, 0, 0)),
        out_shape=jax.ShapeDtypeStruct((b, N_EXPERTS, slots_pad), jnp.int32),
        scratch_shapes=[pltpu.VMEM((LANES * N_EXPERTS, LANES), F32)],
        compiler_params=_params(40, "parallel"),
        name="moe_topk",
    )(aff_t)


def _smem_row(step, g):
    return (step % (SMEM_ROWS // EXPERTS_PER_STEP)) * EXPERTS_PER_STEP + g


def _smem_spec(width):
    steps_per_block = SMEM_ROWS // EXPERTS_PER_STEP
    blocks_per_seq = N_EXPERTS // SMEM_ROWS
    return pl.BlockSpec((SMEM_ROWS, width),
                        lambda bi, s: (bi * blocks_per_seq + s // steps_per_block, 0),
                        memory_space=pltpu.SMEM)


def _gather_kernel(idx_ref, h_ref, g_ref, xs_ref, zx_ref, *, cap_pad):
    h = h_ref.at[0]
    group = max(d for d in range(SUBLANES, GATHER_GROUP_MAX + 1, SUBLANES) if cap_pad % d == 0)
    for g in range(EXPERTS_PER_STEP):
        row = _smem_row(pl.program_id(1), g)

        def body(i, carry):
            toks = [idx_ref[row, i * group + k] for k in range(group)]
            tiles = [h[pl.ds(pl.multiple_of(t * SUBLANES, SUBLANES), SUBLANES), :]
                     for t in toks]
            for k, tile in enumerate(tiles):
                s0 = pl.multiple_of((i * group + k) * SUBLANES, SUBLANES)
                zx_ref[pl.ds(s0, SUBLANES), :] = tile
            return carry

        lax.fori_loop(0, cap_pad // group, body, 0)
        u = _rms_chunks(_chunks(zx_ref, 0, cap_pad), g_ref)
        xs_ref[g, 0] = jnp.concatenate(u, axis=1).astype(BF16)


def _gather(idx_smem, h, g, *, cap_pad):
    b = h.shape[0]
    slots_pad = idx_smem.shape[-1]
    return pl.pallas_call(
        functools.partial(_gather_kernel, cap_pad=cap_pad),
        grid=(b, N_EXPERTS // EXPERTS_PER_STEP),
        in_specs=[
            _smem_spec(slots_pad),
            pl.BlockSpec((1,) + h.shape[1:], lambda bi, e: (bi, 0, 0)),
            pl.BlockSpec((1, D_MODEL), lambda bi, e: (0, 0)),
        ],
        out_specs=pl.BlockSpec((EXPERTS_PER_STEP, 1, cap_pad, D_MODEL),
                               lambda bi, e: (e, bi, 0, 0)),
        out_shape=jax.ShapeDtypeStruct((N_EXPERTS, b, cap_pad, D_MODEL), BF16),
        scratch_shapes=[pltpu.VMEM((cap_pad * SUBLANES, LANES), F32)],
        compiler_params=_params(56, "parallel", "arbitrary"),
        name="moe_gather",
    )(idx_smem, h, g.reshape(1, D_MODEL))


def _ffn_kernel(x_ref, wg_ref, wu_ref, wd_ref, o_ref, hid_ref, wdb_ref, *, ft, n_f):
    r = pl.program_id(1)
    f = pl.program_id(2)

    @pl.when(r == 0)
    def _():
        wdb_ref[pl.ds(pl.multiple_of(f * ft, ft), ft), :] = wd_ref[0, 0].astype(BF16)

    x = x_ref[0]
    hg = jnp.dot(x, wg_ref[0, 0].astype(BF16), preferred_element_type=F32)
    hu = jnp.dot(x, wu_ref[0, 0].astype(BF16), preferred_element_type=F32)
    hid_ref[f] = (hg * jax.nn.sigmoid(hg) * hu).astype(BF16)

    @pl.when(f == n_f - 1)
    def _():
        hid = jnp.concatenate([hid_ref[k] for k in range(n_f)], axis=1)
        y = jnp.dot(hid, wdb_ref[...], preferred_element_type=F32)
        _store_chunks(o_ref.at[0], 0, y.shape[0],
                      [y[:, j * LANES:(j + 1) * LANES] for j in range(N_CHUNKS)])


def _ffn(xs, w_gate, w_up, w_down, layer):
    n_exp, rows, _ = xs.shape
    d_expert = w_gate.shape[-1]
    rt = rows // FFN_ROW_TILES
    ft = min(FFN_F_TILE, d_expert)
    n_f = d_expert // ft

    def wd_tile(e, r, f):
        return (layer, e, jnp.where(r == 0, f, n_f - 1), 0)

    return pl.pallas_call(
        functools.partial(_ffn_kernel, ft=ft, n_f=n_f),
        grid=(n_exp, FFN_ROW_TILES, n_f),
        in_specs=[
            pl.BlockSpec((1, rt, D_MODEL), lambda e, r, f: (e, r, 0)),
            pl.BlockSpec((1, 1, D_MODEL, ft), lambda e, r, f: (layer, e, 0, f)),
            pl.BlockSpec((1, 1, D_MODEL, ft), lambda e, r, f: (layer, e, 0, f)),
            pl.BlockSpec((1, 1, ft, D_MODEL), wd_tile),
        ],
        out_specs=pl.BlockSpec((1, rt * SUBLANES, LANES), lambda e, r, f: (e, r, 0)),
        out_shape=jax.ShapeDtypeStruct((n_exp, rows * SUBLANES, LANES), F32),
        scratch_shapes=[pltpu.VMEM((n_f, rt, ft), BF16),
                        pltpu.VMEM((d_expert, D_MODEL), BF16)],
        compiler_params=_params(60, "parallel", "arbitrary", "arbitrary"),
        name="moe_ffn",
    )(xs, w_gate, w_up, w_down)


def _scatter_kernel(idx_ref, aff_ref, ys_ref, h_hbm, o_hbm, acc0, acc1, load_sem, store_sem,
                    *, cap):
    b = pl.program_id(0)
    e = pl.program_id(1)
    n_b = pl.num_programs(0)
    n_e = pl.num_programs(1)
    accs = (acc0, acc1)

    def load(seq_i, slot):
        return pltpu.make_async_copy(h_hbm.at[seq_i], accs[slot], load_sem.at[slot])

    def store(seq_i, slot):
        return pltpu.make_async_copy(accs[slot], o_hbm.at[seq_i], store_sem.at[slot])

    def run(slot):
        acc = accs[slot]
        other = 1 - slot

        @pl.when((b == 0) & (e == 0))
        def _():
            load(b, slot).start()

        @pl.when(e == 0)
        def _():
            load(b, slot).wait()

        @pl.when((e == SCATTER_PREFETCH_STEP) & (b >= 1))
        def _():
            store(b - 1, other).wait()

        @pl.when((e == SCATTER_PREFETCH_STEP) & (b + 1 < n_b))
        def _():
            load(b + 1, other).start()

        for g in range(EXPERTS_PER_STEP):
            row = _smem_row(e, g)
            ys = ys_ref.at[g, 0]

            def add_rows(slots):
                rows, vals = [], []
                for s in slots:
                    t = idx_ref[row, s]
                    gate = aff_ref[row, t]
                    r0 = pl.multiple_of(t * SUBLANES, SUBLANES)
                    s0 = (s * SUBLANES if isinstance(s, int)
                          else pl.multiple_of(s * SUBLANES, SUBLANES))
                    vals.append(acc[pl.ds(r0, SUBLANES), :]
                                + gate * ys[pl.ds(s0, SUBLANES), :])
                    rows.append(r0)
                for r0, v in zip(rows, vals):
                    acc[pl.ds(r0, SUBLANES), :] = v

            def body(i, carry):
                add_rows([i * SCATTER_GROUP + k for k in range(SCATTER_GROUP)])
                return carry

            n_groups = cap // SCATTER_GROUP
            lax.fori_loop(0, n_groups, body, 0)
            if cap % SCATTER_GROUP:
                add_rows(list(range(n_groups * SCATTER_GROUP, cap)))

        @pl.when(e == n_e - 1)
        def _():
            store(b, slot).start()

        @pl.when((e == n_e - 1) & (b == n_b - 1))
        def _():
            store(b, slot).wait()

    for slot in range(2):
        pl.when(b % 2 == slot)(functools.partial(run, slot))


def _scatter(idx_smem, aff_smem, ys, h, *, cap, cap_pad):
    b = h.shape[0]
    slots_pad = idx_smem.shape[-1]
    seq_pad = aff_smem.shape[-1]
    n_steps = N_EXPERTS // EXPERTS_PER_STEP
    assert 1 <= SCATTER_PREFETCH_STEP <= n_steps - 1
    return pl.pallas_call(
        functools.partial(_scatter_kernel, cap=cap),
        grid=(b, n_steps),
        in_specs=[
            _smem_spec(slots_pad),
            _smem_spec(seq_pad),
            pl.BlockSpec((EXPERTS_PER_STEP, 1, cap_pad * SUBLANES, LANES),
                         lambda bi, e: (e, bi, 0, 0)),
            pl.BlockSpec(memory_space=pl.ANY),
        ],
        out_specs=pl.BlockSpec(memory_space=pl.ANY),
        out_shape=jax.ShapeDtypeStruct(h.shape, F32),
        scratch_shapes=[pltpu.VMEM(h.shape[1:], F32), pltpu.VMEM(h.shape[1:], F32),
                        pltpu.SemaphoreType.DMA((2,)), pltpu.SemaphoreType.DMA((2,))],
        compiler_params=_params(56, "arbitrary", "arbitrary"),
        name="moe_scatter",
    )(idx_smem, aff_smem, ys, h)


def _moe_layer(h, aff_t, g, w_gate, w_up, w_down, layer, *, seq):
    b = h.shape[0]
    cap = CAPACITY_FACTOR * seq // N_EXPERTS
    cap_pad = _round_up(cap, BF16_ROWS)
    assert (b * cap_pad) % (FFN_ROW_TILES * BF16_ROWS) == 0
    slots_pad = _round_up(cap_pad, LANES)
    seq_pad = aff_t.shape[-1]
    idx = _topk(aff_t, seq=seq, cap=cap, cap_pad=cap_pad, slots_pad=slots_pad)
    idx_smem = idx.reshape(b * N_EXPERTS, slots_pad)
    aff_smem = aff_t.reshape(b * N_EXPERTS, seq_pad)
    xs = _gather(idx_smem, h, g, cap_pad=cap_pad)
    ys = _ffn(xs.reshape(N_EXPERTS, b * cap_pad, D_MODEL), w_gate, w_up, w_down, layer)
    ys = ys.reshape(N_EXPERTS, b, cap_pad * SUBLANES, LANES)
    return _scatter(idx_smem, aff_smem, ys, h, cap=cap, cap_pad=cap_pad)


def _final_kernel(ha_ref, hb_ref, g_ref, o_ref, *, tt):
    ha = ha_ref.at[0]
    hb = hb_ref.at[0]
    ch = [jnp.concatenate(
        [ha[pl.ds(N_META * SUBLANES + j, tt - N_META, stride=SUBLANES), :],
         hb[pl.ds(j, N_META, stride=SUBLANES), :]], axis=0) for j in range(N_CHUNKS)]
    o_ref[0] = jnp.concatenate(_rms_chunks(ch, g_ref), axis=1)


def _final(h, g, *, seq, tt):
    b = h.shape[0]
    s_out = seq - N_META
    assert s_out % tt == 0
    assert tt % N_META == 0
    return pl.pallas_call(
        functools.partial(_final_kernel, tt=tt),
        grid=(b, s_out // tt),
        in_specs=[
            pl.BlockSpec((1, tt * SUBLANES, LANES), lambda bi, i: (bi, i, 0)),
            pl.BlockSpec((1, N_META * SUBLANES, LANES),
                         lambda bi, i: (bi, (i + 1) * (tt // N_META), 0)),
            pl.BlockSpec((1, D_MODEL), lambda bi, i: (0, 0)),
        ],
        out_specs=pl.BlockSpec((1, tt, D_MODEL), lambda bi, i: (bi, i, 0)),
        out_shape=jax.ShapeDtypeStruct((b, s_out, D_MODEL), F32),
        compiler_params=_params(40, "parallel", "parallel"),
        name="final_norm",
    )(h, h, g.reshape(1, D_MODEL))


def kernel(x, meta_tokens, norm_mix, norm_ffn, norm_final, pool_w, pool_scale, lru_w_in, lru_conv_w, lru_conv_b, lru_w_gates, lru_b_gates, lru_lambda, lru_w_out, router_w, moe_w_gate, moe_w_up, moe_w_down):
    b, s, d = x.shape
    assert d == D_MODEL
    seq = s + N_META
    tt = TIME_TILE
    depth = norm_mix.shape[0]
    h = _embed(x, meta_tokens.astype(x.dtype), tt=tt)
    for i in range(depth):
        j = i // 2
        if i % 2 == 0:
            h, aff_t = _pool_layer(h, norm_mix[i], pool_w[j], pool_scale[j],
                                   norm_ffn[i], router_w[i], seq=seq, tt=tt)
        else:
            h, aff_t = _lru_layer(h, norm_mix[i], lru_w_in[j], lru_conv_w[j], lru_conv_b[j],
                                  lru_w_gates[j], lru_b_gates[j], lru_lambda[j], lru_w_out[j],
                                  norm_ffn[i], router_w[i], seq=seq, tt=tt)
        h = _moe_layer(h, aff_t, norm_ffn[i], moe_w_gate, moe_w_up, moe_w_down, i, seq=seq)
    return _final(h, norm_final, seq=seq, tt=tt)
```

```python
import functools

import jax
import jax.numpy as jnp
from jax import lax
from jax.experimental import pallas as pl
from jax.experimental.pallas import tpu as pltpu

F32 = jnp.float32
BF16 = jnp.bfloat16

LANES = 128
SUBLANES = 8
BF16_ROWS = 16
D_MODEL = 1024
N_CHUNKS = D_MODEL // LANES
assert N_CHUNKS == SUBLANES
N_META = 16
POOL_WINDOWS = (2, 4, 8, 16)
POOL_GROUP = D_MODEL // len(POOL_WINDOWS)
CHUNKS_PER_GROUP = POOL_GROUP // LANES
POOL_HALO = 8
LRU_HEADS = 4
LRU_HEAD_DIM = D_MODEL // LRU_HEADS
CHUNKS_PER_HEAD = LRU_HEAD_DIM // LANES
LRU_C = 8.0
CONV_WIDTH = 4
CONV_LEFT = 1
CONV_RIGHT = CONV_WIDTH - 1 - CONV_LEFT
N_EXPERTS = 16
CAPACITY_FACTOR = 2
RMS_EPS = 1e-6
TIME_TILE = 512
FFN_ROW_TILES = 4
FFN_F_TILE = 1024
LRU_IN_TILE = 1024
GATHER_GROUP_MAX = 48
EXPERTS_PER_STEP = 4
SMEM_ROWS = 8
SCATTER_PREFETCH_STEP = 1
SCATTER_GROUP = 8
COUNT_RADIX = 32.0
MIB = 1024 * 1024


def _cdiv(a, b):
    return -(-a // b)


def _round_up(a, b):
    return _cdiv(a, b) * b


def _chunks(ref2d, tok0, n):
    return [ref2d[pl.ds(tok0 * SUBLANES + j, n, stride=SUBLANES), :]
            for j in range(N_CHUNKS)]


def _store_chunks(ref2d, tok0, n, chunks):
    for j in range(N_CHUNKS):
        ref2d[pl.ds(tok0 * SUBLANES + j, n, stride=SUBLANES), :] = chunks[j]


def _lane_chunk(ref, j):
    return ref[:, j * LANES:(j + 1) * LANES]


def _rms_chunks(chunks, g_ref):
    ss = chunks[0] * chunks[0]
    for c in chunks[1:]:
        ss = ss + c * c
    ms = jnp.sum(ss, axis=-1, keepdims=True) * (1.0 / D_MODEL)
    inv = lax.rsqrt(ms + RMS_EPS)
    return [c * inv * _lane_chunk(g_ref, j) for j, c in enumerate(chunks)]


def _token_ids(tok0, n):
    return tok0 + lax.broadcasted_iota(jnp.int32, (n, 1), 0)


def _params(vmem_mib, *sem):
    return pltpu.CompilerParams(dimension_semantics=sem,
                                vmem_limit_bytes=vmem_mib * MIB)


def _embed_kernel(meta_ref, xa_ref, xb_ref, o_ref, *, tt):
    i = pl.program_id(1)
    out = o_ref.at[0]
    body = xb_ref[0, :tt - N_META, :]
    _store_chunks(out, N_META, tt - N_META,
                  [body[:, j * LANES:(j + 1) * LANES] for j in range(N_CHUNKS)])

    def head(src):
        _store_chunks(out, 0, N_META,
                      [src[:, j * LANES:(j + 1) * LANES] for j in range(N_CHUNKS)])

    @pl.when(i == 0)
    def _():
        head(meta_ref[...])

    @pl.when(i > 0)
    def _():
        head(xa_ref[0])


def _embed(x, meta, *, tt):
    b, s, _ = x.shape
    seq = s + N_META
    assert s % tt == 0 and tt % N_META == 0
    last = s // tt - 1
    metas_per_tile = tt // N_META
    return pl.pallas_call(
        functools.partial(_embed_kernel, tt=tt),
        grid=(b, _cdiv(seq, tt)),
        in_specs=[
            pl.BlockSpec((N_META, D_MODEL), lambda bi, i: (0, 0)),
            pl.BlockSpec((1, N_META, D_MODEL),
                         lambda bi, i: (bi, jnp.maximum(i * metas_per_tile - 1, 0), 0)),
            pl.BlockSpec((1, tt, D_MODEL), lambda bi, i: (bi, jnp.minimum(i, last), 0)),
        ],
        out_specs=pl.BlockSpec((1, tt * SUBLANES, LANES), lambda bi, i: (bi, i, 0)),
        out_shape=jax.ShapeDtypeStruct((b, seq * SUBLANES, LANES), F32),
        compiler_params=_params(40, "parallel", "parallel"),
        name="embed",
    )(meta, x, x)


def _pool_kernel(hp_ref, hm_ref, hn_ref, g_ref, w_ref, sc_ref, rw_ref,
                 o_ref, aff_ref, ext_ref, s2_ref, s4_ref, s8_ref, p_ref, *, seq, tt):
    ti = pl.program_id(1)
    t0 = ti * tt
    hm = hm_ref.at[0]
    out = o_ref.at[0]

    def norm_into_ext(src, n, ext_tok0, tok0):
        tok = _token_ids(tok0, n)
        valid = (tok >= 0) & (tok < seq)
        ch = [jnp.where(valid, c, 0.0) for c in _chunks(src, 0, n)]
        _store_chunks(ext_ref, ext_tok0, n, _rms_chunks(ch, g_ref))

    norm_into_ext(hp_ref.at[0], POOL_HALO, 0, t0 - POOL_HALO)
    norm_into_ext(hm, tt, POOL_HALO, t0)
    norm_into_ext(hn_ref.at[0], POOL_HALO, POOL_HALO + tt, t0 + tt)

    def rows(a, b):
        return pl.ds(a * SUBLANES, (b - a) * SUBLANES)

    lo, hi = 1, tt + 2 * POOL_HALO
    s2_ref[rows(lo, hi), :] = ext_ref[rows(lo - 1, hi - 1), :] + ext_ref[rows(lo, hi), :]
    prev, half = s2_ref, 1
    for nxt in (s4_ref, s8_ref):
        lo, hi = lo + half, hi - half
        nxt[rows(lo, hi), :] = (prev[rows(lo - half, hi - half), :]
                                + prev[rows(lo + half, hi + half), :])
        prev, half = nxt, 2 * half
    m0, m1 = POOL_HALO, POOL_HALO + tt
    assert lo + half <= m0 and m1 <= hi - half and 4 * half == POOL_WINDOWS[-1]

    grp = lax.broadcasted_iota(jnp.int32, (SUBLANES, LANES), 0) // CHUNKS_PER_GROUP

    def window_sums():
        def tiles(v):
            return v.reshape(tt, SUBLANES, LANES)

        s16 = prev[rows(m0 - half, m1 - half), :] + prev[rows(m0 + half, m1 + half), :]
        return jnp.where(grp == 0, tiles(s2_ref[rows(m0, m1), :]),
                         jnp.where(grp == 1, tiles(s4_ref[rows(m0, m1), :]),
                                   jnp.where(grp == 2, tiles(s8_ref[rows(m0, m1), :]),
                                             tiles(s16))))

    near_end = (ti == 0) | (t0 + tt > seq - POOL_HALO)

    @pl.when(jnp.logical_not(near_end))
    def _():
        inv_win = jnp.where(grp == 0, 0.5, jnp.where(grp == 1, 0.25,
                                                     jnp.where(grp == 2, 0.125, 0.0625)))
        p_ref[...] = ((window_sums() * inv_win).reshape(tt * SUBLANES, LANES)
                      - ext_ref[rows(m0, m1), :])

    @pl.when(near_end)
    def _():
        r = lax.broadcasted_iota(jnp.int32, (tt * SUBLANES, 1), 0)
        tok = t0 + (r >> 3)
        left = lax.shift_left(jnp.int32(1), (r & (SUBLANES - 1)) // CHUNKS_PER_GROUP)
        cnt = jnp.minimum(tok + left, seq) - jnp.maximum(tok - left, 0)
        cnt = jnp.maximum(cnt, 1).astype(F32)
        p_ref[...] = (window_sums().reshape(tt * SUBLANES, LANES) / cnt
                      - ext_ref[rows(m0, m1), :])

    new_chunks = []
    for g in range(len(POOL_WINDOWS)):
        group_chunks = range(g * CHUNKS_PER_GROUP, (g + 1) * CHUNKS_PER_GROUP)
        p = jnp.concatenate([p_ref[pl.ds(c, tt, stride=SUBLANES), :] for c in group_chunks],
                            axis=1).astype(BF16)
        y = jnp.dot(p, w_ref[g], preferred_element_type=F32)
        for q, c in enumerate(group_chunks):
            res = hm[pl.ds(c, tt, stride=SUBLANES), :]
            new_chunks.append(res + y[:, q * LANES:(q + 1) * LANES] * _lane_chunk(sc_ref, c))
            out[pl.ds(c, tt, stride=SUBLANES), :] = new_chunks[-1]
    aff_ref[0] = _route_affinities(new_chunks, t0, rw_ref, seq=seq)


def _pool_layer(h, g, w, scale, g_ffn, router_w, *, seq, tt):
    b = h.shape[0]
    nt = _cdiv(seq, tt)
    halo_rows = POOL_HALO * SUBLANES
    blocks_per_tile = tt // POOL_HALO
    last_halo_block = seq // POOL_HALO - 1
    route_in, route_out, route_shape = _route_specs(b, seq, tt)
    return pl.pallas_call(
        functools.partial(_pool_kernel, seq=seq, tt=tt),
        grid=(b, nt),
        in_specs=[
            pl.BlockSpec((1, halo_rows, LANES),
                         lambda bi, i: (bi, jnp.maximum(i * blocks_per_tile - 1, 0), 0)),
            pl.BlockSpec((1, tt * SUBLANES, LANES), lambda bi, i: (bi, i, 0)),
            pl.BlockSpec((1, halo_rows, LANES),
                         lambda bi, i: (bi, jnp.minimum((i + 1) * blocks_per_tile,
                                                        last_halo_block), 0)),
            pl.BlockSpec((1, D_MODEL), lambda bi, i: (0, 0)),
            pl.BlockSpec((len(POOL_WINDOWS), POOL_GROUP, POOL_GROUP),
                         lambda bi, i: (0, 0, 0)),
            pl.BlockSpec((1, D_MODEL), lambda bi, i: (0, 0)),
        ] + route_in,
        out_specs=[pl.BlockSpec((1, tt * SUBLANES, LANES), lambda bi, i: (bi, i, 0)),
                   route_out],
        out_shape=[jax.ShapeDtypeStruct(h.shape, F32), route_shape],
        scratch_shapes=[pltpu.VMEM(((tt + 2 * POOL_HALO) * SUBLANES, LANES), F32)] * 4
        + [pltpu.VMEM((tt * SUBLANES, LANES), F32)],
        compiler_params=_params(40, "parallel", "arbitrary"),
        name="pool_mixer",
    )(h, h, h, g.reshape(1, D_MODEL), w.astype(BF16), scale.reshape(1, D_MODEL),
      _router_operand(router_w, g_ffn))


def _gelu_tanh(x):
    c = 0.7978845608028654
    return 0.5 * x * (1.0 + jnp.tanh(c * (x + 0.044715 * (x * x * x))))


def _lru_in_kernel(h_ref, g_ref, w_ref, xb_ref, gy_ref, *, tf):
    u = jnp.concatenate(_rms_chunks(_chunks(h_ref, 0, tf), g_ref), axis=1).astype(BF16)
    yb = jnp.dot(u, w_ref[:, D_MODEL:], preferred_element_type=F32)
    gy_ref[...] = _gelu_tanh(yb)
    xb = jnp.dot(u, w_ref[:, :D_MODEL], preferred_element_type=F32)
    _store_chunks(xb_ref, 0, tf,
                  [xb[:, j * LANES:(j + 1) * LANES] for j in range(N_CHUNKS)])


def _lru_in(h_flat, g, w_in, *, tf):
    n_tok = h_flat.shape[0] // SUBLANES
    return pl.pallas_call(
        functools.partial(_lru_in_kernel, tf=tf),
        grid=(_cdiv(n_tok, tf),),
        in_specs=[
            pl.BlockSpec((tf * SUBLANES, LANES), lambda i: (i, 0)),
            pl.BlockSpec((1, D_MODEL), lambda i: (0, 0)),
            pl.BlockSpec((D_MODEL, 2 * D_MODEL), lambda i: (0, 0)),
        ],
        out_specs=[
            pl.BlockSpec((tf * SUBLANES, LANES), lambda i: (i, 0)),
            pl.BlockSpec((tf, D_MODEL), lambda i: (i, 0)),
        ],
        out_shape=[
            jax.ShapeDtypeStruct(h_flat.shape, F32),
            jax.ShapeDtypeStruct((n_tok, D_MODEL), F32),
        ],
        compiler_params=_params(56, "parallel"),
        name="lru_in_proj",
    )(h_flat, g.reshape(1, D_MODEL), w_in.astype(BF16))


def _softplus(x):
    return jnp.maximum(x, 0.0) + jnp.log1p(jnp.exp(-jnp.abs(x)))


def _lru_scan_kernel(*refs, seq, tt, reverse):
    if reverse:
        (xp_ref, xm_ref, xn_ref, cw_ref, cb_ref, wg_ref, bg_ref, lam_ref,
         o_ref, ext_ref, xc_ref, a_ref, b_ref, carry_ref) = refs
        hs = o_ref.at[0]
    else:
        (xp_ref, xm_ref, xn_ref, cw_ref, cb_ref, wg_ref, bg_ref, lam_ref,
         hb_ref, gy_ref, res_ref, wo_ref, rw_ref,
         o_ref, aff_ref, ext_ref, xc_ref, a_ref, b_ref, carry_ref, hs) = refs
    step = pl.program_id(1)
    nt = pl.num_programs(1)
    ti = nt - 1 - step if reverse else step
    t0 = ti * tt

    @pl.when(step == 0)
    def _():
        carry_ref[...] = jnp.zeros_like(carry_ref)

    def masked_rows(src, n_tok, tok0):
        rows = n_tok * SUBLANES
        tok = tok0 + (lax.broadcasted_iota(jnp.int32, (rows, 1), 0) >> 3)
        return jnp.where((tok >= 0) & (tok < seq), src[...], 0.0)

    main_rows = pl.ds(CONV_LEFT * SUBLANES, tt * SUBLANES)
    ext_ref[pl.ds(0, CONV_LEFT * SUBLANES), :] = masked_rows(
        xp_ref.at[0], CONV_LEFT, t0 - CONV_LEFT)
    ext_ref[main_rows, :] = xm_ref[0]
    ext_ref[pl.ds((CONV_LEFT + tt) * SUBLANES, CONV_RIGHT * SUBLANES), :] = masked_rows(
        xn_ref.at[0], CONV_RIGHT, t0 + tt)
    ragged = t0 + tt > seq

    @pl.when(ragged)
    def _():
        ext_ref[main_rows, :] = masked_rows(xm_ref.at[0], tt, t0)

    xc = cb_ref[...]
    for k in range(CONV_WIDTH):
        xc = xc + (ext_ref[pl.ds(k * SUBLANES, tt * SUBLANES), :]
                   .reshape(tt, SUBLANES, LANES) * cw_ref[k])
    xc_ref[...] = xc.reshape(tt * SUBLANES, LANES)

    for hh in range(LRU_HEADS):
        head_chunks = range(hh * CHUNKS_PER_HEAD, (hh + 1) * CHUNKS_PER_HEAD)
        xh = jnp.concatenate([xc_ref[pl.ds(c, tt, stride=SUBLANES), :] for c in head_chunks],
                             axis=1)
        th = jnp.tanh(jnp.dot(xh.astype(BF16), wg_ref[hh], preferred_element_type=F32)
                      + bg_ref[hh])
        lam = lam_ref[:, hh * LRU_HEAD_DIM:(hh + 1) * LRU_HEAD_DIM]
        half_c = (-0.5 * LRU_C) * _softplus(-lam)
        log_a = th[:, :LRU_HEAD_DIM] * half_c + half_c
        a = jnp.exp(log_a)
        bb = (jnp.sqrt(jnp.tanh(-log_a) * (a * a + 1.0))
              * ((th[:, LRU_HEAD_DIM:] + 1.0) * xh))
        for q, c in enumerate(head_chunks):
            a_ref[pl.ds(c, tt, stride=SUBLANES), :] = a[:, q * LANES:(q + 1) * LANES]
            b_ref[pl.ds(c, tt, stride=SUBLANES), :] = bb[:, q * LANES:(q + 1) * LANES]

    @pl.when(ragged)
    def _():
        tok = t0 + (lax.broadcasted_iota(jnp.int32, (tt * SUBLANES, 1), 0) >> 3)
        b_ref[...] = jnp.where(tok < seq, b_ref[...], 0.0)

    def scan_body(i, h):
        for k in range(SUBLANES):
            t = (tt - 1 - (i * SUBLANES + k)) if reverse else (i * SUBLANES + k)
            r0 = pl.multiple_of(t * SUBLANES, SUBLANES)
            h = a_ref[pl.ds(r0, SUBLANES), :] * h + b_ref[pl.ds(r0, SUBLANES), :]
            hs[pl.ds(r0, SUBLANES), :] = h
        return h

    carry_ref[...] = lax.fori_loop(0, tt // SUBLANES, scan_body, carry_ref[...])

    if not reverse:
        hs[...] = hs[...] + hb_ref[0]
        hsum = jnp.concatenate(_chunks(hs, 0, tt), axis=1)
        z = (hsum * gy_ref[0]).astype(BF16)
        y = jnp.dot(z, wo_ref[...], preferred_element_type=F32)
        res = res_ref.at[0]
        new_chunks = [res[pl.ds(j, tt, stride=SUBLANES), :] + y[:, j * LANES:(j + 1) * LANES]
                      for j in range(N_CHUNKS)]
        _store_chunks(o_ref.at[0], 0, tt, new_chunks)
        aff_ref[0] = _route_affinities(new_chunks, t0, rw_ref, seq=seq)


def _lru_scan(xb, conv_w, conv_b, wg, bg, lam, *, seq, tt, reverse,
              hb=None, gy=None, res=None, w_out=None, g_ffn=None, router_w=None):
    b = xb.shape[0]
    nt = _cdiv(seq, tt)
    right_rows = CONV_RIGHT * SUBLANES
    last_right_block = seq // CONV_RIGHT - 1

    def tix(i):
        return nt - 1 - i if reverse else i

    tile_spec = pl.BlockSpec((1, tt * SUBLANES, LANES), lambda bi, i: (bi, tix(i), 0))

    def const_spec(shape):
        return pl.BlockSpec(shape, lambda bi, i: (0,) * len(shape))

    in_specs = [
        pl.BlockSpec((1, CONV_LEFT * SUBLANES, LANES),
                     lambda bi, i: (bi, jnp.maximum(tix(i) * (tt // CONV_LEFT) - 1, 0), 0)),
        tile_spec,
        pl.BlockSpec((1, right_rows, LANES),
                     lambda bi, i: (bi, jnp.minimum((tix(i) + 1) * (tt // CONV_RIGHT),
                                                    last_right_block), 0)),
        const_spec((CONV_WIDTH, SUBLANES, LANES)),
        const_spec((SUBLANES, LANES)),
        const_spec((LRU_HEADS, LRU_HEAD_DIM, 2 * LRU_HEAD_DIM)),
        const_spec((LRU_HEADS, 1, 2 * LRU_HEAD_DIM)),
        const_spec((1, D_MODEL)),
    ]
    args = [xb, xb, xb, (0.5 * conv_w).reshape(CONV_WIDTH, SUBLANES, LANES),
            (0.5 * conv_b).reshape(SUBLANES, LANES), wg, bg, lam.reshape(1, D_MODEL)]
    scratch = [
        pltpu.VMEM(((tt + CONV_WIDTH - 1) * SUBLANES, LANES), F32),
        pltpu.VMEM((tt * SUBLANES, LANES), F32),
        pltpu.VMEM((tt * SUBLANES, LANES), F32),
        pltpu.VMEM((tt * SUBLANES, LANES), F32),
        pltpu.VMEM((SUBLANES, LANES), F32),
    ]
    out_specs = tile_spec
    out_shape = jax.ShapeDtypeStruct(xb.shape, F32)
    if not reverse:
        route_in, route_out, route_shape = _route_specs(b, seq, tt)
        in_specs += [
            tile_spec,
            pl.BlockSpec((1, tt, D_MODEL), lambda bi, i: (bi, i, 0)),
            tile_spec,
            const_spec((D_MODEL, D_MODEL)),
        ] + route_in
        args += [hb, gy, res, w_out, _router_operand(router_w, g_ffn)]
        scratch.append(pltpu.VMEM((tt * SUBLANES, LANES), F32))
        out_specs = [tile_spec, route_out]
        out_shape = [out_shape, route_shape]
    return pl.pallas_call(
        functools.partial(_lru_scan_kernel, seq=seq, tt=tt, reverse=reverse),
        grid=(b, nt),
        in_specs=in_specs,
        out_specs=out_specs,
        out_shape=out_shape,
        scratch_shapes=scratch,
        compiler_params=_params(48, "parallel", "arbitrary"),
        name="lru_scan_rev" if reverse else "lru_scan_fwd_out",
    )(*args)


def _lru_layer(h, g, w_in, conv_w, conv_b, w_gates, b_gates, lam, w_out, g_ffn, router_w,
               *, seq, tt):
    b = h.shape[0]
    xb, gy = _lru_in(h.reshape(b * seq * SUBLANES, LANES), g, w_in, tf=LRU_IN_TILE)
    xb = xb.reshape(h.shape)
    gy = gy.reshape(b, seq, D_MODEL)
    wg = jnp.concatenate([w_gates[:, 0], w_gates[:, 1]], axis=-1).astype(BF16)
    bg = 0.5 * jnp.concatenate([b_gates[:, 0], b_gates[:, 1]], axis=-1)[:, :, None, :]
    scan = functools.partial(_lru_scan, xb, conv_w, conv_b, seq=seq, tt=tt)
    hb = scan(wg[1], bg[1], lam[1], reverse=True)
    return scan(wg[0], bg[0], lam[0], reverse=False,
                hb=hb, gy=gy, res=h, w_out=w_out.astype(BF16),
                g_ffn=g_ffn, router_w=router_w)


def _split_bf16(x):
    hi = x.astype(BF16)
    return hi, (x - hi.astype(F32)).astype(BF16)


def _route_affinities(chunks, tok0, rw_ref, *, seq):
    n = chunks[0].shape[0]
    ss = chunks[0] * chunks[0]
    for c in chunks[1:]:
        ss = ss + c * c
    inv = lax.rsqrt(jnp.sum(ss, axis=-1, keepdims=True) * (1.0 / D_MODEL) + RMS_EPS)
    x_hi, x_lo = _split_bf16(jnp.concatenate(chunks, axis=1))
    p = jnp.dot(x_hi, rw_ref[...], preferred_element_type=F32)
    q = jnp.dot(x_lo, rw_ref[:, :LANES], preferred_element_type=F32)
    logits = (((p[:, :LANES] + p[:, LANES:]) + q) * inv).T[:N_EXPERTS, :]
    tok = tok0 + lax.broadcasted_iota(jnp.int32, (1, n), 1)
    logits = jnp.where(tok < seq, logits, 0.0)
    m = jnp.max(logits, axis=0, keepdims=True)
    ex = jnp.exp(logits - m)
    return ex / jnp.sum(ex, axis=0, keepdims=True)


def _router_operand(router_w, g_ffn):
    w = jnp.pad(g_ffn[:, None] * router_w, ((0, 0), (0, LANES - N_EXPERTS)))
    w_hi, w_lo = _split_bf16(w)
    return jnp.concatenate([w_hi, w_lo], axis=1)


def _route_specs(b, seq, tt):
    seq_pad = _round_up(seq, LANES)
    return ([pl.BlockSpec((D_MODEL, 2 * LANES), lambda bi, i: (0, 0))],
            pl.BlockSpec((1, N_EXPERTS, tt), lambda bi, i: (bi, 0, i)),
            jax.ShapeDtypeStruct((b, N_EXPERTS, seq_pad), F32))


def _cumsum_blocks(x01):
    r = lax.broadcasted_iota(jnp.int32, (LANES, LANES), 0)
    c = lax.broadcasted_iota(jnp.int32, (LANES, LANES), 1)
    tri = (r <= c).astype(BF16)
    local = [jnp.dot(x01[:, j * LANES:(j + 1) * LANES].astype(BF16), tri,
                     preferred_element_type=F32) for j in range(x01.shape[1] // LANES)]
    off = jnp.zeros((x01.shape[0], 1), F32)
    outs = []
    for cs in local:
        outs.append(cs + off)
        off = off + cs[:, LANES - 1:LANES]
    return outs


def _topk_kernel(aff_ref, idx_ref, cs_ref, *, seq, cap, cap_pad, slots_pad):
    aff = aff_ref[0]
    lane = lax.broadcasted_iota(jnp.int32, aff.shape, 1)
    aff = jnp.where(lane < seq, aff, -1.0)
    capf = float(cap)
    n_blk = aff.shape[1] // LANES

    def enough(cand):
        candf = lax.bitcast_convert_type(cand, F32)
        return jnp.sum((aff >= candf).astype(F32), axis=1, keepdims=True) >= capf

    def bit_pair_step(i, cur):
        hi = cur | lax.shift_left(jnp.int32(1), 30 - 2 * i)
        lo = cur | lax.shift_left(jnp.int32(1), 29 - 2 * i)
        both = hi | lo
        return jnp.where(enough(hi), jnp.where(enough(both), both, hi),
                         jnp.where(enough(lo), lo, cur))

    cur = lax.fori_loop(0, 15, bit_pair_step, jnp.zeros((N_EXPERTS, 1), jnp.int32))
    cur = jnp.where(enough(cur | 1), cur | 1, cur)
    thr = lax.bitcast_convert_type(cur, F32)
    gt = aff > thr
    eq = aff == thr
    need = capf - jnp.sum(gt.astype(F32), axis=1, keepdims=True)
    eq_cnt = jnp.concatenate(_cumsum_blocks(eq.astype(F32)), axis=1)
    sel = gt | (eq & (eq_cnt <= need))

    blocks = _cumsum_blocks(sel.astype(F32))
    for j, cs in enumerate(blocks):
        cs_ref[pl.ds(j * N_EXPERTS, N_EXPERTS), :] = cs
    cs_ref[pl.ds(n_blk * N_EXPERTS, (LANES - n_blk) * N_EXPERTS), :] = jnp.zeros(
        ((LANES - n_blk) * N_EXPERTS, LANES), F32)
    never = jnp.full((N_EXPERTS, LANES - n_blk), 2.0 * COUNT_RADIX * 256, F32)
    blk_end = jnp.concatenate([cs[:, LANES - 1:LANES] for cs in blocks] + [never], axis=1)

    slot = lax.broadcasted_iota(jnp.int32, (cap_pad, 1), 0).astype(F32)
    lane_s = lax.broadcasted_iota(jnp.int32, (cap_pad, LANES), 1).astype(F32)
    lane_e = lax.broadcasted_iota(jnp.int32, (slots_pad, LANES), 1)
    experts = range(N_EXPERTS)
    digits = []
    for e in experts:
        cmat = cs_ref[pl.ds(e, LANES, stride=N_EXPERTS), :]
        hi = jnp.floor(cmat * (1.0 / COUNT_RADIX))
        digits.append(jnp.concatenate([hi, cmat - COUNT_RADIX * hi], axis=1).astype(BF16))
    ones = jnp.ones((LANES, LANES), BF16)

    def count_le(x):
        return jnp.dot((x <= slot).astype(BF16), ones, preferred_element_type=F32)

    full = [count_le(blk_end[e:e + 1, :]) for e in experts]
    pick = [(lane_s == full[e]).astype(BF16) for e in experts]
    rows = [jnp.dot(pick[e], digits[e], preferred_element_type=F32) for e in experts]
    rows = [COUNT_RADIX * r_[:, :LANES] + r_[:, LANES:] for r_ in rows]
    inside = [count_le(rows[e]) for e in experts]
    acc = jnp.zeros((slots_pad, LANES), F32)
    pad_rows = jnp.zeros((slots_pad - cap_pad, LANES), F32)
    for e in experts:
        col = jnp.concatenate([LANES * full[e] + inside[e], pad_rows], axis=0)
        acc = jnp.where(lane_e == e, col, acc)
    idx = acc.T[:N_EXPERTS, :].astype(jnp.int32)
    slot_l = lax.broadcasted_iota(jnp.int32, idx.shape, 1)
    idx_ref[0] = jnp.where(slot_l < cap, idx, 0)


def _topk(aff_t, *, seq, cap, cap_pad, slots_pad):
    b, _, seq_pad = aff_t.shape
    assert seq_pad // LANES <= LANES and cap < COUNT_RADIX * 256
    return pl.pallas_call(
        functools.partial(_topk_kernel, seq=seq, cap=cap, cap_pad=cap_pad,
                          slots_pad=slots_pad),
        grid=(b,),
        in_specs=[pl.BlockSpec((1, N_EXPERTS, seq_pad), lambda bi: (bi, 0, 0))],
        out_specs=pl.BlockSpec((1, N_EXPERTS, slots_pad), lambda bi: (bi, 0, 0)),
        out_shape=jax.ShapeDtypeStruct((b, N_EXPERTS, slots_pad), jnp.int32),
        scratch_shapes=[pltpu.VMEM((LANES * N_EXPERTS, LANES), F32)],
        compiler_params=_params(40, "parallel"),
        name="moe_topk",
    )(aff_t)


def _per_smem_phase(step, fn):
    phases = SMEM_ROWS // EXPERTS_PER_STEP
    for phase in range(phases):
        pl.when(step % phases == phase)(functools.partial(fn, phase * EXPERTS_PER_STEP))


def _smem_spec(width):
    steps_per_block = SMEM_ROWS // EXPERTS_PER_STEP
    blocks_per_seq = N_EXPERTS // SMEM_ROWS
    return pl.BlockSpec((1, 1, SMEM_ROWS * width),
                        lambda bi, s: (bi * blocks_per_seq + s // steps_per_block, 0, 0),
                        memory_space=pltpu.SMEM)


def _gather_kernel(idx_ref, h_ref, g_ref, xs_ref, zx_ref, *, cap_pad, slots_pad):
    h = h_ref.at[0]
    group = max(d for d in range(SUBLANES, GATHER_GROUP_MAX + 1, SUBLANES) if cap_pad % d == 0)

    def run(row0):
        for g in range(EXPERTS_PER_STEP):
            row = row0 + g

            def body(i, carry):
                toks = [idx_ref[0, 0, row * slots_pad + i * group + k] for k in range(group)]
                tiles = [h[pl.ds(pl.multiple_of(t * SUBLANES, SUBLANES), SUBLANES), :]
                         for t in toks]
                for k, tile in enumerate(tiles):
                    s0 = pl.multiple_of((i * group + k) * SUBLANES, SUBLANES)
                    zx_ref[pl.ds(s0, SUBLANES), :] = tile
                return carry

            lax.fori_loop(0, cap_pad // group, body, 0)
            u = _rms_chunks(_chunks(zx_ref, 0, cap_pad), g_ref)
            xs_ref[g, 0] = jnp.concatenate(u, axis=1).astype(BF16)

    _per_smem_phase(pl.program_id(1), run)


def _gather(idx_smem, h, g, *, cap_pad):
    b = h.shape[0]
    slots_pad = idx_smem.shape[-1] // SMEM_ROWS
    return pl.pallas_call(
        functools.partial(_gather_kernel, cap_pad=cap_pad, slots_pad=slots_pad),
        grid=(b, N_EXPERTS // EXPERTS_PER_STEP),
        in_specs=[
            _smem_spec(slots_pad),
            pl.BlockSpec((1,) + h.shape[1:], lambda bi, e: (bi, 0, 0)),
            pl.BlockSpec((1, D_MODEL), lambda bi, e: (0, 0)),
        ],
        out_specs=pl.BlockSpec((EXPERTS_PER_STEP, 1, cap_pad, D_MODEL),
                               lambda bi, e: (e, bi, 0, 0)),
        out_shape=jax.ShapeDtypeStruct((N_EXPERTS, b, cap_pad, D_MODEL), BF16),
        scratch_shapes=[pltpu.VMEM((cap_pad * SUBLANES, LANES), F32)],
        compiler_params=_params(56, "parallel", "arbitrary"),
        name="moe_gather",
    )(idx_smem, h, g.reshape(1, D_MODEL))


def _ffn_kernel(x_ref, wg_ref, wu_ref, wd_ref, o_ref, hid_ref, wdb_ref, *, ft, n_f):
    r = pl.program_id(1)
    f = pl.program_id(2)

    @pl.when(r == 0)
    def _():
        wdb_ref[pl.ds(pl.multiple_of(f * ft, ft), ft), :] = wd_ref[0, 0].astype(BF16)

    x = x_ref[0]
    hg = jnp.dot(x, wg_ref[0, 0].astype(BF16), preferred_element_type=F32)
    hu = jnp.dot(x, wu_ref[0, 0].astype(BF16), preferred_element_type=F32)
    hid_ref[f] = (hg * jax.nn.sigmoid(hg) * hu).astype(BF16)

    @pl.when(f == n_f - 1)
    def _():
        hid = jnp.concatenate([hid_ref[k] for k in range(n_f)], axis=1)
        y = jnp.dot(hid, wdb_ref[...], preferred_element_type=F32)
        _store_chunks(o_ref.at[0], 0, y.shape[0],
                      [y[:, j * LANES:(j + 1) * LANES] for j in range(N_CHUNKS)])


def _ffn(xs, w_gate, w_up, w_down, layer):
    n_exp, rows, _ = xs.shape
    d_expert = w_gate.shape[-1]
    rt = rows // FFN_ROW_TILES
    ft = min(FFN_F_TILE, d_expert)
    n_f = d_expert // ft

    def wd_tile(e, r, f):
        return (layer, e, jnp.where(r == 0, f, n_f - 1), 0)

    return pl.pallas_call(
        functools.partial(_ffn_kernel, ft=ft, n_f=n_f),
        grid=(n_exp, FFN_ROW_TILES, n_f),
        in_specs=[
            pl.BlockSpec((1, rt, D_MODEL), lambda e, r, f: (e, r, 0)),
            pl.BlockSpec((1, 1, D_MODEL, ft), lambda e, r, f: (layer, e, 0, f)),
            pl.BlockSpec((1, 1, D_MODEL, ft), lambda e, r, f: (layer, e, 0, f)),
            pl.BlockSpec((1, 1, ft, D_MODEL), wd_tile),
        ],
        out_specs=pl.BlockSpec((1, rt * SUBLANES, LANES), lambda e, r, f: (e, r, 0)),
        out_shape=jax.ShapeDtypeStruct((n_exp, rows * SUBLANES, LANES), F32),
        scratch_shapes=[pltpu.VMEM((n_f, rt, ft), BF16),
                        pltpu.VMEM((d_expert, D_MODEL), BF16)],
        compiler_params=_params(60, "parallel", "arbitrary", "arbitrary"),
        name="moe_ffn",
    )(xs, w_gate, w_up, w_down)


def _scatter_kernel(idx_ref, aff_ref, ys_ref, h_hbm, o_hbm, acc0, acc1, load_sem, store_sem,
                    *, cap, slots_pad, seq_pad):
    b = pl.program_id(0)
    e = pl.program_id(1)
    n_b = pl.num_programs(0)
    n_e = pl.num_programs(1)
    accs = (acc0, acc1)

    def load(seq_i, slot):
        return pltpu.make_async_copy(h_hbm.at[seq_i], accs[slot], load_sem.at[slot])

    def store(seq_i, slot):
        return pltpu.make_async_copy(accs[slot], o_hbm.at[seq_i], store_sem.at[slot])

    def run(slot):
        acc = accs[slot]
        other = 1 - slot

        @pl.when((b == 0) & (e == 0))
        def _():
            load(b, slot).start()

        @pl.when(e == 0)
        def _():
            load(b, slot).wait()

        @pl.when((e == SCATTER_PREFETCH_STEP) & (b >= 1))
        def _():
            store(b - 1, other).wait()

        @pl.when((e == SCATTER_PREFETCH_STEP) & (b + 1 < n_b))
        def _():
            load(b + 1, other).start()

        def add_experts(row0):
            for g in range(EXPERTS_PER_STEP):
                row = row0 + g
                ys = ys_ref.at[g, 0]

                def add_rows(slots):
                    rows, vals = [], []
                    for s in slots:
                        t = idx_ref[0, 0, row * slots_pad + s]
                        gate = aff_ref[0, 0, row * seq_pad + t]
                        r0 = pl.multiple_of(t * SUBLANES, SUBLANES)
                        s0 = (s * SUBLANES if isinstance(s, int)
                              else pl.multiple_of(s * SUBLANES, SUBLANES))
                        vals.append(acc[pl.ds(r0, SUBLANES), :]
                                    + gate * ys[pl.ds(s0, SUBLANES), :])
                        rows.append(r0)
                    for r0, v in zip(rows, vals):
                        acc[pl.ds(r0, SUBLANES), :] = v

                def body(i, carry):
                    add_rows([i * SCATTER_GROUP + k for k in range(SCATTER_GROUP)])
                    return carry

                n_groups = cap // SCATTER_GROUP
                lax.fori_loop(0, n_groups, body, 0)
                if cap % SCATTER_GROUP:
                    add_rows(list(range(n_groups * SCATTER_GROUP, cap)))

        _per_smem_phase(e, add_experts)

        @pl.when(e == n_e - 1)
        def _():
            store(b, slot).start()

        @pl.when((e == n_e - 1) & (b == n_b - 1))
        def _():
            store(b, slot).wait()

    for slot in range(2):
        pl.when(b % 2 == slot)(functools.partial(run, slot))


def _scatter(idx_smem, aff_smem, ys, h, *, cap, cap_pad):
    b = h.shape[0]
    slots_pad = idx_smem.shape[-1] // SMEM_ROWS
    seq_pad = aff_smem.shape[-1] // SMEM_ROWS
    n_steps = N_EXPERTS // EXPERTS_PER_STEP
    assert 1 <= SCATTER_PREFETCH_STEP <= n_steps - 1
    return pl.pallas_call(
        functools.partial(_scatter_kernel, cap=cap, slots_pad=slots_pad, seq_pad=seq_pad),
        grid=(b, n_steps),
        in_specs=[
            _smem_spec(slots_pad),
            _smem_spec(seq_pad),
            pl.BlockSpec((EXPERTS_PER_STEP, 1, cap_pad * SUBLANES, LANES),
                         lambda bi, e: (e, bi, 0, 0)),
            pl.BlockSpec(memory_space=pl.ANY),
        ],
        out_specs=pl.BlockSpec(memory_space=pl.ANY),
        out_shape=jax.ShapeDtypeStruct(h.shape, F32),
        scratch_shapes=[pltpu.VMEM(h.shape[1:], F32), pltpu.VMEM(h.shape[1:], F32),
                        pltpu.SemaphoreType.DMA((2,)), pltpu.SemaphoreType.DMA((2,))],
        compiler_params=_params(56, "arbitrary", "arbitrary"),
        name="moe_scatter",
    )(idx_smem, aff_smem, ys, h)


def _moe_layer(h, aff_t, g, w_gate, w_up, w_down, layer, *, seq):
    b = h.shape[0]
    cap = CAPACITY_FACTOR * seq // N_EXPERTS
    cap_pad = _round_up(cap, BF16_ROWS)
    assert (b * cap_pad) % (FFN_ROW_TILES * BF16_ROWS) == 0
    slots_pad = _round_up(cap_pad, LANES)
    seq_pad = aff_t.shape[-1]
    idx = _topk(aff_t, seq=seq, cap=cap, cap_pad=cap_pad, slots_pad=slots_pad)
    idx_smem = idx.reshape(b * N_EXPERTS // SMEM_ROWS, 1, SMEM_ROWS * slots_pad)
    aff_smem = aff_t.reshape(b * N_EXPERTS // SMEM_ROWS, 1, SMEM_ROWS * seq_pad)
    xs = _gather(idx_smem, h, g, cap_pad=cap_pad)
    ys = _ffn(xs.reshape(N_EXPERTS, b * cap_pad, D_MODEL), w_gate, w_up, w_down, layer)
    ys = ys.reshape(N_EXPERTS, b, cap_pad * SUBLANES, LANES)
    return _scatter(idx_smem, aff_smem, ys, h, cap=cap, cap_pad=cap_pad)


def _final_kernel(ha_ref, hb_ref, g_ref, o_ref, *, tt):
    ha = ha_ref.at[0]
    hb = hb_ref.at[0]
    ch = [jnp.concatenate(
        [ha[pl.ds(N_META * SUBLANES + j, tt - N_META, stride=SUBLANES), :],
         hb[pl.ds(j, N_META, stride=SUBLANES), :]], axis=0) for j in range(N_CHUNKS)]
    o_ref[0] = jnp.concatenate(_rms_chunks(ch, g_ref), axis=1)


def _final(h, g, *, seq, tt):
    b = h.shape[0]
    s_out = seq - N_META
    assert s_out % tt == 0
    assert tt % N_META == 0
    return pl.pallas_call(
        functools.partial(_final_kernel, tt=tt),
        grid=(b, s_out // tt),
        in_specs=[
            pl.BlockSpec((1, tt * SUBLANES, LANES), lambda bi, i: (bi, i, 0)),
            pl.BlockSpec((1, N_META * SUBLANES, LANES),
                         lambda bi, i: (bi, (i + 1) * (tt // N_META), 0)),
            pl.BlockSpec((1, D_MODEL), lambda bi, i: (0, 0)),
        ],
        out_specs=pl.BlockSpec((1, tt, D_MODEL), lambda bi, i: (bi, i, 0)),
        out_shape=jax.ShapeDtypeStruct((b, s_out, D_MODEL), F32),
        compiler_params=_params(40, "parallel", "parallel"),
        name="final_norm",
    )(h, h, g.reshape(1, D_MODEL))


def kernel(x, meta_tokens, norm_mix, norm_ffn, norm_final, pool_w, pool_scale, lru_w_in, lru_conv_w, lru_conv_b, lru_w_gates, lru_b_gates, lru_lambda, lru_w_out, router_w, moe_w_gate, moe_w_up, moe_w_down):
    b, s, d = x.shape
    assert d == D_MODEL
    seq = s + N_META
    tt = TIME_TILE
    depth = norm_mix.shape[0]
    h = _embed(x, meta_tokens.astype(x.dtype), tt=tt)
    for i in range(depth):
        j = i // 2
        if i % 2 == 0:
            h, aff_t = _pool_layer(h, norm_mix[i], pool_w[j], pool_scale[j],
                                   norm_ffn[i], router_w[i], seq=seq, tt=tt)
        else:
            h, aff_t = _lru_layer(h, norm_mix[i], lru_w_in[j], lru_conv_w[j], lru_conv_b[j],
                                  lru_w_gates[j], lru_b_gates[j], lru_lambda[j], lru_w_out[j],
                                  norm_ffn[i], router_w[i], seq=seq, tt=tt)
        h = _moe_layer(h, aff_t, norm_ffn[i], moe_w_gate, moe_w_up, moe_w_down, i, seq=seq)
    return _final(h, norm_final, seq=seq, tt=tt)
```

```python
import functools

import jax
import jax.numpy as jnp
from jax import lax
from jax.experimental import pallas as pl
from jax.experimental.pallas import tpu as pltpu

F32 = jnp.float32
BF16 = jnp.bfloat16

LANES = 128
SUBLANES = 8
BF16_ROWS = 16
D_MODEL = 1024
N_CHUNKS = D_MODEL // LANES
assert N_CHUNKS == SUBLANES
N_META = 16
POOL_WINDOWS = (2, 4, 8, 16)
POOL_GROUP = D_MODEL // len(POOL_WINDOWS)
CHUNKS_PER_GROUP = POOL_GROUP // LANES
POOL_HALO = 8
LRU_HEADS = 4
LRU_HEAD_DIM = D_MODEL // LRU_HEADS
CHUNKS_PER_HEAD = LRU_HEAD_DIM // LANES
LRU_C = 8.0
CONV_WIDTH = 4
CONV_LEFT = 1
CONV_RIGHT = CONV_WIDTH - 1 - CONV_LEFT
N_EXPERTS = 16
CAPACITY_FACTOR = 2
RMS_EPS = 1e-6
TIME_TILE = 512
FFN_ROW_TILES = 4
FFN_F_TILE = 1024
LRU_IN_TILE = 1024
GATHER_GROUP_MAX = 48
EXPERTS_PER_STEP = 4
SMEM_ROWS = 8
SCATTER_PREFETCH_STEP = 1
SCATTER_GROUP = 8
COUNT_RADIX = 32.0
MIB = 1024 * 1024


def _cdiv(a, b):
    return -(-a // b)


def _round_up(a, b):
    return _cdiv(a, b) * b


def _chunks(ref2d, tok0, n):
    return [ref2d[pl.ds(tok0 * SUBLANES + j, n, stride=SUBLANES), :]
            for j in range(N_CHUNKS)]


def _store_chunks(ref2d, tok0, n, chunks):
    for j in range(N_CHUNKS):
        ref2d[pl.ds(tok0 * SUBLANES + j, n, stride=SUBLANES), :] = chunks[j]


def _lane_chunk(ref, j):
    return ref[:, j * LANES:(j + 1) * LANES]


def _rms_chunks(chunks, g_ref):
    ss = chunks[0] * chunks[0]
    for c in chunks[1:]:
        ss = ss + c * c
    ms = jnp.sum(ss, axis=-1, keepdims=True) * (1.0 / D_MODEL)
    inv = lax.rsqrt(ms + RMS_EPS)
    return [c * inv * _lane_chunk(g_ref, j) for j, c in enumerate(chunks)]


def _token_ids(tok0, n):
    return tok0 + lax.broadcasted_iota(jnp.int32, (n, 1), 0)


def _params(vmem_mib, *sem):
    return pltpu.CompilerParams(dimension_semantics=sem,
                                vmem_limit_bytes=vmem_mib * MIB)


def _embed_kernel(meta_ref, xa_ref, xb_ref, o_ref, *, tt):
    i = pl.program_id(1)
    out = o_ref.at[0]
    body = xb_ref[0, :tt - N_META, :]
    _store_chunks(out, N_META, tt - N_META,
                  [body[:, j * LANES:(j + 1) * LANES] for j in range(N_CHUNKS)])

    def head(src):
        _store_chunks(out, 0, N_META,
                      [src[:, j * LANES:(j + 1) * LANES] for j in range(N_CHUNKS)])

    @pl.when(i == 0)
    def _():
        head(meta_ref[...])

    @pl.when(i > 0)
    def _():
        head(xa_ref[0])


def _embed(x, meta, *, tt):
    b, s, _ = x.shape
    seq = s + N_META
    assert s % tt == 0 and tt % N_META == 0
    last = s // tt - 1
    metas_per_tile = tt // N_META
    return pl.pallas_call(
        functools.partial(_embed_kernel, tt=tt),
        grid=(b, _cdiv(seq, tt)),
        in_specs=[
            pl.BlockSpec((N_META, D_MODEL), lambda bi, i: (0, 0)),
            pl.BlockSpec((1, N_META, D_MODEL),
                         lambda bi, i: (bi, jnp.maximum(i * metas_per_tile - 1, 0), 0)),
            pl.BlockSpec((1, tt, D_MODEL), lambda bi, i: (bi, jnp.minimum(i, last), 0)),
        ],
        out_specs=pl.BlockSpec((1, tt * SUBLANES, LANES), lambda bi, i: (bi, i, 0)),
        out_shape=jax.ShapeDtypeStruct((b, seq * SUBLANES, LANES), F32),
        compiler_params=_params(40, "parallel", "parallel"),
        name="embed",
    )(meta, x, x)


def _pool_kernel(hp_ref, hm_ref, hn_ref, g_ref, w_ref, sc_ref, rw_ref,
                 o_ref, aff_ref, ext_ref, s2_ref, s4_ref, s8_ref, p_ref, *, seq, tt):
    ti = pl.program_id(1)
    t0 = ti * tt
    hm = hm_ref.at[0]
    out = o_ref.at[0]

    def norm_into_ext(src, n, ext_tok0, tok0):
        tok = _token_ids(tok0, n)
        valid = (tok >= 0) & (tok < seq)
        ch = [jnp.where(valid, c, 0.0) for c in _chunks(src, 0, n)]
        _store_chunks(ext_ref, ext_tok0, n, _rms_chunks(ch, g_ref))

    norm_into_ext(hp_ref.at[0], POOL_HALO, 0, t0 - POOL_HALO)
    norm_into_ext(hm, tt, POOL_HALO, t0)
    norm_into_ext(hn_ref.at[0], POOL_HALO, POOL_HALO + tt, t0 + tt)

    def rows(a, b):
        return pl.ds(a * SUBLANES, (b - a) * SUBLANES)

    lo, hi = 1, tt + 2 * POOL_HALO
    s2_ref[rows(lo, hi), :] = ext_ref[rows(lo - 1, hi - 1), :] + ext_ref[rows(lo, hi), :]
    prev, half = s2_ref, 1
    for nxt in (s4_ref, s8_ref):
        lo, hi = lo + half, hi - half
        nxt[rows(lo, hi), :] = (prev[rows(lo - half, hi - half), :]
                                + prev[rows(lo + half, hi + half), :])
        prev, half = nxt, 2 * half
    m0, m1 = POOL_HALO, POOL_HALO + tt
    assert lo + half <= m0 and m1 <= hi - half and 4 * half == POOL_WINDOWS[-1]

    grp = lax.broadcasted_iota(jnp.int32, (SUBLANES, LANES), 0) // CHUNKS_PER_GROUP

    def window_sums():
        def tiles(v):
            return v.reshape(tt, SUBLANES, LANES)

        s16 = prev[rows(m0 - half, m1 - half), :] + prev[rows(m0 + half, m1 + half), :]
        return jnp.where(grp == 0, tiles(s2_ref[rows(m0, m1), :]),
                         jnp.where(grp == 1, tiles(s4_ref[rows(m0, m1), :]),
                                   jnp.where(grp == 2, tiles(s8_ref[rows(m0, m1), :]),
                                             tiles(s16))))

    near_end = (ti == 0) | (t0 + tt > seq - POOL_HALO)

    @pl.when(jnp.logical_not(near_end))
    def _():
        inv_win = jnp.where(grp == 0, 0.5, jnp.where(grp == 1, 0.25,
                                                     jnp.where(grp == 2, 0.125, 0.0625)))
        p_ref[...] = ((window_sums() * inv_win).reshape(tt * SUBLANES, LANES)
                      - ext_ref[rows(m0, m1), :])

    @pl.when(near_end)
    def _():
        r = lax.broadcasted_iota(jnp.int32, (tt * SUBLANES, 1), 0)
        tok = t0 + (r >> 3)
        left = lax.shift_left(jnp.int32(1), (r & (SUBLANES - 1)) // CHUNKS_PER_GROUP)
        cnt = jnp.minimum(tok + left, seq) - jnp.maximum(tok - left, 0)
        cnt = jnp.maximum(cnt, 1).astype(F32)
        p_ref[...] = (window_sums().reshape(tt * SUBLANES, LANES) / cnt
                      - ext_ref[rows(m0, m1), :])

    new_chunks = []
    for g in range(len(POOL_WINDOWS)):
        group_chunks = range(g * CHUNKS_PER_GROUP, (g + 1) * CHUNKS_PER_GROUP)
        p = jnp.concatenate([p_ref[pl.ds(c, tt, stride=SUBLANES), :] for c in group_chunks],
                            axis=1).astype(BF16)
        y = jnp.dot(p, w_ref[g], preferred_element_type=F32)
        for q, c in enumerate(group_chunks):
            res = hm[pl.ds(c, tt, stride=SUBLANES), :]
            new_chunks.append(res + y[:, q * LANES:(q + 1) * LANES] * _lane_chunk(sc_ref, c))
            out[pl.ds(c, tt, stride=SUBLANES), :] = new_chunks[-1]
    aff_ref[0] = _route_affinities(new_chunks, t0, rw_ref, seq=seq)


def _pool_layer(h, g, w, scale, g_ffn, router_w, *, seq, tt):
    b = h.shape[0]
    nt = _cdiv(seq, tt)
    halo_rows = POOL_HALO * SUBLANES
    blocks_per_tile = tt // POOL_HALO
    last_halo_block = seq // POOL_HALO - 1
    route_in, route_out, route_shape = _route_specs(b, seq, tt)
    return pl.pallas_call(
        functools.partial(_pool_kernel, seq=seq, tt=tt),
        grid=(b, nt),
        in_specs=[
            pl.BlockSpec((1, halo_rows, LANES),
                         lambda bi, i: (bi, jnp.maximum(i * blocks_per_tile - 1, 0), 0)),
            pl.BlockSpec((1, tt * SUBLANES, LANES), lambda bi, i: (bi, i, 0)),
            pl.BlockSpec((1, halo_rows, LANES),
                         lambda bi, i: (bi, jnp.minimum((i + 1) * blocks_per_tile,
                                                        last_halo_block), 0)),
            pl.BlockSpec((1, D_MODEL), lambda bi, i: (0, 0)),
            pl.BlockSpec((len(POOL_WINDOWS), POOL_GROUP, POOL_GROUP),
                         lambda bi, i: (0, 0, 0)),
            pl.BlockSpec((1, D_MODEL), lambda bi, i: (0, 0)),
        ] + route_in,
        out_specs=[pl.BlockSpec((1, tt * SUBLANES, LANES), lambda bi, i: (bi, i, 0)),
                   route_out],
        out_shape=[jax.ShapeDtypeStruct(h.shape, F32), route_shape],
        scratch_shapes=[pltpu.VMEM(((tt + 2 * POOL_HALO) * SUBLANES, LANES), F32)] * 4
        + [pltpu.VMEM((tt * SUBLANES, LANES), F32)],
        compiler_params=_params(40, "parallel", "arbitrary"),
        name="pool_mixer",
    )(h, h, h, g.reshape(1, D_MODEL), w.astype(BF16), scale.reshape(1, D_MODEL),
      _router_operand(router_w, g_ffn))


def _gelu_tanh(x):
    c = 0.7978845608028654
    return 0.5 * x * (1.0 + jnp.tanh(c * (x + 0.044715 * (x * x * x))))


def _lru_in_kernel(h_ref, g_ref, w_ref, xb_ref, gy_ref, *, tf):
    u = jnp.concatenate(_rms_chunks(_chunks(h_ref, 0, tf), g_ref), axis=1).astype(BF16)
    yb = jnp.dot(u, w_ref[:, D_MODEL:], preferred_element_type=F32)
    gy_ref[...] = _gelu_tanh(yb)
    xb = jnp.dot(u, w_ref[:, :D_MODEL], preferred_element_type=F32)
    _store_chunks(xb_ref, 0, tf,
                  [xb[:, j * LANES:(j + 1) * LANES] for j in range(N_CHUNKS)])


def _lru_in(h_flat, g, w_in, *, tf):
    n_tok = h_flat.shape[0] // SUBLANES
    return pl.pallas_call(
        functools.partial(_lru_in_kernel, tf=tf),
        grid=(_cdiv(n_tok, tf),),
        in_specs=[
            pl.BlockSpec((tf * SUBLANES, LANES), lambda i: (i, 0)),
            pl.BlockSpec((1, D_MODEL), lambda i: (0, 0)),
            pl.BlockSpec((D_MODEL, 2 * D_MODEL), lambda i: (0, 0)),
        ],
        out_specs=[
            pl.BlockSpec((tf * SUBLANES, LANES), lambda i: (i, 0)),
            pl.BlockSpec((tf, D_MODEL), lambda i: (i, 0)),
        ],
        out_shape=[
            jax.ShapeDtypeStruct(h_flat.shape, F32),
            jax.ShapeDtypeStruct((n_tok, D_MODEL), F32),
        ],
        compiler_params=_params(56, "parallel"),
        name="lru_in_proj",
    )(h_flat, g.reshape(1, D_MODEL), w_in.astype(BF16))


def _softplus(x):
    return jnp.maximum(x, 0.0) + jnp.log1p(jnp.exp(-jnp.abs(x)))


def _lru_scan_kernel(*refs, seq, tt, reverse):
    if reverse:
        (xp_ref, xm_ref, xn_ref, cw_ref, cb_ref, wg_ref, bg_ref, lam_ref,
         o_ref, ext_ref, xc_ref, a_ref, b_ref, carry_ref) = refs
        hs = o_ref.at[0]
    else:
        (xp_ref, xm_ref, xn_ref, cw_ref, cb_ref, wg_ref, bg_ref, lam_ref,
         hb_ref, gy_ref, res_ref, wo_ref, rw_ref,
         o_ref, aff_ref, ext_ref, xc_ref, a_ref, b_ref, carry_ref, hs) = refs
    step = pl.program_id(1)
    nt = pl.num_programs(1)
    ti = nt - 1 - step if reverse else step
    t0 = ti * tt

    @pl.when(step == 0)
    def _():
        carry_ref[...] = jnp.zeros_like(carry_ref)

    def masked_rows(src, n_tok, tok0):
        rows = n_tok * SUBLANES
        tok = tok0 + (lax.broadcasted_iota(jnp.int32, (rows, 1), 0) >> 3)
        return jnp.where((tok >= 0) & (tok < seq), src[...], 0.0)

    def tile_body(n):
        ext_ref[pl.ds(0, CONV_LEFT * SUBLANES), :] = masked_rows(
            xp_ref.at[0], CONV_LEFT, t0 - CONV_LEFT)
        ext_ref[pl.ds(CONV_LEFT * SUBLANES, n * SUBLANES), :] = xm_ref[0, pl.ds(0, n * SUBLANES), :]
        right_rows = pl.ds((CONV_LEFT + n) * SUBLANES, CONV_RIGHT * SUBLANES)
        if n == tt:
            ext_ref[right_rows, :] = masked_rows(xn_ref.at[0], CONV_RIGHT, t0 + tt)
        else:
            ext_ref[right_rows, :] = jnp.zeros((CONV_RIGHT * SUBLANES, LANES), F32)

        xc = cb_ref[...]
        for k in range(CONV_WIDTH):
            xc = xc + (ext_ref[pl.ds(k * SUBLANES, n * SUBLANES), :]
                       .reshape(n, SUBLANES, LANES) * cw_ref[k])
        xc_ref[pl.ds(0, n * SUBLANES), :] = xc.reshape(n * SUBLANES, LANES)

        for hh in range(LRU_HEADS):
            head_chunks = range(hh * CHUNKS_PER_HEAD, (hh + 1) * CHUNKS_PER_HEAD)
            xh = jnp.concatenate(
                [xc_ref[pl.ds(c, n, stride=SUBLANES), :] for c in head_chunks],
                axis=1)
            th = jnp.tanh(jnp.dot(xh.astype(BF16), wg_ref[hh], preferred_element_type=F32)
                          + bg_ref[hh])
            lam = lam_ref[:, hh * LRU_HEAD_DIM:(hh + 1) * LRU_HEAD_DIM]
            half_c = (-0.5 * LRU_C) * _softplus(-lam)
            log_a = th[:, :LRU_HEAD_DIM] * half_c + half_c
            a = jnp.exp(log_a)
            bb = (jnp.sqrt(jnp.tanh(-log_a) * (a * a + 1.0))
                  * ((th[:, LRU_HEAD_DIM:] + 1.0) * xh))
            for q, c in enumerate(head_chunks):
                a_ref[pl.ds(c, n, stride=SUBLANES), :] = a[:, q * LANES:(q + 1) * LANES]
                b_ref[pl.ds(c, n, stride=SUBLANES), :] = bb[:, q * LANES:(q + 1) * LANES]

        def scan_body(i, h):
            for k in range(SUBLANES):
                t = (n - 1 - (i * SUBLANES + k)) if reverse else (i * SUBLANES + k)
                r0 = pl.multiple_of(t * SUBLANES, SUBLANES)
                h = a_ref[pl.ds(r0, SUBLANES), :] * h + b_ref[pl.ds(r0, SUBLANES), :]
                hs[pl.ds(r0, SUBLANES), :] = h
            return h

        carry_ref[...] = lax.fori_loop(0, n // SUBLANES, scan_body, carry_ref[...])

        if not reverse:
            rows = pl.ds(0, n * SUBLANES)
            hs[rows, :] = hs[rows, :] + hb_ref[0, rows, :]
            hsum = jnp.concatenate(_chunks(hs, 0, n), axis=1)
            z = (hsum * gy_ref[0, pl.ds(0, n), :]).astype(BF16)
            y = jnp.dot(z, wo_ref[...], preferred_element_type=F32)
            new_chunks = [c + y[:, j * LANES:(j + 1) * LANES]
                          for j, c in enumerate(_chunks(res_ref.at[0], 0, n))]
            _store_chunks(o_ref.at[0], 0, n, new_chunks)
            if n == tt:
                aff_ref[0] = _route_affinities(new_chunks, t0, rw_ref, seq=seq)
            else:
                n_r = _round_up(n, LANES)
                padded = [jnp.concatenate([c, jnp.zeros((n_r - n, LANES), F32)], axis=0)
                          for c in new_chunks]
                aff_ref[0] = jnp.zeros(aff_ref.shape[1:], F32)
                aff_ref[0, :, pl.ds(0, n_r)] = _route_affinities(padded, t0, rw_ref, seq=seq)

    rem = seq % tt
    if rem == 0:
        tile_body(tt)
    else:
        last = pl.num_programs(1) - 1
        pl.when(ti < last)(functools.partial(tile_body, tt))
        pl.when(ti == last)(functools.partial(tile_body, rem))


def _lru_scan(xb, conv_w, conv_b, wg, bg, lam, *, seq, tt, reverse,
              hb=None, gy=None, res=None, w_out=None, g_ffn=None, router_w=None):
    b = xb.shape[0]
    nt = _cdiv(seq, tt)
    right_rows = CONV_RIGHT * SUBLANES
    last_right_block = seq // CONV_RIGHT - 1

    def tix(i):
        return nt - 1 - i if reverse else i

    tile_spec = pl.BlockSpec((1, tt * SUBLANES, LANES), lambda bi, i: (bi, tix(i), 0))

    def const_spec(shape):
        return pl.BlockSpec(shape, lambda bi, i: (0,) * len(shape))

    in_specs = [
        pl.BlockSpec((1, CONV_LEFT * SUBLANES, LANES),
                     lambda bi, i: (bi, jnp.maximum(tix(i) * (tt // CONV_LEFT) - 1, 0), 0)),
        tile_spec,
        pl.BlockSpec((1, right_rows, LANES),
                     lambda bi, i: (bi, jnp.minimum((tix(i) + 1) * (tt // CONV_RIGHT),
                                                    last_right_block), 0)),
        const_spec((CONV_WIDTH, SUBLANES, LANES)),
        const_spec((SUBLANES, LANES)),
        const_spec((LRU_HEADS, LRU_HEAD_DIM, 2 * LRU_HEAD_DIM)),
        const_spec((LRU_HEADS, 1, 2 * LRU_HEAD_DIM)),
        const_spec((1, D_MODEL)),
    ]
    args = [xb, xb, xb, (0.5 * conv_w).reshape(CONV_WIDTH, SUBLANES, LANES),
            (0.5 * conv_b).reshape(SUBLANES, LANES), wg, bg, lam.reshape(1, D_MODEL)]
    scratch = [
        pltpu.VMEM(((tt + CONV_WIDTH - 1) * SUBLANES, LANES), F32),
        pltpu.VMEM((tt * SUBLANES, LANES), F32),
        pltpu.VMEM((tt * SUBLANES, LANES), F32),
        pltpu.VMEM((tt * SUBLANES, LANES), F32),
        pltpu.VMEM((SUBLANES, LANES), F32),
    ]
    out_specs = tile_spec
    out_shape = jax.ShapeDtypeStruct(xb.shape, F32)
    if not reverse:
        route_in, route_out, route_shape = _route_specs(b, seq, tt)
        in_specs += [
            tile_spec,
            pl.BlockSpec((1, tt, D_MODEL), lambda bi, i: (bi, i, 0)),
            tile_spec,
            const_spec((D_MODEL, D_MODEL)),
        ] + route_in
        args += [hb, gy, res, w_out, _router_operand(router_w, g_ffn)]
        scratch.append(pltpu.VMEM((tt * SUBLANES, LANES), F32))
        out_specs = [tile_spec, route_out]
        out_shape = [out_shape, route_shape]
    return pl.pallas_call(
        functools.partial(_lru_scan_kernel, seq=seq, tt=tt, reverse=reverse),
        grid=(b, nt),
        in_specs=in_specs,
        out_specs=out_specs,
        out_shape=out_shape,
        scratch_shapes=scratch,
        compiler_params=_params(48, "parallel", "arbitrary"),
        name="lru_scan_rev" if reverse else "lru_scan_fwd_out",
    )(*args)


def _lru_layer(h, g, w_in, conv_w, conv_b, w_gates, b_gates, lam, w_out, g_ffn, router_w,
               *, seq, tt):
    b = h.shape[0]
    xb, gy = _lru_in(h.reshape(b * seq * SUBLANES, LANES), g, w_in, tf=LRU_IN_TILE)
    xb = xb.reshape(h.shape)
    gy = gy.reshape(b, seq, D_MODEL)
    wg = jnp.concatenate([w_gates[:, 0], w_gates[:, 1]], axis=-1).astype(BF16)
    bg = 0.5 * jnp.concatenate([b_gates[:, 0], b_gates[:, 1]], axis=-1)[:, :, None, :]
    scan = functools.partial(_lru_scan, xb, conv_w, conv_b, seq=seq, tt=tt)
    hb = scan(wg[1], bg[1], lam[1], reverse=True)
    return scan(wg[0], bg[0], lam[0], reverse=False,
                hb=hb, gy=gy, res=h, w_out=w_out.astype(BF16),
                g_ffn=g_ffn, router_w=router_w)


def _split_bf16(x):
    hi = x.astype(BF16)
    return hi, (x - hi.astype(F32)).astype(BF16)


def _route_affinities(chunks, tok0, rw_ref, *, seq):
    n = chunks[0].shape[0]
    ss = chunks[0] * chunks[0]
    for c in chunks[1:]:
        ss = ss + c * c
    inv = lax.rsqrt(jnp.sum(ss, axis=-1, keepdims=True) * (1.0 / D_MODEL) + RMS_EPS)
    x_hi, x_lo = _split_bf16(jnp.concatenate(chunks, axis=1))
    p = jnp.dot(x_hi, rw_ref[...], preferred_element_type=F32)
    q = jnp.dot(x_lo, rw_ref[:, :LANES], preferred_element_type=F32)
    logits = (((p[:, :LANES] + p[:, LANES:]) + q) * inv).T[:N_EXPERTS, :]
    tok = tok0 + lax.broadcasted_iota(jnp.int32, (1, n), 1)
    logits = jnp.where(tok < seq, logits, 0.0)
    m = jnp.max(logits, axis=0, keepdims=True)
    ex = jnp.exp(logits - m)
    return ex / jnp.sum(ex, axis=0, keepdims=True)


def _router_operand(router_w, g_ffn):
    w = jnp.pad(g_ffn[:, None] * router_w, ((0, 0), (0, LANES - N_EXPERTS)))
    w_hi, w_lo = _split_bf16(w)
    return jnp.concatenate([w_hi, w_lo], axis=1)


def _route_specs(b, seq, tt):
    seq_pad = _round_up(seq, LANES)
    return ([pl.BlockSpec((D_MODEL, 2 * LANES), lambda bi, i: (0, 0))],
            pl.BlockSpec((1, N_EXPERTS, tt), lambda bi, i: (bi, 0, i)),
            jax.ShapeDtypeStruct((b, N_EXPERTS, seq_pad), F32))


def _cumsum_blocks(x01):
    r = lax.broadcasted_iota(jnp.int32, (LANES, LANES), 0)
    c = lax.broadcasted_iota(jnp.int32, (LANES, LANES), 1)
    tri = (r <= c).astype(BF16)
    local = [jnp.dot(x01[:, j * LANES:(j + 1) * LANES].astype(BF16), tri,
                     preferred_element_type=F32) for j in range(x01.shape[1] // LANES)]
    off = jnp.zeros((x01.shape[0], 1), F32)
    outs = []
    for cs in local:
        outs.append(cs + off)
        off = off + cs[:, LANES - 1:LANES]
    return outs


def _topk_kernel(aff_ref, idx_ref, cs_ref, *, seq, cap, cap_pad, slots_pad):
    aff = aff_ref[0]
    lane = lax.broadcasted_iota(jnp.int32, aff.shape, 1)
    aff = jnp.where(lane < seq, aff, -1.0)
    capf = float(cap)
    n_blk = aff.shape[1] // LANES

    def enough(cand):
        candf = lax.bitcast_convert_type(cand, F32)
        return jnp.sum((aff >= candf).astype(F32), axis=1, keepdims=True) >= capf

    def bit_pair_step(i, cur):
        hi = cur | lax.shift_left(jnp.int32(1), 30 - 2 * i)
        lo = cur | lax.shift_left(jnp.int32(1), 29 - 2 * i)
        both = hi | lo
        return jnp.where(enough(hi), jnp.where(enough(both), both, hi),
                         jnp.where(enough(lo), lo, cur))

    cur = lax.fori_loop(0, 15, bit_pair_step, jnp.zeros((N_EXPERTS, 1), jnp.int32))
    cur = jnp.where(enough(cur | 1), cur | 1, cur)
    thr = lax.bitcast_convert_type(cur, F32)
    gt = aff > thr
    eq = aff == thr
    need = capf - jnp.sum(gt.astype(F32), axis=1, keepdims=True)
    eq_cnt = jnp.concatenate(_cumsum_blocks(eq.astype(F32)), axis=1)
    sel = gt | (eq & (eq_cnt <= need))

    blocks = _cumsum_blocks(sel.astype(F32))
    for j, cs in enumerate(blocks):
        cs_ref[pl.ds(j * N_EXPERTS, N_EXPERTS), :] = cs
    cs_ref[pl.ds(n_blk * N_EXPERTS, (LANES - n_blk) * N_EXPERTS), :] = jnp.zeros(
        ((LANES - n_blk) * N_EXPERTS, LANES), F32)
    never = jnp.full((N_EXPERTS, LANES - n_blk), 2.0 * COUNT_RADIX * 256, F32)
    blk_end = jnp.concatenate([cs[:, LANES - 1:LANES] for cs in blocks] + [never], axis=1)

    slot = lax.broadcasted_iota(jnp.int32, (cap_pad, 1), 0).astype(F32)
    lane_s = lax.broadcasted_iota(jnp.int32, (cap_pad, LANES), 1).astype(F32)
    lane_e = lax.broadcasted_iota(jnp.int32, (slots_pad, LANES), 1)
    experts = range(N_EXPERTS)
    digits = []
    for e in experts:
        cmat = cs_ref[pl.ds(e, LANES, stride=N_EXPERTS), :]
        hi = jnp.floor(cmat * (1.0 / COUNT_RADIX))
        digits.append(jnp.concatenate([hi, cmat - COUNT_RADIX * hi], axis=1).astype(BF16))
    ones = jnp.ones((LANES, LANES), BF16)

    def count_le(x):
        return jnp.dot((x <= slot).astype(BF16), ones, preferred_element_type=F32)

    full = [count_le(blk_end[e:e + 1, :]) for e in experts]
    pick = [(lane_s == full[e]).astype(BF16) for e in experts]
    rows = [jnp.dot(pick[e], digits[e], preferred_element_type=F32) for e in experts]
    rows = [COUNT_RADIX * r_[:, :LANES] + r_[:, LANES:] for r_ in rows]
    inside = [count_le(rows[e]) for e in experts]
    acc = jnp.zeros((slots_pad, LANES), F32)
    pad_rows = jnp.zeros((slots_pad - cap_pad, LANES), F32)
    for e in experts:
        col = jnp.concatenate([LANES * full[e] + inside[e], pad_rows], axis=0)
        acc = jnp.where(lane_e == e, col, acc)
    idx = acc.T[:N_EXPERTS, :].astype(jnp.int32)
    slot_l = lax.broadcasted_iota(jnp.int32, idx.shape, 1)
    idx_ref[0] = jnp.where(slot_l < cap, idx, 0)


def _topk(aff_t, *, seq, cap, cap_pad, slots_pad):
    b, _, seq_pad = aff_t.shape
    assert seq_pad // LANES <= LANES and cap < COUNT_RADIX * 256
    return pl.pallas_call(
        functools.partial(_topk_kernel, seq=seq, cap=cap, cap_pad=cap_pad,
                          slots_pad=slots_pad),
        grid=(b,),
        in_specs=[pl.BlockSpec((1, N_EXPERTS, seq_pad), lambda bi: (bi, 0, 0))],
        out_specs=pl.BlockSpec((1, N_EXPERTS, slots_pad), lambda bi: (bi, 0, 0)),
        out_shape=jax.ShapeDtypeStruct((b, N_EXPERTS, slots_pad), jnp.int32),
        scratch_shapes=[pltpu.VMEM((LANES * N_EXPERTS, LANES), F32)],
        compiler_params=_params(40, "parallel"),
        name="moe_topk",
    )(aff_t)


def _per_smem_phase(step, fn):
    phases = SMEM_ROWS // EXPERTS_PER_STEP
    for phase in range(phases):
        pl.when(step % phases == phase)(functools.partial(fn, phase * EXPERTS_PER_STEP))


def _smem_spec(width):
    steps_per_block = SMEM_ROWS // EXPERTS_PER_STEP
    blocks_per_seq = N_EXPERTS // SMEM_ROWS
    return pl.BlockSpec((1, 1, SMEM_ROWS * width),
                        lambda bi, s: (bi * blocks_per_seq + s // steps_per_block, 0, 0),
                        memory_space=pltpu.SMEM)


def _gather_kernel(idx_ref, h_ref, g_ref, xs_ref, zx_ref, *, cap_pad, slots_pad):
    h = h_ref.at[0]
    group = max(d for d in range(SUBLANES, GATHER_GROUP_MAX + 1, SUBLANES) if cap_pad % d == 0)

    def run(row0):
        for g in range(EXPERTS_PER_STEP):
            row = row0 + g

            def body(i, carry):
                toks = [idx_ref[0, 0, row * slots_pad + i * group + k] for k in range(group)]
                tiles = [h[pl.ds(pl.multiple_of(t * SUBLANES, SUBLANES), SUBLANES), :]
                         for t in toks]
                for k, tile in enumerate(tiles):
                    s0 = pl.multiple_of((i * group + k) * SUBLANES, SUBLANES)
                    zx_ref[pl.ds(s0, SUBLANES), :] = tile
                return carry

            lax.fori_loop(0, cap_pad // group, body, 0)
            u = _rms_chunks(_chunks(zx_ref, 0, cap_pad), g_ref)
            xs_ref[g, 0] = jnp.concatenate(u, axis=1).astype(BF16)

    _per_smem_phase(pl.program_id(1), run)


def _gather(idx_smem, h, g, *, cap_pad):
    b = h.shape[0]
    slots_pad = idx_smem.shape[-1] // SMEM_ROWS
    return pl.pallas_call(
        functools.partial(_gather_kernel, cap_pad=cap_pad, slots_pad=slots_pad),
        grid=(b, N_EXPERTS // EXPERTS_PER_STEP),
        in_specs=[
            _smem_spec(slots_pad),
            pl.BlockSpec((1,) + h.shape[1:], lambda bi, e: (bi, 0, 0)),
            pl.BlockSpec((1, D_MODEL), lambda bi, e: (0, 0)),
        ],
        out_specs=pl.BlockSpec((EXPERTS_PER_STEP, 1, cap_pad, D_MODEL),
                               lambda bi, e: (e, bi, 0, 0)),
        out_shape=jax.ShapeDtypeStruct((N_EXPERTS, b, cap_pad, D_MODEL), BF16),
        scratch_shapes=[pltpu.VMEM((cap_pad * SUBLANES, LANES), F32)],
        compiler_params=_params(56, "parallel", "arbitrary"),
        name="moe_gather",
    )(idx_smem, h, g.reshape(1, D_MODEL))


def _ffn_kernel(x_ref, wg_ref, wu_ref, wd_ref, o_ref, hid_ref, wdb_ref, *, ft, n_f):
    r = pl.program_id(1)
    f = pl.program_id(2)

    @pl.when(r == 0)
    def _():
        wdb_ref[pl.ds(pl.multiple_of(f * ft, ft), ft), :] = wd_ref[0, 0].astype(BF16)

    x = x_ref[0]
    hg = jnp.dot(x, wg_ref[0, 0].astype(BF16), preferred_element_type=F32)
    hu = jnp.dot(x, wu_ref[0, 0].astype(BF16), preferred_element_type=F32)
    hid_ref[f] = (hg * jax.nn.sigmoid(hg) * hu).astype(BF16)

    @pl.when(f == n_f - 1)
    def _():
        hid = jnp.concatenate([hid_ref[k] for k in range(n_f)], axis=1)
        y = jnp.dot(hid, wdb_ref[...], preferred_element_type=F32)
        _store_chunks(o_ref.at[0], 0, y.shape[0],
                      [y[:, j * LANES:(j + 1) * LANES] for j in range(N_CHUNKS)])


def _ffn(xs, w_gate, w_up, w_down, layer):
    n_exp, rows, _ = xs.shape
    d_expert = w_gate.shape[-1]
    rt = rows // FFN_ROW_TILES
    ft = min(FFN_F_TILE, d_expert)
    n_f = d_expert // ft

    def wd_tile(e, r, f):
        return (layer, e, jnp.where(r == 0, f, n_f - 1), 0)

    return pl.pallas_call(
        functools.partial(_ffn_kernel, ft=ft, n_f=n_f),
        grid=(n_exp, FFN_ROW_TILES, n_f),
        in_specs=[
            pl.BlockSpec((1, rt, D_MODEL), lambda e, r, f: (e, r, 0)),
            pl.BlockSpec((1, 1, D_MODEL, ft), lambda e, r, f: (layer, e, 0, f)),
            pl.BlockSpec((1, 1, D_MODEL, ft), lambda e, r, f: (layer, e, 0, f)),
            pl.BlockSpec((1, 1, ft, D_MODEL), wd_tile),
        ],
        out_specs=pl.BlockSpec((1, rt * SUBLANES, LANES), lambda e, r, f: (e, r, 0)),
        out_shape=jax.ShapeDtypeStruct((n_exp, rows * SUBLANES, LANES), F32),
        scratch_shapes=[pltpu.VMEM((n_f, rt, ft), BF16),
                        pltpu.VMEM((d_expert, D_MODEL), BF16)],
        compiler_params=_params(60, "parallel", "arbitrary", "arbitrary"),
        name="moe_ffn",
    )(xs, w_gate, w_up, w_down)


def _scatter_kernel(idx_ref, aff_ref, ys_ref, h_hbm, o_hbm, acc0, acc1, load_sem, store_sem,
                    *, cap, slots_pad, seq_pad):
    b = pl.program_id(0)
    e = pl.program_id(1)
    n_b = pl.num_programs(0)
    n_e = pl.num_programs(1)
    accs = (acc0, acc1)

    def load(seq_i, slot):
        return pltpu.make_async_copy(h_hbm.at[seq_i], accs[slot], load_sem.at[slot])

    def store(seq_i, slot):
        return pltpu.make_async_copy(accs[slot], o_hbm.at[seq_i], store_sem.at[slot])

    def run(slot):
        acc = accs[slot]
        other = 1 - slot

        @pl.when((b == 0) & (e == 0))
        def _():
            load(b, slot).start()

        @pl.when(e == 0)
        def _():
            load(b, slot).wait()

        @pl.when((e == SCATTER_PREFETCH_STEP) & (b >= 1))
        def _():
            store(b - 1, other).wait()

        @pl.when((e == SCATTER_PREFETCH_STEP) & (b + 1 < n_b))
        def _():
            load(b + 1, other).start()

        def add_experts(row0):
            for g in range(EXPERTS_PER_STEP):
                row = row0 + g
                ys = ys_ref.at[g, 0]

                def add_rows(slots):
                    rows, vals = [], []
                    for s in slots:
                        t = idx_ref[0, 0, row * slots_pad + s]
                        gate = aff_ref[0, 0, row * seq_pad + t]
                        r0 = pl.multiple_of(t * SUBLANES, SUBLANES)
                        s0 = (s * SUBLANES if isinstance(s, int)
                              else pl.multiple_of(s * SUBLANES, SUBLANES))
                        vals.append(acc[pl.ds(r0, SUBLANES), :]
                                    + gate * ys[pl.ds(s0, SUBLANES), :])
                        rows.append(r0)
                    for r0, v in zip(rows, vals):
                        acc[pl.ds(r0, SUBLANES), :] = v

                def body(i, carry):
                    add_rows([i * SCATTER_GROUP + k for k in range(SCATTER_GROUP)])
                    return carry

                n_groups = cap // SCATTER_GROUP
                lax.fori_loop(0, n_groups, body, 0)
                if cap % SCATTER_GROUP:
                    add_rows(list(range(n_groups * SCATTER_GROUP, cap)))

        _per_smem_phase(e, add_experts)

        @pl.when(e == n_e - 1)
        def _():
            store(b, slot).start()

        @pl.when((e == n_e - 1) & (b == n_b - 1))
        def _():
            store(b, slot).wait()

    for slot in range(2):
        pl.when(b % 2 == slot)(functools.partial(run, slot))


def _scatter(idx_smem, aff_smem, ys, h, *, cap, cap_pad):
    b = h.shape[0]
    slots_pad = idx_smem.shape[-1] // SMEM_ROWS
    seq_pad = aff_smem.shape[-1] // SMEM_ROWS
    n_steps = N_EXPERTS // EXPERTS_PER_STEP
    assert 1 <= SCATTER_PREFETCH_STEP <= n_steps - 1
    return pl.pallas_call(
        functools.partial(_scatter_kernel, cap=cap, slots_pad=slots_pad, seq_pad=seq_pad),
        grid=(b, n_steps),
        in_specs=[
            _smem_spec(slots_pad),
            _smem_spec(seq_pad),
            pl.BlockSpec((EXPERTS_PER_STEP, 1, cap_pad * SUBLANES, LANES),
                         lambda bi, e: (e, bi, 0, 0)),
            pl.BlockSpec(memory_space=pl.ANY),
        ],
        out_specs=pl.BlockSpec(memory_space=pl.ANY),
        out_shape=jax.ShapeDtypeStruct(h.shape, F32),
        scratch_shapes=[pltpu.VMEM(h.shape[1:], F32), pltpu.VMEM(h.shape[1:], F32),
                        pltpu.SemaphoreType.DMA((2,)), pltpu.SemaphoreType.DMA((2,))],
        compiler_params=_params(56, "arbitrary", "arbitrary"),
        name="moe_scatter",
    )(idx_smem, aff_smem, ys, h)


def _moe_layer(h, aff_t, g, w_gate, w_up, w_down, layer, *, seq):
    b = h.shape[0]
    cap = CAPACITY_FACTOR * seq // N_EXPERTS
    cap_pad = _round_up(cap, BF16_ROWS)
    assert (b * cap_pad) % (FFN_ROW_TILES * BF16_ROWS) == 0
    slots_pad = _round_up(cap_pad, LANES)
    seq_pad = aff_t.shape[-1]
    idx = _topk(aff_t, seq=seq, cap=cap, cap_pad=cap_pad, slots_pad=slots_pad)
    idx_smem = idx.reshape(b * N_EXPERTS // SMEM_ROWS, 1, SMEM_ROWS * slots_pad)
    aff_smem = aff_t.reshape(b * N_EXPERTS // SMEM_ROWS, 1, SMEM_ROWS * seq_pad)
    xs = _gather(idx_smem, h, g, cap_pad=cap_pad)
    ys = _ffn(xs.reshape(N_EXPERTS, b * cap_pad, D_MODEL), w_gate, w_up, w_down, layer)
    ys = ys.reshape(N_EXPERTS, b, cap_pad * SUBLANES, LANES)
    return _scatter(idx_smem, aff_smem, ys, h, cap=cap, cap_pad=cap_pad)


def _final_kernel(ha_ref, hb_ref, g_ref, o_ref, *, tt):
    ha = ha_ref.at[0]
    hb = hb_ref.at[0]
    ch = [jnp.concatenate(
        [ha[pl.ds(N_META * SUBLANES + j, tt - N_META, stride=SUBLANES), :],
         hb[pl.ds(j, N_META, stride=SUBLANES), :]], axis=0) for j in range(N_CHUNKS)]
    o_ref[0] = jnp.concatenate(_rms_chunks(ch, g_ref), axis=1)


def _final(h, g, *, seq, tt):
    b = h.shape[0]
    s_out = seq - N_META
    assert s_out % tt == 0
    assert tt % N_META == 0
    return pl.pallas_call(
        functools.partial(_final_kernel, tt=tt),
        grid=(b, s_out // tt),
        in_specs=[
            pl.BlockSpec((1, tt * SUBLANES, LANES), lambda bi, i: (bi, i, 0)),
            pl.BlockSpec((1, N_META * SUBLANES, LANES),
                         lambda bi, i: (bi, (i + 1) * (tt // N_META), 0)),
            pl.BlockSpec((1, D_MODEL), lambda bi, i: (0, 0)),
        ],
        out_specs=pl.BlockSpec((1, tt, D_MODEL), lambda bi, i: (bi, i, 0)),
        out_shape=jax.ShapeDtypeStruct((b, s_out, D_MODEL), F32),
        compiler_params=_params(40, "parallel", "parallel"),
        name="final_norm",
    )(h, h, g.reshape(1, D_MODEL))


def kernel(x, meta_tokens, norm_mix, norm_ffn, norm_final, pool_w, pool_scale, lru_w_in, lru_conv_w, lru_conv_b, lru_w_gates, lru_b_gates, lru_lambda, lru_w_out, router_w, moe_w_gate, moe_w_up, moe_w_down):
    b, s, d = x.shape
    assert d == D_MODEL
    seq = s + N_META
    tt = TIME_TILE
    depth = norm_mix.shape[0]
    h = _embed(x, meta_tokens.astype(x.dtype), tt=tt)
    for i in range(depth):
        j = i // 2
        if i % 2 == 0:
            h, aff_t = _pool_layer(h, norm_mix[i], pool_w[j], pool_scale[j],
                                   norm_ffn[i], router_w[i], seq=seq, tt=tt)
        else:
            h, aff_t = _lru_layer(h, norm_mix[i], lru_w_in[j], lru_conv_w[j], lru_conv_b[j],
                                  lru_w_gates[j], lru_b_gates[j], lru_lambda[j], lru_w_out[j],
                                  norm_ffn[i], router_w[i], seq=seq, tt=tt)
        h = _moe_layer(h, aff_t, norm_ffn[i], moe_w_gate, moe_w_up, moe_w_down, i, seq=seq)
    return _final(h, norm_final, seq=seq, tt=tt)
```

```python
import functools

import jax
import jax.numpy as jnp
from jax import lax
from jax.experimental import pallas as pl
from jax.experimental.pallas import tpu as pltpu

F32 = jnp.float32
BF16 = jnp.bfloat16

LANES = 128
SUBLANES = 8
BF16_ROWS = 16
D_MODEL = 1024
N_CHUNKS = D_MODEL // LANES
assert N_CHUNKS == SUBLANES
N_META = 16
POOL_WINDOWS = (2, 4, 8, 16)
POOL_GROUP = D_MODEL // len(POOL_WINDOWS)
CHUNKS_PER_GROUP = POOL_GROUP // LANES
POOL_HALO = 8
LRU_HEADS = 4
LRU_HEAD_DIM = D_MODEL // LRU_HEADS
CHUNKS_PER_HEAD = LRU_HEAD_DIM // LANES
LRU_C = 8.0
CONV_WIDTH = 4
CONV_LEFT = 1
CONV_RIGHT = CONV_WIDTH - 1 - CONV_LEFT
N_EXPERTS = 16
CAPACITY_FACTOR = 2
RMS_EPS = 1e-6
TIME_TILE = 512
FFN_ROW_TILES = 4
FFN_F_TILE = 1024
LRU_IN_TILE = 1024
GATHER_GROUP_MAX = 48
EXPERTS_PER_STEP = 4
SMEM_ROWS = 8
SCATTER_PREFETCH_STEP = 1
SCATTER_GROUP = 8
COUNT_RADIX = 32.0
MIB = 1024 * 1024


def _cdiv(a, b):
    return -(-a // b)


def _round_up(a, b):
    return _cdiv(a, b) * b


def _chunks(ref2d, tok0, n):
    return [ref2d[pl.ds(tok0 * SUBLANES + j, n, stride=SUBLANES), :]
            for j in range(N_CHUNKS)]


def _store_chunks(ref2d, tok0, n, chunks):
    for j in range(N_CHUNKS):
        ref2d[pl.ds(tok0 * SUBLANES + j, n, stride=SUBLANES), :] = chunks[j]


def _lane_chunk(ref, j):
    return ref[:, j * LANES:(j + 1) * LANES]


def _rms_chunks(chunks, g_ref):
    ss = chunks[0] * chunks[0]
    for c in chunks[1:]:
        ss = ss + c * c
    ms = jnp.sum(ss, axis=-1, keepdims=True) * (1.0 / D_MODEL)
    inv = lax.rsqrt(ms + RMS_EPS)
    return [c * inv * _lane_chunk(g_ref, j) for j, c in enumerate(chunks)]


def _token_ids(tok0, n):
    return tok0 + lax.broadcasted_iota(jnp.int32, (n, 1), 0)


def _params(vmem_mib, *sem):
    return pltpu.CompilerParams(dimension_semantics=sem,
                                vmem_limit_bytes=vmem_mib * MIB)


def _embed_kernel(meta_ref, xa_ref, xb_ref, o_ref, *, tt):
    i = pl.program_id(1)
    out = o_ref.at[0]
    body = xb_ref[0, :tt - N_META, :]
    _store_chunks(out, N_META, tt - N_META,
                  [body[:, j * LANES:(j + 1) * LANES] for j in range(N_CHUNKS)])

    def head(src):
        _store_chunks(out, 0, N_META,
                      [src[:, j * LANES:(j + 1) * LANES] for j in range(N_CHUNKS)])

    @pl.when(i == 0)
    def _():
        head(meta_ref[...])

    @pl.when(i > 0)
    def _():
        head(xa_ref[0])


def _embed(x, meta, *, tt):
    b, s, _ = x.shape
    seq = s + N_META
    assert s % tt == 0 and tt % N_META == 0
    last = s // tt - 1
    metas_per_tile = tt // N_META
    return pl.pallas_call(
        functools.partial(_embed_kernel, tt=tt),
        grid=(b, _cdiv(seq, tt)),
        in_specs=[
            pl.BlockSpec((N_META, D_MODEL), lambda bi, i: (0, 0)),
            pl.BlockSpec((1, N_META, D_MODEL),
                         lambda bi, i: (bi, jnp.maximum(i * metas_per_tile - 1, 0), 0)),
            pl.BlockSpec((1, tt, D_MODEL), lambda bi, i: (bi, jnp.minimum(i, last), 0)),
        ],
        out_specs=pl.BlockSpec((1, tt * SUBLANES, LANES), lambda bi, i: (bi, i, 0)),
        out_shape=jax.ShapeDtypeStruct((b, seq * SUBLANES, LANES), F32),
        compiler_params=_params(40, "parallel", "parallel"),
        name="embed",
    )(meta, x, x)


def _pool_kernel(hp_ref, hm_ref, hn_ref, g_ref, w_ref, sc_ref, rw_ref,
                 o_ref, aff_ref, ext_ref, s2_ref, s4_ref, s8_ref, p_ref, *, seq, tt):
    ti = pl.program_id(1)
    t0 = ti * tt
    hm = hm_ref.at[0]
    out = o_ref.at[0]

    def tile_body(n):
        def norm_into_ext(src, n_tok, ext_tok0, tok0):
            tok = _token_ids(tok0, n_tok)
            valid = (tok >= 0) & (tok < seq)
            ch = [jnp.where(valid, c, 0.0) for c in _chunks(src, 0, n_tok)]
            _store_chunks(ext_ref, ext_tok0, n_tok, _rms_chunks(ch, g_ref))

        norm_into_ext(hp_ref.at[0], POOL_HALO, 0, t0 - POOL_HALO)
        norm_into_ext(hm, n, POOL_HALO, t0)
        norm_into_ext(hn_ref.at[0], POOL_HALO, POOL_HALO + n, t0 + n)

        def rows(a, b):
            return pl.ds(a * SUBLANES, (b - a) * SUBLANES)

        lo, hi = 1, n + 2 * POOL_HALO
        s2_ref[rows(lo, hi), :] = ext_ref[rows(lo - 1, hi - 1), :] + ext_ref[rows(lo, hi), :]
        prev, half = s2_ref, 1
        for nxt in (s4_ref, s8_ref):
            lo, hi = lo + half, hi - half
            nxt[rows(lo, hi), :] = (prev[rows(lo - half, hi - half), :]
                                    + prev[rows(lo + half, hi + half), :])
            prev, half = nxt, 2 * half
        m0, m1 = POOL_HALO, POOL_HALO + n
        assert lo + half <= m0 and m1 <= hi - half and 4 * half == POOL_WINDOWS[-1]
        main = pl.ds(0, n * SUBLANES)

        grp = lax.broadcasted_iota(jnp.int32, (SUBLANES, LANES), 0) // CHUNKS_PER_GROUP

        def window_sums():
            def tiles(v):
                return v.reshape(n, SUBLANES, LANES)

            s16 = prev[rows(m0 - half, m1 - half), :] + prev[rows(m0 + half, m1 + half), :]
            return jnp.where(grp == 0, tiles(s2_ref[rows(m0, m1), :]),
                             jnp.where(grp == 1, tiles(s4_ref[rows(m0, m1), :]),
                                       jnp.where(grp == 2, tiles(s8_ref[rows(m0, m1), :]),
                                                 tiles(s16))))

        near_end = (ti == 0) | (t0 + n > seq - POOL_HALO)

        @pl.when(jnp.logical_not(near_end))
        def _():
            inv_win = jnp.where(grp == 0, 0.5, jnp.where(grp == 1, 0.25,
                                                         jnp.where(grp == 2, 0.125, 0.0625)))
            p_ref[main, :] = ((window_sums() * inv_win).reshape(n * SUBLANES, LANES)
                              - ext_ref[rows(m0, m1), :])

        @pl.when(near_end)
        def _():
            r = lax.broadcasted_iota(jnp.int32, (n * SUBLANES, 1), 0)
            tok = t0 + (r >> 3)
            left = lax.shift_left(jnp.int32(1), (r & (SUBLANES - 1)) // CHUNKS_PER_GROUP)
            cnt = jnp.minimum(tok + left, seq) - jnp.maximum(tok - left, 0)
            cnt = jnp.maximum(cnt, 1).astype(F32)
            p_ref[main, :] = (window_sums().reshape(n * SUBLANES, LANES) / cnt
                              - ext_ref[rows(m0, m1), :])

        new_chunks = []
        for g in range(len(POOL_WINDOWS)):
            group_chunks = range(g * CHUNKS_PER_GROUP, (g + 1) * CHUNKS_PER_GROUP)
            p = jnp.concatenate([p_ref[pl.ds(c, n, stride=SUBLANES), :] for c in group_chunks],
                                axis=1).astype(BF16)
            y = jnp.dot(p, w_ref[g], preferred_element_type=F32)
            for q, c in enumerate(group_chunks):
                res = hm[pl.ds(c, n, stride=SUBLANES), :]
                new_chunks.append(res + y[:, q * LANES:(q + 1) * LANES] * _lane_chunk(sc_ref, c))
                out[pl.ds(c, n, stride=SUBLANES), :] = new_chunks[-1]
        if n == tt:
            aff_ref[0] = _route_affinities(new_chunks, t0, rw_ref, seq=seq)
        else:
            n_r = _round_up(n, LANES)
            padded = [jnp.concatenate([c, jnp.zeros((n_r - n, LANES), F32)], axis=0)
                      for c in new_chunks]
            aff_ref[0] = jnp.zeros(aff_ref.shape[1:], F32)
            aff_ref[0, :, pl.ds(0, n_r)] = _route_affinities(padded, t0, rw_ref, seq=seq)

    rem = seq % tt
    if rem == 0:
        tile_body(tt)
    else:
        last = pl.num_programs(1) - 1
        pl.when(ti < last)(functools.partial(tile_body, tt))
        pl.when(ti == last)(functools.partial(tile_body, rem))


def _pool_layer(h, g, w, scale, g_ffn, router_w, *, seq, tt):
    b = h.shape[0]
    nt = _cdiv(seq, tt)
    halo_rows = POOL_HALO * SUBLANES
    blocks_per_tile = tt // POOL_HALO
    last_halo_block = seq // POOL_HALO - 1
    route_in, route_out, route_shape = _route_specs(b, seq, tt)
    return pl.pallas_call(
        functools.partial(_pool_kernel, seq=seq, tt=tt),
        grid=(b, nt),
        in_specs=[
            pl.BlockSpec((1, halo_rows, LANES),
                         lambda bi, i: (bi, jnp.maximum(i * blocks_per_tile - 1, 0), 0)),
            pl.BlockSpec((1, tt * SUBLANES, LANES), lambda bi, i: (bi, i, 0)),
            pl.BlockSpec((1, halo_rows, LANES),
                         lambda bi, i: (bi, jnp.minimum((i + 1) * blocks_per_tile,
                                                        last_halo_block), 0)),
            pl.BlockSpec((1, D_MODEL), lambda bi, i: (0, 0)),
            pl.BlockSpec((len(POOL_WINDOWS), POOL_GROUP, POOL_GROUP),
                         lambda bi, i: (0, 0, 0)),
            pl.BlockSpec((1, D_MODEL), lambda bi, i: (0, 0)),
        ] + route_in,
        out_specs=[pl.BlockSpec((1, tt * SUBLANES, LANES), lambda bi, i: (bi, i, 0)),
                   route_out],
        out_shape=[jax.ShapeDtypeStruct(h.shape, F32), route_shape],
        scratch_shapes=[pltpu.VMEM(((tt + 2 * POOL_HALO) * SUBLANES, LANES), F32)] * 4
        + [pltpu.VMEM((tt * SUBLANES, LANES), F32)],
        compiler_params=_params(40, "parallel", "arbitrary"),
        name="pool_mixer",
    )(h, h, h, g.reshape(1, D_MODEL), w.astype(BF16), scale.reshape(1, D_MODEL),
      _router_operand(router_w, g_ffn))


def _gelu_tanh(x):
    c = 0.7978845608028654
    return 0.5 * x * (1.0 + jnp.tanh(c * (x + 0.044715 * (x * x * x))))


def _lru_in_kernel(h_ref, g_ref, w_ref, xb_ref, gy_ref, *, tf):
    u = jnp.concatenate(_rms_chunks(_chunks(h_ref, 0, tf), g_ref), axis=1).astype(BF16)
    yb = jnp.dot(u, w_ref[:, D_MODEL:], preferred_element_type=F32)
    gy_ref[...] = _gelu_tanh(yb)
    xb = jnp.dot(u, w_ref[:, :D_MODEL], preferred_element_type=F32)
    _store_chunks(xb_ref, 0, tf,
                  [xb[:, j * LANES:(j + 1) * LANES] for j in range(N_CHUNKS)])


def _lru_in(h_flat, g, w_in, *, tf):
    n_tok = h_flat.shape[0] // SUBLANES
    return pl.pallas_call(
        functools.partial(_lru_in_kernel, tf=tf),
        grid=(_cdiv(n_tok, tf),),
        in_specs=[
            pl.BlockSpec((tf * SUBLANES, LANES), lambda i: (i, 0)),
            pl.BlockSpec((1, D_MODEL), lambda i: (0, 0)),
            pl.BlockSpec((D_MODEL, 2 * D_MODEL), lambda i: (0, 0)),
        ],
        out_specs=[
            pl.BlockSpec((tf * SUBLANES, LANES), lambda i: (i, 0)),
            pl.BlockSpec((tf, D_MODEL), lambda i: (i, 0)),
        ],
        out_shape=[
            jax.ShapeDtypeStruct(h_flat.shape, F32),
            jax.ShapeDtypeStruct((n_tok, D_MODEL), F32),
        ],
        compiler_params=_params(56, "parallel"),
        name="lru_in_proj",
    )(h_flat, g.reshape(1, D_MODEL), w_in.astype(BF16))


def _softplus(x):
    return jnp.maximum(x, 0.0) + jnp.log1p(jnp.exp(-jnp.abs(x)))


def _lru_scan_kernel(*refs, seq, tt, reverse):
    if reverse:
        (xp_ref, xm_ref, xn_ref, cw_ref, cb_ref, wg_ref, bg_ref, lam_ref,
         o_ref, ext_ref, xc_ref, a_ref, b_ref, carry_ref) = refs
        hs = o_ref.at[0]
    else:
        (xp_ref, xm_ref, xn_ref, cw_ref, cb_ref, wg_ref, bg_ref, lam_ref,
         hb_ref, gy_ref, res_ref, wo_ref, rw_ref,
         o_ref, aff_ref, ext_ref, xc_ref, a_ref, b_ref, carry_ref, hs) = refs
    step = pl.program_id(1)
    nt = pl.num_programs(1)
    ti = nt - 1 - step if reverse else step
    t0 = ti * tt

    @pl.when(step == 0)
    def _():
        carry_ref[...] = jnp.zeros_like(carry_ref)

    def masked_rows(src, n_tok, tok0):
        rows = n_tok * SUBLANES
        tok = tok0 + (lax.broadcasted_iota(jnp.int32, (rows, 1), 0) >> 3)
        return jnp.where((tok >= 0) & (tok < seq), src[...], 0.0)

    def tile_body(n):
        ext_ref[pl.ds(0, CONV_LEFT * SUBLANES), :] = masked_rows(
            xp_ref.at[0], CONV_LEFT, t0 - CONV_LEFT)
        ext_ref[pl.ds(CONV_LEFT * SUBLANES, n * SUBLANES), :] = xm_ref[0, pl.ds(0, n * SUBLANES), :]
        right_rows = pl.ds((CONV_LEFT + n) * SUBLANES, CONV_RIGHT * SUBLANES)
        if n == tt:
            ext_ref[right_rows, :] = masked_rows(xn_ref.at[0], CONV_RIGHT, t0 + tt)
        else:
            ext_ref[right_rows, :] = jnp.zeros((CONV_RIGHT * SUBLANES, LANES), F32)

        xc = cb_ref[...]
        for k in range(CONV_WIDTH):
            xc = xc + (ext_ref[pl.ds(k * SUBLANES, n * SUBLANES), :]
                       .reshape(n, SUBLANES, LANES) * cw_ref[k])
        xc_ref[pl.ds(0, n * SUBLANES), :] = xc.reshape(n * SUBLANES, LANES)

        for hh in range(LRU_HEADS):
            head_chunks = range(hh * CHUNKS_PER_HEAD, (hh + 1) * CHUNKS_PER_HEAD)
            xh = jnp.concatenate(
                [xc_ref[pl.ds(c, n, stride=SUBLANES), :] for c in head_chunks],
                axis=1)
            th = jnp.tanh(jnp.dot(xh.astype(BF16), wg_ref[hh], preferred_element_type=F32)
                          + bg_ref[hh])
            lam = lam_ref[:, hh * LRU_HEAD_DIM:(hh + 1) * LRU_HEAD_DIM]
            half_c = (-0.5 * LRU_C) * _softplus(-lam)
            log_a = th[:, :LRU_HEAD_DIM] * half_c + half_c
            a = jnp.exp(log_a)
            bb = (jnp.sqrt(jnp.tanh(-log_a) * (a * a + 1.0))
                  * ((th[:, LRU_HEAD_DIM:] + 1.0) * xh))
            for q, c in enumerate(head_chunks):
                a_ref[pl.ds(c, n, stride=SUBLANES), :] = a[:, q * LANES:(q + 1) * LANES]
                b_ref[pl.ds(c, n, stride=SUBLANES), :] = bb[:, q * LANES:(q + 1) * LANES]

        def scan_body(i, h):
            for k in range(SUBLANES):
                t = (n - 1 - (i * SUBLANES + k)) if reverse else (i * SUBLANES + k)
                r0 = pl.multiple_of(t * SUBLANES, SUBLANES)
                h = a_ref[pl.ds(r0, SUBLANES), :] * h + b_ref[pl.ds(r0, SUBLANES), :]
                hs[pl.ds(r0, SUBLANES), :] = h
            return h

        carry_ref[...] = lax.fori_loop(0, n // SUBLANES, scan_body, carry_ref[...])

        if not reverse:
            rows = pl.ds(0, n * SUBLANES)
            hs[rows, :] = hs[rows, :] + hb_ref[0, rows, :]
            hsum = jnp.concatenate(_chunks(hs, 0, n), axis=1)
            z = (hsum * gy_ref[0, pl.ds(0, n), :]).astype(BF16)
            y = jnp.dot(z, wo_ref[...], preferred_element_type=F32)
            new_chunks = [c + y[:, j * LANES:(j + 1) * LANES]
                          for j, c in enumerate(_chunks(res_ref.at[0], 0, n))]
            _store_chunks(o_ref.at[0], 0, n, new_chunks)
            if n == tt:
                aff_ref[0] = _route_affinities(new_chunks, t0, rw_ref, seq=seq)
            else:
                n_r = _round_up(n, LANES)
                padded = [jnp.concatenate([c, jnp.zeros((n_r - n, LANES), F32)], axis=0)
                          for c in new_chunks]
                aff_ref[0] = jnp.zeros(aff_ref.shape[1:], F32)
                aff_ref[0, :, pl.ds(0, n_r)] = _route_affinities(padded, t0, rw_ref, seq=seq)

    rem = seq % tt
    if rem == 0:
        tile_body(tt)
    else:
        last = pl.num_programs(1) - 1
        pl.when(ti < last)(functools.partial(tile_body, tt))
        pl.when(ti == last)(functools.partial(tile_body, rem))


def _lru_scan(xb, conv_w, conv_b, wg, bg, lam, *, seq, tt, reverse,
              hb=None, gy=None, res=None, w_out=None, g_ffn=None, router_w=None):
    b = xb.shape[0]
    nt = _cdiv(seq, tt)
    right_rows = CONV_RIGHT * SUBLANES
    last_right_block = seq // CONV_RIGHT - 1

    def tix(i):
        return nt - 1 - i if reverse else i

    tile_spec = pl.BlockSpec((1, tt * SUBLANES, LANES), lambda bi, i: (bi, tix(i), 0))

    def const_spec(shape):
        return pl.BlockSpec(shape, lambda bi, i: (0,) * len(shape))

    in_specs = [
        pl.BlockSpec((1, CONV_LEFT * SUBLANES, LANES),
                     lambda bi, i: (bi, jnp.maximum(tix(i) * (tt // CONV_LEFT) - 1, 0), 0)),
        tile_spec,
        pl.BlockSpec((1, right_rows, LANES),
                     lambda bi, i: (bi, jnp.minimum((tix(i) + 1) * (tt // CONV_RIGHT),
                                                    last_right_block), 0)),
        const_spec((CONV_WIDTH, SUBLANES, LANES)),
        const_spec((SUBLANES, LANES)),
        const_spec((LRU_HEADS, LRU_HEAD_DIM, 2 * LRU_HEAD_DIM)),
        const_spec((LRU_HEADS, 1, 2 * LRU_HEAD_DIM)),
        const_spec((1, D_MODEL)),
    ]
    args = [xb, xb, xb, (0.5 * conv_w).reshape(CONV_WIDTH, SUBLANES, LANES),
            (0.5 * conv_b).reshape(SUBLANES, LANES), wg, bg, lam.reshape(1, D_MODEL)]
    scratch = [
        pltpu.VMEM(((tt + CONV_WIDTH - 1) * SUBLANES, LANES), F32),
        pltpu.VMEM((tt * SUBLANES, LANES), F32),
        pltpu.VMEM((tt * SUBLANES, LANES), F32),
        pltpu.VMEM((tt * SUBLANES, LANES), F32),
        pltpu.VMEM((SUBLANES, LANES), F32),
    ]
    out_specs = tile_spec
    out_shape = jax.ShapeDtypeStruct(xb.shape, F32)
    if not reverse:
        route_in, route_out, route_shape = _route_specs(b, seq, tt)
        in_specs += [
            tile_spec,
            pl.BlockSpec((1, tt, D_MODEL), lambda bi, i: (bi, i, 0)),
            tile_spec,
            const_spec((D_MODEL, D_MODEL)),
        ] + route_in
        args += [hb, gy, res, w_out, _router_operand(router_w, g_ffn)]
        scratch.append(pltpu.VMEM((tt * SUBLANES, LANES), F32))
        out_specs = [tile_spec, route_out]
        out_shape = [out_shape, route_shape]
    return pl.pallas_call(
        functools.partial(_lru_scan_kernel, seq=seq, tt=tt, reverse=reverse),
        grid=(b, nt),
        in_specs=in_specs,
        out_specs=out_specs,
        out_shape=out_shape,
        scratch_shapes=scratch,
        compiler_params=_params(48, "parallel", "arbitrary"),
        name="lru_scan_rev" if reverse else "lru_scan_fwd_out",
    )(*args)


def _lru_layer(h, g, w_in, conv_w, conv_b, w_gates, b_gates, lam, w_out, g_ffn, router_w,
               *, seq, tt):
    b = h.shape[0]
    xb, gy = _lru_in(h.reshape(b * seq * SUBLANES, LANES), g, w_in, tf=LRU_IN_TILE)
    xb = xb.reshape(h.shape)
    gy = gy.reshape(b, seq, D_MODEL)
    wg = jnp.concatenate([w_gates[:, 0], w_gates[:, 1]], axis=-1).astype(BF16)
    bg = 0.5 * jnp.concatenate([b_gates[:, 0], b_gates[:, 1]], axis=-1)[:, :, None, :]
    scan = functools.partial(_lru_scan, xb, conv_w, conv_b, seq=seq, tt=tt)
    hb = scan(wg[1], bg[1], lam[1], reverse=True)
    return scan(wg[0], bg[0], lam[0], reverse=False,
                hb=hb, gy=gy, res=h, w_out=w_out.astype(BF16),
                g_ffn=g_ffn, router_w=router_w)


def _split_bf16(x):
    hi = x.astype(BF16)
    return hi, (x - hi.astype(F32)).astype(BF16)


def _route_affinities(chunks, tok0, rw_ref, *, seq):
    n = chunks[0].shape[0]
    ss = chunks[0] * chunks[0]
    for c in chunks[1:]:
        ss = ss + c * c
    inv = lax.rsqrt(jnp.sum(ss, axis=-1, keepdims=True) * (1.0 / D_MODEL) + RMS_EPS)
    x_hi, x_lo = _split_bf16(jnp.concatenate(chunks, axis=1))
    p = jnp.dot(x_hi, rw_ref[...], preferred_element_type=F32)
    q = jnp.dot(x_lo, rw_ref[:, :LANES], preferred_element_type=F32)
    logits = (((p[:, :LANES] + p[:, LANES:]) + q) * inv).T[:N_EXPERTS, :]
    tok = tok0 + lax.broadcasted_iota(jnp.int32, (1, n), 1)
    logits = jnp.where(tok < seq, logits, 0.0)
    m = jnp.max(logits, axis=0, keepdims=True)
    ex = jnp.exp(logits - m)
    return ex / jnp.sum(ex, axis=0, keepdims=True)


def _router_operand(router_w, g_ffn):
    w = jnp.pad(g_ffn[:, None] * router_w, ((0, 0), (0, LANES - N_EXPERTS)))
    w_hi, w_lo = _split_bf16(w)
    return jnp.concatenate([w_hi, w_lo], axis=1)


def _route_specs(b, seq, tt):
    seq_pad = _round_up(seq, LANES)
    return ([pl.BlockSpec((D_MODEL, 2 * LANES), lambda bi, i: (0, 0))],
            pl.BlockSpec((1, N_EXPERTS, tt), lambda bi, i: (bi, 0, i)),
            jax.ShapeDtypeStruct((b, N_EXPERTS, seq_pad), F32))


def _cumsum_blocks(x01):
    r = lax.broadcasted_iota(jnp.int32, (LANES, LANES), 0)
    c = lax.broadcasted_iota(jnp.int32, (LANES, LANES), 1)
    tri = (r <= c).astype(BF16)
    local = [jnp.dot(x01[:, j * LANES:(j + 1) * LANES].astype(BF16), tri,
                     preferred_element_type=F32) for j in range(x01.shape[1] // LANES)]
    off = jnp.zeros((x01.shape[0], 1), F32)
    outs = []
    for cs in local:
        outs.append(cs + off)
        off = off + cs[:, LANES - 1:LANES]
    return outs


def _topk_kernel(aff_ref, idx_ref, cs_ref, *, seq, cap, cap_pad, slots_pad):
    aff = aff_ref[0]
    lane = lax.broadcasted_iota(jnp.int32, aff.shape, 1)
    aff = jnp.where(lane < seq, aff, -1.0)
    capf = float(cap)
    n_blk = aff.shape[1] // LANES

    def enough(cand):
        candf = lax.bitcast_convert_type(cand, F32)
        return jnp.sum((aff >= candf).astype(F32), axis=1, keepdims=True) >= capf

    def bit_pair_step(i, cur):
        hi = cur | lax.shift_left(jnp.int32(1), 30 - 2 * i)
        lo = cur | lax.shift_left(jnp.int32(1), 29 - 2 * i)
        both = hi | lo
        return jnp.where(enough(hi), jnp.where(enough(both), both, hi),
                         jnp.where(enough(lo), lo, cur))

    cur = lax.fori_loop(0, 15, bit_pair_step, jnp.zeros((N_EXPERTS, 1), jnp.int32))
    cur = jnp.where(enough(cur | 1), cur | 1, cur)
    thr = lax.bitcast_convert_type(cur, F32)
    gt = aff > thr
    eq = aff == thr
    need = capf - jnp.sum(gt.astype(F32), axis=1, keepdims=True)
    eq_cnt = jnp.concatenate(_cumsum_blocks(eq.astype(F32)), axis=1)
    sel = gt | (eq & (eq_cnt <= need))

    blocks = _cumsum_blocks(sel.astype(F32))
    for j, cs in enumerate(blocks):
        cs_ref[pl.ds(j * N_EXPERTS, N_EXPERTS), :] = cs
    cs_ref[pl.ds(n_blk * N_EXPERTS, (LANES - n_blk) * N_EXPERTS), :] = jnp.zeros(
        ((LANES - n_blk) * N_EXPERTS, LANES), F32)
    never = jnp.full((N_EXPERTS, LANES - n_blk), 2.0 * COUNT_RADIX * 256, F32)
    blk_end = jnp.concatenate([cs[:, LANES - 1:LANES] for cs in blocks] + [never], axis=1)

    slot = lax.broadcasted_iota(jnp.int32, (cap_pad, 1), 0).astype(F32)
    lane_s = lax.broadcasted_iota(jnp.int32, (cap_pad, LANES), 1).astype(F32)
    lane_e = lax.broadcasted_iota(jnp.int32, (slots_pad, LANES), 1)
    experts = range(N_EXPERTS)
    digits = []
    for e in experts:
        cmat = cs_ref[pl.ds(e, LANES, stride=N_EXPERTS), :]
        hi = jnp.floor(cmat * (1.0 / COUNT_RADIX))
        digits.append(jnp.concatenate([hi, cmat - COUNT_RADIX * hi], axis=1).astype(BF16))
    ones = jnp.ones((LANES, LANES), BF16)

    def count_le(x):
        return jnp.dot((x <= slot).astype(BF16), ones, preferred_element_type=F32)

    full = [count_le(blk_end[e:e + 1, :]) for e in experts]
    pick = [(lane_s == full[e]).astype(BF16) for e in experts]
    rows = [jnp.dot(pick[e], digits[e], preferred_element_type=F32) for e in experts]
    rows = [COUNT_RADIX * r_[:, :LANES] + r_[:, LANES:] for r_ in rows]
    inside = [count_le(rows[e]) for e in experts]
    acc = jnp.zeros((slots_pad, LANES), F32)
    pad_rows = jnp.zeros((slots_pad - cap_pad, LANES), F32)
    for e in experts:
        col = jnp.concatenate([LANES * full[e] + inside[e], pad_rows], axis=0)
        acc = jnp.where(lane_e == e, col, acc)
    idx = acc.T[:N_EXPERTS, :].astype(jnp.int32)
    slot_l = lax.broadcasted_iota(jnp.int32, idx.shape, 1)
    idx_ref[0] = jnp.where(slot_l < cap, idx, 0)


def _topk(aff_t, *, seq, cap, cap_pad, slots_pad):
    b, _, seq_pad = aff_t.shape
    assert seq_pad // LANES <= LANES and cap < COUNT_RADIX * 256
    return pl.pallas_call(
        functools.partial(_topk_kernel, seq=seq, cap=cap, cap_pad=cap_pad,
                          slots_pad=slots_pad),
        grid=(b,),
        in_specs=[pl.BlockSpec((1, N_EXPERTS, seq_pad), lambda bi: (bi, 0, 0))],
        out_specs=pl.BlockSpec((1, N_EXPERTS, slots_pad), lambda bi: (bi, 0, 0)),
        out_shape=jax.ShapeDtypeStruct((b, N_EXPERTS, slots_pad), jnp.int32),
        scratch_shapes=[pltpu.VMEM((LANES * N_EXPERTS, LANES), F32)],
        compiler_params=_params(40, "parallel"),
        name="moe_topk",
    )(aff_t)


def _per_smem_phase(step, fn):
    phases = SMEM_ROWS // EXPERTS_PER_STEP
    for phase in range(phases):
        pl.when(step % phases == phase)(functools.partial(fn, phase * EXPERTS_PER_STEP))


def _smem_spec(width):
    steps_per_block = SMEM_ROWS // EXPERTS_PER_STEP
    blocks_per_seq = N_EXPERTS // SMEM_ROWS
    return pl.BlockSpec((1, 1, SMEM_ROWS * width),
                        lambda bi, s: (bi * blocks_per_seq + s // steps_per_block, 0, 0),
                        memory_space=pltpu.SMEM)


def _gather_kernel(idx_ref, h_ref, g_ref, xs_ref, zx_ref, *, cap_pad, slots_pad):
    h = h_ref.at[0]
    group = max(d for d in range(SUBLANES, GATHER_GROUP_MAX + 1, SUBLANES) if cap_pad % d == 0)

    def run(row0):
        for g in range(EXPERTS_PER_STEP):
            row = row0 + g

            def body(i, carry):
                toks = [idx_ref[0, 0, row * slots_pad + i * group + k] for k in range(group)]
                tiles = [h[pl.ds(pl.multiple_of(t * SUBLANES, SUBLANES), SUBLANES), :]
                         for t in toks]
                for k, tile in enumerate(tiles):
                    s0 = pl.multiple_of((i * group + k) * SUBLANES, SUBLANES)
                    zx_ref[pl.ds(s0, SUBLANES), :] = tile
                return carry

            lax.fori_loop(0, cap_pad // group, body, 0)
            u = _rms_chunks(_chunks(zx_ref, 0, cap_pad), g_ref)
            xs_ref[g, 0] = jnp.concatenate(u, axis=1).astype(BF16)

    _per_smem_phase(pl.program_id(1), run)


def _gather(idx_smem, h, g, *, cap_pad):
    b = h.shape[0]
    slots_pad = idx_smem.shape[-1] // SMEM_ROWS
    return pl.pallas_call(
        functools.partial(_gather_kernel, cap_pad=cap_pad, slots_pad=slots_pad),
        grid=(b, N_EXPERTS // EXPERTS_PER_STEP),
        in_specs=[
            _smem_spec(slots_pad),
            pl.BlockSpec((1,) + h.shape[1:], lambda bi, e: (bi, 0, 0)),
            pl.BlockSpec((1, D_MODEL), lambda bi, e: (0, 0)),
        ],
        out_specs=pl.BlockSpec((EXPERTS_PER_STEP, 1, cap_pad, D_MODEL),
                               lambda bi, e: (e, bi, 0, 0)),
        out_shape=jax.ShapeDtypeStruct((N_EXPERTS, b, cap_pad, D_MODEL), BF16),
        scratch_shapes=[pltpu.VMEM((cap_pad * SUBLANES, LANES), F32)],
        compiler_params=_params(56, "parallel", "arbitrary"),
        name="moe_gather",
    )(idx_smem, h, g.reshape(1, D_MODEL))


def _ffn_kernel(x_ref, wg_ref, wu_ref, wd_ref, o_ref, hid_ref, wdb_ref, *, ft, n_f):
    r = pl.program_id(1)
    f = pl.program_id(2)

    @pl.when(r == 0)
    def _():
        wdb_ref[pl.ds(pl.multiple_of(f * ft, ft), ft), :] = wd_ref[0, 0].astype(BF16)

    x = x_ref[0]
    hg = jnp.dot(x, wg_ref[0, 0].astype(BF16), preferred_element_type=F32)
    hu = jnp.dot(x, wu_ref[0, 0].astype(BF16), preferred_element_type=F32)
    hid_ref[f] = (hg * jax.nn.sigmoid(hg) * hu).astype(BF16)

    @pl.when(f == n_f - 1)
    def _():
        hid = jnp.concatenate([hid_ref[k] for k in range(n_f)], axis=1)
        y = jnp.dot(hid, wdb_ref[...], preferred_element_type=F32)
        _store_chunks(o_ref.at[0], 0, y.shape[0],
                      [y[:, j * LANES:(j + 1) * LANES] for j in range(N_CHUNKS)])


def _ffn(xs, w_gate, w_up, w_down, layer):
    n_exp, rows, _ = xs.shape
    d_expert = w_gate.shape[-1]
    rt = rows // FFN_ROW_TILES
    ft = min(FFN_F_TILE, d_expert)
    n_f = d_expert // ft

    def wd_tile(e, r, f):
        return (layer, e, jnp.where(r == 0, f, n_f - 1), 0)

    return pl.pallas_call(
        functools.partial(_ffn_kernel, ft=ft, n_f=n_f),
        grid=(n_exp, FFN_ROW_TILES, n_f),
        in_specs=[
            pl.BlockSpec((1, rt, D_MODEL), lambda e, r, f: (e, r, 0)),
            pl.BlockSpec((1, 1, D_MODEL, ft), lambda e, r, f: (layer, e, 0, f)),
            pl.BlockSpec((1, 1, D_MODEL, ft), lambda e, r, f: (layer, e, 0, f)),
            pl.BlockSpec((1, 1, ft, D_MODEL), wd_tile),
        ],
        out_specs=pl.BlockSpec((1, rt * SUBLANES, LANES), lambda e, r, f: (e, r, 0)),
        out_shape=jax.ShapeDtypeStruct((n_exp, rows * SUBLANES, LANES), F32),
        scratch_shapes=[pltpu.VMEM((n_f, rt, ft), BF16),
                        pltpu.VMEM((d_expert, D_MODEL), BF16)],
        compiler_params=_params(60, "parallel", "arbitrary", "arbitrary"),
        name="moe_ffn",
    )(xs, w_gate, w_up, w_down)


def _scatter_kernel(idx_ref, aff_ref, ys_ref, h_hbm, o_hbm, acc0, acc1, load_sem, store_sem,
                    *, cap, slots_pad, seq_pad):
    b = pl.program_id(0)
    e = pl.program_id(1)
    n_b = pl.num_programs(0)
    n_e = pl.num_programs(1)
    accs = (acc0, acc1)

    def load(seq_i, slot):
        return pltpu.make_async_copy(h_hbm.at[seq_i], accs[slot], load_sem.at[slot])

    def store(seq_i, slot):
        return pltpu.make_async_copy(accs[slot], o_hbm.at[seq_i], store_sem.at[slot])

    def run(slot):
        acc = accs[slot]
        other = 1 - slot

        @pl.when((b == 0) & (e == 0))
        def _():
            load(b, slot).start()

        @pl.when(e == 0)
        def _():
            load(b, slot).wait()

        @pl.when((e == SCATTER_PREFETCH_STEP) & (b >= 1))
        def _():
            store(b - 1, other).wait()

        @pl.when((e == SCATTER_PREFETCH_STEP) & (b + 1 < n_b))
        def _():
            load(b + 1, other).start()

        def add_experts(row0):
            for g in range(EXPERTS_PER_STEP):
                row = row0 + g
                ys = ys_ref.at[g, 0]

                def add_rows(slots):
                    rows, vals = [], []
                    for s in slots:
                        t = idx_ref[0, 0, row * slots_pad + s]
                        gate = aff_ref[0, 0, row * seq_pad + t]
                        r0 = pl.multiple_of(t * SUBLANES, SUBLANES)
                        s0 = (s * SUBLANES if isinstance(s, int)
                              else pl.multiple_of(s * SUBLANES, SUBLANES))
                        vals.append(acc[pl.ds(r0, SUBLANES), :]
                                    + gate * ys[pl.ds(s0, SUBLANES), :])
                        rows.append(r0)
                    for r0, v in zip(rows, vals):
                        acc[pl.ds(r0, SUBLANES), :] = v

                def body(i, carry):
                    add_rows([i * SCATTER_GROUP + k for k in range(SCATTER_GROUP)])
                    return carry

                n_groups = cap // SCATTER_GROUP
                lax.fori_loop(0, n_groups, body, 0)
                if cap % SCATTER_GROUP:
                    add_rows(list(range(n_groups * SCATTER_GROUP, cap)))

        _per_smem_phase(e, add_experts)

        @pl.when(e == n_e - 1)
        def _():
            store(b, slot).start()

        @pl.when((e == n_e - 1) & (b == n_b - 1))
        def _():
            store(b, slot).wait()

    for slot in range(2):
        pl.when(b % 2 == slot)(functools.partial(run, slot))


def _scatter(idx_smem, aff_smem, ys, h, *, cap, cap_pad):
    b = h.shape[0]
    slots_pad = idx_smem.shape[-1] // SMEM_ROWS
    seq_pad = aff_smem.shape[-1] // SMEM_ROWS
    n_steps = N_EXPERTS // EXPERTS_PER_STEP
    assert 1 <= SCATTER_PREFETCH_STEP <= n_steps - 1
    return pl.pallas_call(
        functools.partial(_scatter_kernel, cap=cap, slots_pad=slots_pad, seq_pad=seq_pad),
        grid=(b, n_steps),
        in_specs=[
            _smem_spec(slots_pad),
            _smem_spec(seq_pad),
            pl.BlockSpec((EXPERTS_PER_STEP, 1, cap_pad * SUBLANES, LANES),
                         lambda bi, e: (e, bi, 0, 0)),
            pl.BlockSpec(memory_space=pl.ANY),
        ],
        out_specs=pl.BlockSpec(memory_space=pl.ANY),
        out_shape=jax.ShapeDtypeStruct(h.shape, F32),
        scratch_shapes=[pltpu.VMEM(h.shape[1:], F32), pltpu.VMEM(h.shape[1:], F32),
                        pltpu.SemaphoreType.DMA((2,)), pltpu.SemaphoreType.DMA((2,))],
        compiler_params=_params(56, "arbitrary", "arbitrary"),
        name="moe_scatter",
    )(idx_smem, aff_smem, ys, h)


def _moe_layer(h, aff_t, g, w_gate, w_up, w_down, layer, *, seq):
    b = h.shape[0]
    cap = CAPACITY_FACTOR * seq // N_EXPERTS
    cap_pad = _round_up(cap, BF16_ROWS)
    assert (b * cap_pad) % (FFN_ROW_TILES * BF16_ROWS) == 0
    slots_pad = _round_up(cap_pad, LANES)
    seq_pad = aff_t.shape[-1]
    idx = _topk(aff_t, seq=seq, cap=cap, cap_pad=cap_pad, slots_pad=slots_pad)
    idx_smem = idx.reshape(b * N_EXPERTS // SMEM_ROWS, 1, SMEM_ROWS * slots_pad)
    aff_smem = aff_t.reshape(b * N_EXPERTS // SMEM_ROWS, 1, SMEM_ROWS * seq_pad)
    xs = _gather(idx_smem, h, g, cap_pad=cap_pad)
    ys = _ffn(xs.reshape(N_EXPERTS, b * cap_pad, D_MODEL), w_gate, w_up, w_down, layer)
    ys = ys.reshape(N_EXPERTS, b, cap_pad * SUBLANES, LANES)
    return _scatter(idx_smem, aff_smem, ys, h, cap=cap, cap_pad=cap_pad)


def _final_kernel(ha_ref, hb_ref, g_ref, o_ref, *, tt):
    ha = ha_ref.at[0]
    hb = hb_ref.at[0]
    ch = [jnp.concatenate(
        [ha[pl.ds(N_META * SUBLANES + j, tt - N_META, stride=SUBLANES), :],
         hb[pl.ds(j, N_META, stride=SUBLANES), :]], axis=0) for j in range(N_CHUNKS)]
    o_ref[0] = jnp.concatenate(_rms_chunks(ch, g_ref), axis=1)


def _final(h, g, *, seq, tt):
    b = h.shape[0]
    s_out = seq - N_META
    assert s_out % tt == 0
    assert tt % N_META == 0
    return pl.pallas_call(
        functools.partial(_final_kernel, tt=tt),
        grid=(b, s_out // tt),
        in_specs=[
            pl.BlockSpec((1, tt * SUBLANES, LANES), lambda bi, i: (bi, i, 0)),
            pl.BlockSpec((1, N_META * SUBLANES, LANES),
                         lambda bi, i: (bi, (i + 1) * (tt // N_META), 0)),
            pl.BlockSpec((1, D_MODEL), lambda bi, i: (0, 0)),
        ],
        out_specs=pl.BlockSpec((1, tt, D_MODEL), lambda bi, i: (bi, i, 0)),
        out_shape=jax.ShapeDtypeStruct((b, s_out, D_MODEL), F32),
        compiler_params=_params(40, "parallel", "parallel"),
        name="final_norm",
    )(h, h, g.reshape(1, D_MODEL))


def kernel(x, meta_tokens, norm_mix, norm_ffn, norm_final, pool_w, pool_scale, lru_w_in, lru_conv_w, lru_conv_b, lru_w_gates, lru_b_gates, lru_lambda, lru_w_out, router_w, moe_w_gate, moe_w_up, moe_w_down):
    b, s, d = x.shape
    assert d == D_MODEL
    seq = s + N_META
    tt = TIME_TILE
    depth = norm_mix.shape[0]
    h = _embed(x, meta_tokens.astype(x.dtype), tt=tt)
    for i in range(depth):
        j = i // 2
        if i % 2 == 0:
            h, aff_t = _pool_layer(h, norm_mix[i], pool_w[j], pool_scale[j],
                                   norm_ffn[i], router_w[i], seq=seq, tt=tt)
        else:
            h, aff_t = _lru_layer(h, norm_mix[i], lru_w_in[j], lru_conv_w[j], lru_conv_b[j],
                                  lru_w_gates[j], lru_b_gates[j], lru_lambda[j], lru_w_out[j],
                                  norm_ffn[i], router_w[i], seq=seq, tt=tt)
        h = _moe_layer(h, aff_t, norm_ffn[i], moe_w_gate, moe_w_up, moe_w_down, i, seq=seq)
    return _final(h, norm_final, seq=seq, tt=tt)
```
